```python
import math
import jax, jax.numpy as jnp
from jax import lax
import numpy as np

D_MODEL = 1024
BATCH = 2
SEQ = 8192
DEPTH = 1
DEC_BATCH = 32
DEC_SEQ = 8
PAST_LEN = 8192
PAGE_SIZE = 128

HEAD_DIM = 64
NSA_HEADS = 8
NSA_KV_HEADS = 2
NSA_GROUP = NSA_HEADS // NSA_KV_HEADS
CMP_BLOCK = 32
CMP_STRIDE = 16
CMP_HIDDEN = 2 * HEAD_DIM
SEL_BLOCK = 64
N_SELECT = 16
WINDOW = 512
FOX_HEADS = 8
N_BUCKETS = 32
MAX_DISTANCE = 128
D_FF = 2816
CONV_W = 3
PLE_DIM = 256
Q_BLOCK = 128
ALPHA = (2.0 * DEPTH) ** 0.25
BETA = (8.0 * DEPTH) ** -0.25
LN_EPS = 1e-5
NSA_Q_W = NSA_HEADS * HEAD_DIM
NSA_KV_W = NSA_KV_HEADS * HEAD_DIM
NSA_GATE_W = NSA_HEADS * 3
FOX_W = FOX_HEADS * HEAD_DIM
IN_SPLITS = (NSA_Q_W, 6 * NSA_KV_W, NSA_GATE_W, FOX_W, FOX_W, FOX_W, FOX_HEADS, D_MODEL, D_MODEL)
D_IN = sum(IN_SPLITS)

kernel_name = "hybrid_nsa_fox_decoder_step"


def layer_norm(x, g, b):
    xf = x.astype(jnp.float32)
    mu = jnp.mean(xf, axis=-1, keepdims=True)
    var = jnp.mean(jnp.square(xf - mu), axis=-1, keepdims=True)
    y = (xf - mu) * lax.rsqrt(var + LN_EPS) * g.astype(jnp.float32) + b.astype(jnp.float32)
    return y.astype(x.dtype)


def masked_softmax(logits, mask):
    l = jnp.where(mask, logits.astype(jnp.float32), -jnp.inf)
    m = jnp.max(l, axis=-1, keepdims=True)
    m = jnp.where(jnp.isfinite(m), m, 0.0)
    e = jnp.where(mask, jnp.exp(l - m), 0.0)
    return e / jnp.maximum(jnp.sum(e, axis=-1, keepdims=True), 1e-30)


def t5_bucket(dist):
    n = jnp.maximum(dist, 0)
    max_exact = N_BUCKETS // 2
    nf = jnp.maximum(n, 1).astype(jnp.float32)
    large = max_exact + (jnp.log(nf / max_exact) / math.log(MAX_DISTANCE / max_exact)
                         * (N_BUCKETS - max_exact)).astype(jnp.int32)
    large = jnp.minimum(large, N_BUCKETS - 1)
    return jnp.where(n < max_exact, n, large)


def to_blocks(a):
    B, T = a.shape[:2]
    return jnp.swapaxes(a.reshape((B, T // Q_BLOCK, Q_BLOCK) + a.shape[2:]), 0, 1)


def from_blocks(a):
    a = jnp.swapaxes(a, 0, 1)
    return a.reshape((a.shape[0], a.shape[1] * a.shape[2]) + a.shape[3:])


def gather_pages(cache, page_table):
    g = cache[page_table]
    return g.reshape((g.shape[0], g.shape[1] * g.shape[2]) + g.shape[3:])


def project(h, w_in, b_forget):
    B, T, _ = h.shape
    z = h @ w_in
    idx = np.cumsum(IN_SPLITS)[:-1].tolist()
    nq, nkv, ng, fq, fk, fv, ff, ga, gb = jnp.split(z, idx, axis=-1)
    nq = nq.reshape(B, T, NSA_KV_HEADS, NSA_GROUP, HEAD_DIM)
    nkv = nkv.reshape(B, T, 6, NSA_KV_HEADS, HEAD_DIM)
    ng = jax.nn.sigmoid(ng.reshape(B, T, NSA_KV_HEADS, NSA_GROUP, 3))
    fq = fq.reshape(B, T, FOX_HEADS, HEAD_DIM)
    fk = fk.reshape(B, T, FOX_HEADS, HEAD_DIM)
    fv = fv.reshape(B, T, FOX_HEADS, HEAD_DIM)
    logf = jax.nn.log_sigmoid((ff + b_forget).astype(jnp.float32))
    return nq, nkv, ng, fq, fk, fv, logf, ga, gb


def compress(raw, pe, w1, w2):
    T = raw.shape[1]
    n_c = (T - CMP_BLOCK) // CMP_STRIDE + 1
    idx = np.arange(n_c)[:, None] * CMP_STRIDE + np.arange(CMP_BLOCK)[None, :]
    blocks = raw[:, idx] + pe[None, None, :, None, :]
    hid = jax.nn.gelu(jnp.einsum('bnlhd,ldk->bnhk', blocks, w1))
    return jnp.einsum('bnhk,kd->bnhd', hid, w2), jnp.asarray(idx[:, -1], dtype=jnp.int32)


def sel_blocks(raw):
    B, T = raw.shape[:2]
    n_sel = -(-T // SEL_BLOCK)
    raw = jnp.pad(raw, ((0, 0), (0, n_sel * SEL_BLOCK - T), (0, 0), (0, 0)))
    return raw.reshape(B, n_sel, SEL_BLOCK, NSA_KV_HEADS, HEAD_DIM)


def nsa_prepare(rows, W):
    kc, cpos = compress(rows[:, :, 0], W['nsa_cmp_pe'][0], W['nsa_cmp_w1'][0], W['nsa_cmp_w2'][0])
    vc, _ = compress(rows[:, :, 1], W['nsa_cmp_pe'][1], W['nsa_cmp_w1'][1], W['nsa_cmp_w2'][1])
    ks = sel_blocks(rows[:, :, 2])
    vs = sel_blocks(rows[:, :, 3])
    n_c, n_sel = kc.shape[1], ks.shape[1]
    start = np.arange(n_c)[:, None] * CMP_STRIDE
    j = np.arange(n_sel)[None, :]
    ov = jnp.asarray((start < (j + 1) * SEL_BLOCK) & (start + CMP_BLOCK > j * SEL_BLOCK), dtype=jnp.float32)
    return kc, vc, cpos, ks, vs, ov


def nsa_block(q, qpos, g, kc, vc, cpos, ks, vs, ov, kw, vw, wpos, tb):
    B, Tq = q.shape[:2]
    scale = HEAD_DIM ** -0.5
    dc = qpos[:, None] - cpos[None, :]
    lc = jnp.einsum('bthgd,bnhd->bhgtn', q, kc).astype(jnp.float32) * scale
    lc = lc + jnp.transpose(tb[t5_bucket(dc)], (2, 3, 0, 1))
    pc = masked_softmax(lc, dc >= 0)
    oc = jnp.einsum('bhgtn,bnhd->bthgd', pc.astype(vc.dtype), vc)
    n_sel = ks.shape[1]
    imp = jnp.einsum('bhgtn,nj->bhtj', pc, ov)
    blk = jnp.arange(n_sel, dtype=jnp.int32)[None, :]
    qblk = (qpos // SEL_BLOCK)[:, None]
    forced = (blk == 0) | (blk == qblk) | (blk == qblk - 1)
    imp = jnp.where(forced, jnp.inf, jnp.where(blk > qblk, -jnp.inf, imp))
    top_s, top_i = lax.top_k(imp, min(N_SELECT, n_sel))
    valid = top_s > -jnp.inf
    bi = jnp.arange(B)[:, None, None, None]
    hi = jnp.arange(NSA_KV_HEADS)[None, :, None, None]
    kg = jnp.transpose(ks, (0, 3, 1, 2, 4))[bi, hi, top_i]
    vg = jnp.transpose(vs, (0, 3, 1, 2, 4))[bi, hi, top_i]
    kpos = top_i[..., None] * SEL_BLOCK + jnp.arange(SEL_BLOCK, dtype=jnp.int32)
    ds = qpos[None, None, :, None, None] - kpos
    bias_s = jax.vmap(lambda tbh, bk: tbh[bk], in_axes=(1, 1), out_axes=1)(tb, t5_bucket(ds))
    ls = jnp.einsum('bthgd,bhtnsd->bhgtns', q, kg).astype(jnp.float32) * scale + jnp.moveaxis(bias_s, -1, 2)
    n_k = top_i.shape[-1] * SEL_BLOCK
    ms = (valid[..., None] & (ds >= 0)).reshape(B, NSA_KV_HEADS, 1, Tq, n_k)
    ps = masked_softmax(ls.reshape(B, NSA_KV_HEADS, NSA_GROUP, Tq, n_k), ms)
    osl = jnp.einsum('bhgtm,bhtmd->bthgd', ps.astype(vs.dtype), vg.reshape(B, NSA_KV_HEADS, Tq, n_k, HEAD_DIM))
    dw = qpos[:, None] - wpos[None, :]
    lw = jnp.einsum('bthgd,bshd->bhgts', q, kw).astype(jnp.float32) * scale
    lw = lw + jnp.transpose(tb[t5_bucket(dw)], (2, 3, 0, 1))
    pw = masked_softmax(lw, (dw >= 0) & (dw < WINDOW) & (wpos[None, :] >= 0))
    ow = jnp.einsum('bhgts,bshd->bthgd', pw.astype(vw.dtype), vw)
    return g[..., 0:1] * oc + g[..., 1:2] * osl + g[..., 2:3] * ow


def fox_block(q, qpos, cq, k, v, ck, kpos):
    scale = HEAD_DIM ** -0.5
    l = jnp.einsum('bthd,bshd->bhts', q, k).astype(jnp.float32) * scale
    l = l + jnp.transpose(cq, (0, 2, 1))[..., None] - jnp.transpose(ck, (0, 2, 1))[:, :, None, :]
    p = masked_softmax(l, kpos[None, :] <= qpos[:, None])
    return jnp.einsum('bhts,bshd->bthd', p.astype(v.dtype), v)


def merge(o_nsa, o_fox, ga, gb, W):
    B, T = o_nsa.shape[:2]
    a = o_nsa.reshape(B, T, NSA_Q_W) @ W['w_branch_a']
    b = o_fox.reshape(B, T, FOX_W) @ W['w_branch_b']
    return (jax.nn.sigmoid(ga) * a + jax.nn.sigmoid(gb) * b) @ W['w_out']


def finish_layer(x, mix, p_emb, conv_prev, W):
    x = layer_norm(ALPHA * x + mix, W['ln1_g'], W['ln1_b'])
    T = x.shape[1]
    u = x @ W['w_ffn_up']
    ext = jnp.concatenate([conv_prev.astype(u.dtype), u], axis=1)
    conv = W['ffn_conv_b'] + sum(W['ffn_conv_w'][k] * ext[:, k:k + T] for k in range(CONV_W))
    f = (jax.nn.gelu(conv) * (x @ W['w_ffn_gate'])) @ W['w_ffn_down']
    x = layer_norm(ALPHA * x + f, W['ln2_g'], W['ln2_b'])
    x = x + jax.nn.sigmoid(x @ W['w_ple_gate']) * (p_emb @ W['w_ple'])
    return x, ext[:, T:]


def prompt_layer(x, p_emb, W, tb):
    B, T = x.shape[:2]
    nq, nkv, ng, fq, fk, fv, logf, ga, gb = project(x, W['w_in'], W['b_forget'])
    kc, vc, cpos, ks, vs, ov = nsa_prepare(nkv[:, :, :4], W)
    kw_pad = jnp.pad(nkv[:, :, 4:], ((0, 0), (WINDOW, 0), (0, 0), (0, 0), (0, 0)))
    c = jnp.cumsum(logf, axis=1)
    kpos = jnp.arange(T, dtype=jnp.int32)

    def block(args):
        i, q_b, g_b, fq_b, c_b = args
        t0 = i * Q_BLOCK
        qpos = t0 + jnp.arange(Q_BLOCK, dtype=jnp.int32)
        kw_b = lax.dynamic_slice_in_dim(kw_pad, t0, WINDOW + Q_BLOCK, axis=1)
        wpos = t0 - WINDOW + jnp.arange(WINDOW + Q_BLOCK, dtype=jnp.int32)
        o_n = nsa_block(q_b, qpos, g_b, kc, vc, cpos, ks, vs, ov, kw_b[:, :, 0], kw_b[:, :, 1], wpos, tb)
        o_f = fox_block(fq_b, qpos, c_b, fk, fv, c, kpos)
        return o_n, o_f

    nb = T // Q_BLOCK
    o_n, o_f = lax.map(block, (jnp.arange(nb, dtype=jnp.int32), to_blocks(nq), to_blocks(ng),
                                to_blocks(fq), to_blocks(c)))
    mix = merge(from_blocks(o_n), from_blocks(o_f), ga, gb, W)
    y, conv_state = finish_layer(x, mix, p_emb, jnp.zeros((B, CONV_W - 1, D_FF), x.dtype), W)
    wb = min(WINDOW, T)
    return y, (jnp.stack([fk, fv], axis=2), logf, nkv[:, :, :4], nkv[:, T - wb:, 4:], conv_state)


def sample_layer(x, p_emb, c_fox_kv, c_fox_logf, c_nsa, win, conv_prev, page_table, W, tb):
    S = x.shape[1]
    nq, nkv, ng, fq, fk, fv, logf, ga, gb = project(x, W['w_in'], W['b_forget'])
    past_nsa = gather_pages(c_nsa, page_table)
    P = past_nsa.shape[1]
    qpos = P + jnp.arange(S, dtype=jnp.int32)
    kc, vc, cpos, ks, vs, ov = nsa_prepare(jnp.concatenate([past_nsa, nkv[:, :, :4]], axis=1), W)
    wb = win.shape[1]
    win_all = jnp.concatenate([win, nkv[:, :, 4:]], axis=1)
    wpos = P - wb + jnp.arange(wb + S, dtype=jnp.int32)
    o_n = nsa_block(nq, qpos, ng, kc, vc, cpos, ks, vs, ov, win_all[:, :, 0], win_all[:, :, 1], wpos, tb)
    past_fox = gather_pages(c_fox_kv, page_table)
    k_all = jnp.concatenate([past_fox[:, :, 0], fk], axis=1)
    v_all = jnp.concatenate([past_fox[:, :, 1], fv], axis=1)
    logf_all = jnp.concatenate([gather_pages(c_fox_logf, page_table).astype(jnp.float32), logf], axis=1)
    c = jnp.cumsum(logf_all, axis=1)
    o_f = fox_block(fq, qpos, c[:, P:], k_all, v_all, c, jnp.arange(P + S, dtype=jnp.int32))
    mix = merge(o_n, o_f, ga, gb, W)
    y, conv_state = finish_layer(x, mix, p_emb, conv_prev, W)
    return y, (jnp.stack([fk, fv], axis=2), logf, nkv[:, :, :4], win_all[:, S:], conv_state)


def setup_inputs(seed: int = 0) -> dict:
    key = jax.random.key(seed)
    ks = jax.random.split(key, 40)
    n_pages = PAST_LEN // PAGE_SIZE
    n_pool = (5 * DEC_BATCH * n_pages) // 4
    wb = min(WINDOW, PAST_LEN)

    def nrm(k, shape, s=1.0):
        return s * jax.random.normal(k, shape, jnp.float32)

    page_table = jax.random.permutation(ks[9], n_pool)[: DEC_BATCH * n_pages].reshape(DEC_BATCH, n_pages).astype(jnp.int32)
    return {
        "x_prompt": nrm(ks[0], (BATCH, SEQ, D_MODEL)),
        "x_sample": nrm(ks[1], (DEC_BATCH, DEC_SEQ, D_MODEL)),
        "p_prompt": nrm(ks[2], (DEPTH, BATCH, SEQ, PLE_DIM)),
        "p_sample": nrm(ks[3], (DEPTH, DEC_BATCH, DEC_SEQ, PLE_DIM)),
        "cache_fox_kv": nrm(ks[4], (DEPTH, n_pool, PAGE_SIZE, 2, FOX_HEADS, HEAD_DIM)),
        "cache_fox_logf": jax.nn.log_sigmoid(4.0 + nrm(ks[5], (DEPTH, n_pool, PAGE_SIZE, FOX_HEADS))),
        "cache_nsa_kv": nrm(ks[6], (DEPTH, n_pool, PAGE_SIZE, 4, NSA_KV_HEADS, HEAD_DIM)),
        "state_nsa_win": nrm(ks[7], (DEPTH, DEC_BATCH, wb, 2, NSA_KV_HEADS, HEAD_DIM)),
        "state_ffn_conv": nrm(ks[8], (DEPTH, DEC_BATCH, CONV_W - 1, D_FF)),
        "page_table": page_table,
        "w_in": nrm(ks[10], (DEPTH, D_MODEL, D_IN), D_MODEL ** -0.5),
        "b_forget": 4.0 + nrm(ks[11], (DEPTH, FOX_HEADS), 0.5),
        "nsa_cmp_pe": nrm(ks[12], (DEPTH, 2, CMP_BLOCK, HEAD_DIM), 0.1),
        "nsa_cmp_w1": nrm(ks[13], (DEPTH, 2, CMP_BLOCK, HEAD_DIM, CMP_HIDDEN), (CMP_BLOCK * HEAD_DIM) ** -0.5),
        "nsa_cmp_w2": nrm(ks[14], (DEPTH, 2, CMP_HIDDEN, HEAD_DIM), CMP_HIDDEN ** -0.5),
        "rel_bias": nrm(ks[15], (N_BUCKETS, NSA_HEADS), 0.5),
        "w_branch_a": nrm(ks[16], (DEPTH, NSA_Q_W, D_MODEL), NSA_Q_W ** -0.5),
        "w_branch_b": nrm(ks[17], (DEPTH, FOX_W, D_MODEL), FOX_W ** -0.5),
        "w_out": nrm(ks[18], (DEPTH, D_MODEL, D_MODEL), BETA * D_MODEL ** -0.5),
        "ln1_g": 1.0 + nrm(ks[19], (DEPTH, D_MODEL), 0.05),
        "ln1_b": nrm(ks[20], (DEPTH, D_MODEL), 0.02),
        "ln2_g": 1.0 + nrm(ks[21], (DEPTH, D_MODEL), 0.05),
        "ln2_b": nrm(ks[22], (DEPTH, D_MODEL), 0.02),
        "w_ffn_up": nrm(ks[23], (DEPTH, D_MODEL, D_FF), D_MODEL ** -0.5),
        "w_ffn_gate": nrm(ks[24], (DEPTH, D_MODEL, D_FF), D_MODEL ** -0.5),
        "ffn_conv_w": nrm(ks[25], (DEPTH, CONV_W, D_FF), CONV_W ** -0.5),
        "ffn_conv_b": nrm(ks[26], (DEPTH, D_FF), 0.02),
        "w_ffn_down": nrm(ks[27], (DEPTH, D_FF, D_MODEL), BETA * D_FF ** -0.5),
        "w_ple": nrm(ks[28], (DEPTH, PLE_DIM, D_MODEL), PLE_DIM ** -0.5),
        "w_ple_gate": nrm(ks[29], (DEPTH, D_MODEL, D_MODEL), D_MODEL ** -0.5),
    }


def reference(x_prompt, x_sample, p_prompt, p_sample, cache_fox_kv, cache_fox_logf, cache_nsa_kv,
              state_nsa_win, state_ffn_conv, page_table, w_in, b_forget, nsa_cmp_pe, nsa_cmp_w1,
              nsa_cmp_w2, rel_bias, w_branch_a, w_branch_b, w_out, ln1_g, ln1_b, ln2_g, ln2_b,
              w_ffn_up, w_ffn_gate, ffn_conv_w, ffn_conv_b, w_ffn_down, w_ple, w_ple_gate):
    tb = rel_bias.reshape(N_BUCKETS, NSA_KV_HEADS, NSA_GROUP)
    xp, xs = x_prompt, x_sample
    st_p, st_s = [], []
    for i in range(DEPTH):
        W = {
            'w_in': w_in[i], 'b_forget': b_forget[i], 'nsa_cmp_pe': nsa_cmp_pe[i],
            'nsa_cmp_w1': nsa_cmp_w1[i], 'nsa_cmp_w2': nsa_cmp_w2[i], 'w_branch_a': w_branch_a[i],
            'w_branch_b': w_branch_b[i], 'w_out': w_out[i], 'ln1_g': ln1_g[i], 'ln1_b': ln1_b[i],
            'ln2_g': ln2_g[i], 'ln2_b': ln2_b[i], 'w_ffn_up': w_ffn_up[i], 'w_ffn_gate': w_ffn_gate[i],
            'ffn_conv_w': ffn_conv_w[i], 'ffn_conv_b': ffn_conv_b[i], 'w_ffn_down': w_ffn_down[i],
            'w_ple': w_ple[i], 'w_ple_gate': w_ple_gate[i],
        }
        xp, sp = prompt_layer(xp, p_prompt[i], W, tb)
        xs, ss = sample_layer(xs, p_sample[i], cache_fox_kv[i], cache_fox_logf[i], cache_nsa_kv[i],
                              state_nsa_win[i], state_ffn_conv[i], page_table, W, tb)
        st_p.append(sp)
        st_s.append(ss)

    def stk(lst, j):
        return jnp.stack([s[j] for s in lst])

    return (xp, xs, stk(st_p, 0), stk(st_s, 0), stk(st_p, 1), stk(st_s, 1), stk(st_p, 2), stk(st_s, 2),
            stk(st_p, 3), stk(st_s, 3), stk(st_p, 4), stk(st_s, 4))
```

```python
import functools
import math

import numpy as np
import jax
import jax.numpy as jnp
from jax import lax
from jax.experimental import pallas as pl
from jax.experimental.pallas import tpu as pltpu

F32 = jnp.float32
BF16 = jnp.bfloat16
MXU = jnp.bfloat16

HEAD_DIM = 64
KVH = 2
GRP = 4
FOX_HEADS = 8
CMP_BLOCK = 32
CMP_STRIDE = 16
SEL_BLOCK = 64
N_SELECT = 16
WINDOW = 512
N_BUCKETS = 32
MAX_DISTANCE = 128
PAGE = 128
DEPTH = 1
ALPHA = (2.0 * DEPTH) ** 0.25
LN_EPS = 1e-5
SCALE = HEAD_DIM ** -0.5
NSA_Q_W = KVH * GRP * HEAD_DIM
NSA_KV_W = KVH * HEAD_DIM
FOX_W = FOX_HEADS * HEAD_DIM
LANE = 128
VMEM_LIMIT = 56 * 1024 * 1024
NEG_INF = float("-inf")


def _cparams(sem):
    return pltpu.CompilerParams(dimension_semantics=sem, vmem_limit_bytes=VMEM_LIMIT)


def _dot(a, b):
    return jnp.dot(a, b, preferred_element_type=F32)


def _dot_nt(a, b):
    return lax.dot_general(a, b, (((1,), (1,)), ((), ())), preferred_element_type=F32)


def _split3(x):
    hi = x.astype(BF16)
    r = x - hi.astype(F32)
    mid = r.astype(BF16)
    lo = (r - mid.astype(F32)).astype(BF16)
    return hi, mid, lo


def _dot01_exact(sel01, x):
    return sum(_dot(sel01, t) for t in _split3(x))


def _iota(shape, dim):
    return lax.broadcasted_iota(jnp.int32, shape, dim)


def _softmax_rows(lg):
    m = jnp.max(lg, axis=-1, keepdims=True)
    m = jnp.where(m == NEG_INF, 0.0, m)
    e = jnp.exp(lg - m)
    d = jnp.maximum(jnp.sum(e, axis=-1, keepdims=True), 1e-30)
    return e * (1.0 / d)


def _online_update(lg, m_ref, l_ref):
    m_old = m_ref[...]
    m_new = jnp.maximum(m_old, jnp.max(lg, axis=-1, keepdims=True))
    m_safe = jnp.where(m_new == NEG_INF, 0.0, m_new)
    alpha = jnp.exp(m_old - m_safe)
    p = jnp.exp(lg - m_safe)
    l_ref[...] = alpha * l_ref[...] + jnp.sum(p, axis=-1, keepdims=True)
    m_ref[...] = m_new
    return alpha, p


def _layer_norm(x, g, b):
    mu = jnp.mean(x, axis=-1, keepdims=True)
    xc = x - mu
    var = jnp.mean(xc * xc, axis=-1, keepdims=True)
    return xc * lax.rsqrt(var + LN_EPS) * g + b


def _bucket_np(dist):
    n = np.maximum(np.asarray(dist), 0)
    max_exact = N_BUCKETS // 2
    nf = np.maximum(n, 1).astype(np.float32)
    large = max_exact + (np.log(nf / np.float32(max_exact)) / np.float32(math.log(MAX_DISTANCE / max_exact))
                         * np.float32(N_BUCKETS - max_exact)).astype(np.int32)
    large = np.minimum(large, N_BUCKETS - 1)
    return np.where(n < max_exact, n, large).astype(np.int32)


def _proj_body(x_ref, wnq, wnkv, wkw, wfq, wfkv, wga, wgb, wsm, bsm,
               nq_o, nkv_o, kw_o, fq_o, fkv_o, ga_o, gb_o, ng_o, logf_o):
    xb = x_ref[...].astype(MXU)
    nq_o[...] = _dot(xb, wnq[...])
    nkv_o[...] = _dot(xb, wnkv[...])
    kw_o[...] = _dot(xb, wkw[...])
    fq_o[...] = _dot(xb, wfq[...])
    fkv_o[...] = _dot(xb, wfkv[...])
    ga_o[...] = jax.nn.sigmoid(_dot(xb, wga[...]))
    gb_o[...] = jax.nn.sigmoid(_dot(xb, wgb[...]))
    sm = _dot(xb, wsm[...])
    ng_o[...] = jax.nn.sigmoid(sm)
    z = sm + bsm[...]
    ls = jnp.minimum(z, 0.0) - jnp.log1p(jnp.exp(-jnp.abs(z)))
    logf_o[...] = ls[:, :FOX_HEADS]


def _proj(x2d, w_in, b_forget, tm):
    m, d = x2d.shape
    o = np.cumsum([0, NSA_Q_W, 6 * NSA_KV_W, KVH * GRP * 3, FOX_W, FOX_W, FOX_W, FOX_HEADS, d, d])
    wb = w_in.astype(MXU)
    wnq = wb[:, o[0]:o[1]]
    wnkv = wb[:, o[1]:o[1] + 4 * NSA_KV_W]
    wkw = wb[:, o[1] + 4 * NSA_KV_W:o[2]]
    wfq = wb[:, o[3]:o[4]]
    wfkv = wb[:, o[4]:o[6]]
    wga = wb[:, o[7]:o[8]]
    wgb = wb[:, o[8]:o[9]]
    nsm = FOX_HEADS + KVH * GRP * 3
    wsm = jnp.concatenate([wb[:, o[6]:o[7]], wb[:, o[2]:o[3]], jnp.zeros((d, LANE - nsm), MXU)], axis=1)
    bsm = jnp.concatenate([b_forget.astype(F32), jnp.zeros((LANE - FOX_HEADS,), F32)]).reshape(1, LANE)
    ws = [wnq, wnkv, wkw, wfq, wfkv, wga, wgb, wsm]
    widths = [w.shape[1] for w in ws] + [FOX_HEADS]
    row = lambda n: pl.BlockSpec((tm, n), lambda i: (i, 0))
    full = lambda a: pl.BlockSpec(a.shape, lambda i: (0, 0))
    out_shapes = [jax.ShapeDtypeStruct((m, n), F32) for n in widths]
    return pl.pallas_call(
        _proj_body,
        grid=(m // tm,),
        in_specs=[row(d)] + [full(w) for w in ws] + [full(bsm)],
        out_specs=[row(n) for n in widths],
        out_shape=out_shapes,
        compiler_params=_cparams(("parallel",)),
        name="proj",
    )(x2d, *ws, bsm)


def _cumsum_body(pt_ref, x_ref, c_o, ct_o, pad_s, carry_s):
    p = pl.program_id(1)

    @pl.when(p == 0)
    def _():
        pad_s[...] = jnp.zeros_like(pad_s)
        carry_s[...] = jnp.zeros_like(carry_s)

    pad_s[:, 0:FOX_HEADS] = x_ref[0]
    xp = pad_s[...]
    ltri = (_iota((PAGE, PAGE), 1) <= _iota((PAGE, PAGE), 0)).astype(BF16)
    cs = _dot01_exact(ltri, xp) + carry_s[...]
    c_o[0] = cs[:, 0:FOX_HEADS]
    carry_s[...] = cs[PAGE - 1:PAGE, :]
    ct_o[0] = cs.T[0:FOX_HEADS, :]


def _paged_cumsum(pool, page_table):
    b, npg = page_table.shape
    grid_spec = pltpu.PrefetchScalarGridSpec(
        num_scalar_prefetch=1,
        grid=(b, npg),
        in_specs=[pl.BlockSpec((1, PAGE, FOX_HEADS), lambda i, p, pt: (pt[i, p], 0, 0))],
        out_specs=[pl.BlockSpec((1, PAGE, FOX_HEADS), lambda i, p, pt: (i, p, 0)),
                   pl.BlockSpec((1, FOX_HEADS, PAGE), lambda i, p, pt: (i, 0, p))],
        scratch_shapes=[pltpu.VMEM((PAGE, LANE), F32), pltpu.VMEM((1, LANE), F32)],
    )
    return pl.pallas_call(
        _cumsum_body,
        grid_spec=grid_spec,
        out_shape=[jax.ShapeDtypeStruct((b, npg * PAGE, FOX_HEADS), F32),
                   jax.ShapeDtypeStruct((b, FOX_HEADS, npg * PAGE), F32)],
        compiler_params=_cparams(("arbitrary", "arbitrary")),
        name="logf_cumsum",
    )(page_table, pool)


def _compress_body(pt_ref, rows_ref, w1_ref, pe_ref, w2_ref, out_ref, buf_k, buf_v, *, nrows):
    p = pl.program_id(1)
    npg = nrows // PAGE
    nb = nrows // CMP_STRIDE
    bufs = (buf_k, buf_v)

    @pl.when(p == 0)
    def _():
        for buf in bufs:
            buf[nrows:nrows + CMP_BLOCK, :] = jnp.zeros((CMP_BLOCK, LANE), F32)

    for typ, buf in enumerate(bufs):
        buf[pl.ds(pl.multiple_of(p * PAGE, PAGE), PAGE), :] = rows_ref[0, :, typ * LANE:(typ + 1) * LANE]

    @pl.when(p == npg - 1)
    def _():
        for typ, buf in enumerate(bufs):
            acc = jnp.zeros((nb, 2 * LANE), F32)
            for l in range(CMP_BLOCK):
                xl = buf[pl.ds(l, nb, stride=CMP_STRIDE), :] + pe_ref[typ, l:l + 1, :]
                acc = acc + _dot(xl.astype(MXU), w1_ref[typ, l])
            hid = jax.nn.gelu(acc)
            out_ref[0, :, typ * LANE:(typ + 1) * LANE] = _dot(hid.astype(MXU), w2_ref[typ])


def _compress(pool, page_table, pe, w1, w2):
    b, npg = page_table.shape
    nrows = npg * PAGE
    nb = nrows // CMP_STRIDE
    hid = w1.shape[-1]
    z = jnp.zeros_like(w1)
    w1bd = jnp.concatenate([jnp.concatenate([w1, z], axis=-1), jnp.concatenate([z, w1], axis=-1)], axis=-2).astype(MXU)
    z2 = jnp.zeros_like(w2)
    w2bd = jnp.concatenate([jnp.concatenate([w2, z2], axis=-1), jnp.concatenate([z2, w2], axis=-1)], axis=-2).astype(MXU)
    pe2 = jnp.concatenate([pe, pe], axis=-1).astype(F32)
    grid_spec = pltpu.PrefetchScalarGridSpec(
        num_scalar_prefetch=1,
        grid=(b, npg),
        in_specs=[pl.BlockSpec((1, PAGE, 2 * LANE), lambda i, p, pt: (pt[i, p], 0, 0)),
                  pl.BlockSpec(w1bd.shape, lambda i, p, pt: (0, 0, 0, 0)),
                  pl.BlockSpec(pe2.shape, lambda i, p, pt: (0, 0, 0)),
                  pl.BlockSpec(w2bd.shape, lambda i, p, pt: (0, 0, 0))],
        out_specs=pl.BlockSpec((1, nb, 2 * LANE), lambda i, p, pt: (i, 0, 0)),
        scratch_shapes=[pltpu.VMEM((nrows + CMP_BLOCK, LANE), F32), pltpu.VMEM((nrows + CMP_BLOCK, LANE), F32)],
    )
    del hid
    return pl.pallas_call(
        functools.partial(_compress_body, nrows=nrows),
        grid_spec=grid_spec,
        out_shape=jax.ShapeDtypeStruct((b, nb, 2 * LANE), F32),
        compiler_params=_cparams(("arbitrary", "arbitrary")),
        name="nsa_compress",
    )(page_table, pool, w1bd, pe2, w2bd)


def _pfox_body(q_ref, k_ref, v_ref, cq_ref, ck_ref, o_ref, m_s, l_s, acc_s, *, tq, tk, nk):
    qi = pl.program_id(2)
    kj = pl.program_id(3)
    last = ((qi + 1) * tq - 1) // tk

    @pl.when(kj == 0)
    def _():
        m_s[...] = jnp.full_like(m_s, NEG_INF)
        l_s[...] = jnp.zeros_like(l_s)
        acc_s[...] = jnp.zeros_like(acc_s)

    @pl.when(kj <= last)
    def _():
        q = q_ref[0] * SCALE
        k = k_ref[0].astype(MXU)
        v = v_ref[0].astype(MXU)
        lane = _iota((tq, LANE), 1)
        causal = (kj * tk + _iota((tq, tk), 1)) <= (qi * tq + _iota((tq, tk), 0))
        cq = cq_ref[0, 0]
        ck = ck_ref[0, 0]
        for hh in range(2):
            qh = jnp.where((lane // HEAD_DIM) == hh, q, 0.0).astype(MXU)
            s = _dot_nt(qh, k)
            lg = s + cq[:, hh:hh + 1] - ck[hh:hh + 1, :]
            lg = jnp.where(causal, lg, NEG_INF)
            alpha, p = _online_update(lg, m_s.at[hh], l_s.at[hh])
            acc_s[hh] = alpha * acc_s[hh] + _dot(p.astype(MXU), v)

    @pl.when(kj == nk - 1)
    def _():
        lane = _iota((tq, LANE), 1)
        o0 = acc_s[0] * (1.0 / l_s[0])
        o1 = acc_s[1] * (1.0 / l_s[1])
        o_ref[0] = jnp.where(lane < HEAD_DIM, o0, o1)


def _prompt_fox(fq, fkv, c4, ct4, tq, tk):
    b, t, _ = fq.shape
    nq, nk = t // tq, t // tk
    hp = FOX_HEADS // 2
    lastf = lambda qi: ((qi + 1) * tq - 1) // tk
    return pl.pallas_call(
        functools.partial(_pfox_body, tq=tq, tk=tk, nk=nk),
        grid=(b, hp, nq, nk),
        in_specs=[pl.BlockSpec((1, tq, LANE), lambda i, h, qi, kj: (i, qi, h)),
                  pl.BlockSpec((1, tk, LANE), lambda i, h, qi, kj: (i, jnp.minimum(kj, lastf(qi)), h)),
                  pl.BlockSpec((1, tk, LANE), lambda i, h, qi, kj: (i, jnp.minimum(kj, lastf(qi)), hp + h)),
                  pl.BlockSpec((1, 1, tq, 2), lambda i, h, qi, kj: (i, h, qi, 0)),
                  pl.BlockSpec((1, 1, 2, tk), lambda i, h, qi, kj: (i, h, 0, jnp.minimum(kj, lastf(qi))))],
        out_specs=pl.BlockSpec((1, tq, LANE), lambda i, h, qi, kj: (i, qi, h)),
        out_shape=jax.ShapeDtypeStruct((b, t, FOX_W), F32),
        scratch_shapes=[pltpu.VMEM((2, tq, 1), F32), pltpu.VMEM((2, tq, 1), F32), pltpu.VMEM((2, tq, LANE), F32)],
        compiler_params=_cparams(("parallel", "parallel", "parallel", "arbitrary")),
        name="prompt_fox",
    )(fq, fkv, fkv, c4, ct4)


QB = 128
NWIN = WINDOW // QB + 1


def _pnsa_body(q_ref, g_ref, cmp_ref, ks_ref, w0, w1, w2, w3, w4, ovt_ref, band_ref, c31_ref, bandc_ref,
               o_ref, q_s, sel_s, imp_s, oc_s, ow_s, m_s, l_s, acc_s, *, t_len, tk, nk):
    qi = pl.program_id(1)
    kj = pl.program_id(3)
    t0 = qi * QB
    last = (t0 + QB - 1) // tk
    nj = t_len // SEL_BLOCK
    nc = t_len // CMP_STRIDE
    n_c = (t_len - CMP_BLOCK) // CMP_STRIDE + 1
    rows = GRP * QB
    band = band_ref[...].reshape(rows, 2 * QB)
    c31 = c31_ref[0]

    @pl.when(kj == 0)
    def _():
        q = q_ref[0]
        lane = _iota((QB, LANE), 1)
        parts = []
        for g in range(GRP):
            sl = q[:, (g // 2) * LANE:(g // 2 + 1) * LANE]
            if g % 2 == 1:
                sl = pltpu.roll(sl, HEAD_DIM, 1)
            parts.append(jnp.where(lane < HEAD_DIM, sl, 0.0))
        qf = (jnp.concatenate(parts, axis=0) * SCALE).astype(MXU)
        q_s[...] = qf

        cmb = cmp_ref[0, 0].astype(MXU)
        s = _dot_nt(qf, cmb)
        shift = ((_iota((LANE, nc), 1) == 8 * qi - 9 + _iota((LANE, nc), 0))
                 & (_iota((LANE, nc), 0) < 16)).astype(BF16)
        placed = sum(_dot(t, shift) for t in _split3(bandc_ref[0]))
        s = s + placed + c31[:, 0:1]
        trow = t0 + (_iota((rows, nc), 0) % QB)
        ncol = _iota((rows, nc), 1)
        maskc = (CMP_STRIDE * ncol + (CMP_BLOCK - 1) <= trow) & (ncol < n_c)
        pc = _softmax_rows(jnp.where(maskc, s, NEG_INF))
        oc_s[...] = _dot(pc.astype(MXU), cmb)
        pcs = pc[0:QB] + pc[QB:2 * QB] + pc[2 * QB:3 * QB] + pc[3 * QB:4 * QB]
        imp = _dot_nt(ovt_ref[...], pcs.astype(MXU))
        jr = _iota((nj, QB), 0)
        qblk = (t0 + _iota((nj, QB), 1)) // SEL_BLOCK
        forced = (jr == 0) | (jr == qblk) | (jr == qblk - 1)
        imp = jnp.where(forced, jnp.inf, jnp.where(jr > qblk, NEG_INF, imp))
        imp_s[...] = imp

        def rank_step(jp, cnt):
            row = imp_s[pl.ds(jp, 1), :]
            beats = (row > imp) | ((row == imp) & (jp < jr))
            return cnt + beats.astype(F32)

        cnt = lax.fori_loop(0, nj, rank_step, jnp.zeros((nj, QB), F32))
        sel_t = ((cnt < N_SELECT) & (imp > NEG_INF)).astype(BF16)
        eye = (_iota((QB, QB), 0) == _iota((QB, QB), 1)).astype(BF16)
        sel = _dot_nt(eye, sel_t).astype(BF16)
        sel_s[...] = jnp.concatenate([sel] * GRP, axis=0)

        ri = _iota((rows, QB), 0) % QB
        ci = _iota((rows, QB), 1)
        scores = []
        kws = []
        for w, wref in enumerate((w0, w1, w2, w3, w4)):
            kw = wref[0, 0].astype(MXU)
            kws.append(kw)
            sw = _dot_nt(qf, kw)
            if w == NWIN - 1:
                bias = band[:, QB:2 * QB]
            elif w == NWIN - 2:
                bias = band[:, 0:QB]
            else:
                bias = c31
            dw = WINDOW - QB * w + ri - ci
            mask = (dw >= 0) & (dw < WINDOW) & (qi - (NWIN - 1) + w >= 0)
            scores.append(jnp.where(mask, sw + bias, NEG_INF))
        pw = _softmax_rows(jnp.concatenate(scores, axis=1))
        ow = jnp.zeros((rows, LANE), F32)
        for w in range(NWIN):
            ow = ow + _dot(pw[:, w * QB:(w + 1) * QB].astype(MXU), kws[w])
        ow_s[...] = ow

        m_s[...] = jnp.full_like(m_s, NEG_INF)
        l_s[...] = jnp.zeros_like(l_s)
        acc_s[...] = jnp.zeros_like(acc_s)

    @pl.when(kj <= last)
    def _():
        ksb = ks_ref[0, 0].astype(MXU)
        s = _dot_nt(q_s[...], ksb)
        expand = (_iota((nj, tk), 0) == (kj * tk + _iota((nj, tk), 1)) // SEL_BLOCK).astype(BF16)
        mexp = _dot(sel_s[...], expand)
        trow = t0 + (_iota((rows, tk), 0) % QB)
        scol = kj * tk + _iota((rows, tk), 1)
        mask = (mexp > 0.5) & (scol <= trow)
        chunks = []
        for c in range(tk // QB):
            delta = qi - (kj * (tk // QB) + c)
            chunks.append(jnp.where(delta == 0, band[:, QB:2 * QB], jnp.where(delta == 1, band[:, 0:QB], c31)))
        bias = jnp.concatenate(chunks, axis=1)
        lg = jnp.where(mask, s + bias, NEG_INF)
        alpha, p = _online_update(lg, m_s, l_s)
        acc_s[...] = alpha * acc_s[...] + _dot(p.astype(MXU), ksb)

    @pl.when(kj == nk - 1)
    def _():
        osl = acc_s[...] * (1.0 / l_s[...])
        gt = g_ref[0, 0]

        def gate(k):
            return jnp.concatenate([gt[:, g * 3 + k:g * 3 + k + 1] for g in range(GRP)], axis=0)

        of = gate(0) * oc_s[...] + gate(1) * osl + gate(2) * ow_s[...]
        lane = _iota((QB, LANE), 1)
        for gp in range(GRP // 2):
            ev = of[(2 * gp) * QB:(2 * gp + 1) * QB]
            od = of[(2 * gp + 1) * QB:(2 * gp + 2) * QB]
            o_ref[0, :, gp * LANE:(gp + 1) * LANE] = jnp.where(lane < HEAD_DIM, pltpu.roll(ev, HEAD_DIM, 1), od)


def _prompt_nsa(nq, gates, cmp4, ksv, kwv, ovt, band, c31, bandc, tk):
    b, t, _ = nq.shape
    nqb, nk = t // QB, t // tk
    nj = t // SEL_BLOCK
    nc = t // CMP_STRIDE
    rows = GRP * QB
    lastf = lambda qi: (qi * QB + QB - 1) // tk
    wspec = lambda w: pl.BlockSpec((1, 1, QB, LANE),
                                   lambda i, qi, h, kj: (i, h, jnp.maximum(qi - (NWIN - 1) + w, 0), 0))
    return pl.pallas_call(
        functools.partial(_pnsa_body, t_len=t, tk=tk, nk=nk),
        grid=(b, nqb, KVH, nk),
        in_specs=[pl.BlockSpec((1, QB, GRP * HEAD_DIM), lambda i, qi, h, kj: (i, qi, h)),
                  pl.BlockSpec((1, 1, QB, GRP * 3), lambda i, qi, h, kj: (i, h, qi, 0)),
                  pl.BlockSpec((1, 1, nc, LANE), lambda i, qi, h, kj: (i, h, 0, 0)),
                  pl.BlockSpec((1, 1, tk, LANE), lambda i, qi, h, kj: (i, h, jnp.minimum(kj, lastf(qi)), 0))]
                 + [wspec(w) for w in range(NWIN)]
                 + [pl.BlockSpec((nj, nc), lambda i, qi, h, kj: (0, 0)),
                    pl.BlockSpec((GRP, QB, 2 * QB), lambda i, qi, h, kj: (h, 0, 0)),
                    pl.BlockSpec((1, rows, LANE), lambda i, qi, h, kj: (h, 0, 0)),
                    pl.BlockSpec((1, rows, LANE), lambda i, qi, h, kj: (h, 0, 0))],
        out_specs=pl.BlockSpec((1, QB, GRP * HEAD_DIM), lambda i, qi, h, kj: (i, qi, h)),
        out_shape=jax.ShapeDtypeStruct((b, t, NSA_Q_W), F32),
        scratch_shapes=[pltpu.VMEM((rows, LANE), MXU), pltpu.VMEM((rows, nj), BF16), pltpu.VMEM((nj, QB), F32),
                        pltpu.VMEM((rows, LANE), F32), pltpu.VMEM((rows, LANE), F32),
                        pltpu.VMEM((rows, 1), F32), pltpu.VMEM((rows, 1), F32), pltpu.VMEM((rows, LANE), F32)],
        compiler_params=_cparams(("parallel", "parallel", "parallel", "arbitrary")),
        name="prompt_nsa",
    )(nq, gates, cmp4, ksv, kwv, kwv, kwv, kwv, kwv, ovt, band, c31, bandc)


def _dfox_body(pt_ref, *refs, gp, ns, s_new):
    pages = refs[:gp]
    q_ref, cq_ref, ck_ref, ckn_ref, new_ref, o_ref, m_s, l_s, acc_s = refs[gp:]
    st = pl.program_id(1)
    rows = FOX_HEADS * s_new
    qb = (q_ref[0] * SCALE).astype(MXU)

    @pl.when(st == 0)
    def _():
        m_s[...] = jnp.full_like(m_s, NEG_INF)
        l_s[...] = jnp.zeros_like(l_s)
        acc_s[...] = jnp.zeros_like(acc_s)

    @pl.when(st < ns)
    def _():
        kv = [pg[0].astype(MXU) for pg in pages]
        s = jnp.concatenate([_dot_nt(qb, x[:, 0:FOX_W]) for x in kv], axis=1)
        ck = jnp.concatenate([ck_ref[0]] * s_new, axis=0)
        lg = s + cq_ref[0] - ck
        alpha, p = _online_update(lg, m_s, l_s)
        pv = jnp.zeros((rows, FOX_W), F32)
        for w in range(gp):
            pv = pv + _dot(p[:, w * PAGE:(w + 1) * PAGE].astype(MXU), kv[w][:, FOX_W:2 * FOX_W])
        acc_s[...] = alpha * acc_s[...] + pv

    @pl.when(st == ns)
    def _():
        x = new_ref[0].astype(MXU)
        s = _dot_nt(qb, x[:, 0:FOX_W])
        ck = jnp.concatenate([ckn_ref[0]] * s_new, axis=0)
        qrow = _iota((rows, PAGE), 0) // FOX_HEADS
        kcol = _iota((rows, PAGE), 1)
        lg = jnp.where(kcol <= qrow, s + cq_ref[0] - ck, NEG_INF)
        alpha, p = _online_update(lg, m_s, l_s)
        acc = alpha * acc_s[...] + _dot(p.astype(MXU), x[:, FOX_W:2 * FOX_W])
        o = acc * (1.0 / l_s[...])
        own = (_iota((rows, FOX_W), 1) // HEAD_DIM) == (_iota((rows, FOX_W), 0) % FOX_HEADS)
        o = jnp.where(own, o, 0.0)
        o_ref[0] = jnp.sum(o.reshape(s_new, FOX_HEADS, FOX_W), axis=1)


def _decode_fox(cache2d, page_table, qrows, cq, ct_all, new_pad, gp):
    b, npg = page_table.shape
    ns = npg // gp
    rows = qrows.shape[1]
    s_new = rows // FOX_HEADS
    stc = lambda st: jnp.minimum(st, ns - 1)
    page_spec = lambda w: pl.BlockSpec((1, PAGE, 2 * FOX_W), lambda i, st, pt: (pt[i, stc(st) * gp + w], 0, 0))
    grid_spec = pltpu.PrefetchScalarGridSpec(
        num_scalar_prefetch=1,
        grid=(b, ns + 1),
        in_specs=[page_spec(w) for w in range(gp)]
                 + [pl.BlockSpec((1, rows, FOX_W), lambda i, st, pt: (i, 0, 0)),
                    pl.BlockSpec((1, rows, 1), lambda i, st, pt: (i, 0, 0)),
                    pl.BlockSpec((1, FOX_HEADS, gp * PAGE), lambda i, st, pt: (i, 0, stc(st))),
                    pl.BlockSpec((1, FOX_HEADS, PAGE), lambda i, st, pt: (i, 0, npg)),
                    pl.BlockSpec((1, PAGE, 2 * FOX_W), lambda i, st, pt: (i, 0, 0))],
        out_specs=pl.BlockSpec((1, s_new, FOX_W), lambda i, st, pt: (i, 0, 0)),
        scratch_shapes=[pltpu.VMEM((rows, 1), F32), pltpu.VMEM((rows, 1), F32), pltpu.VMEM((rows, FOX_W), F32)],
    )
    return pl.pallas_call(
        functools.partial(_dfox_body, gp=gp, ns=ns, s_new=s_new),
        grid_spec=grid_spec,
        out_shape=jax.ShapeDtypeStruct((b, s_new, FOX_W), F32),
        compiler_params=_cparams(("parallel", "arbitrary")),
        name="decode_fox",
    )(page_table, *([cache2d] * gp), qrows, cq, ct_all, ct_all, new_pad)


def _dnsa_body(pt_ref, *refs, gp, ns, nj, njp):
    pages = refs[:gp]
    (q_ref, g_ref, kc_ref, vc_ref, ovt_ref, bcmp_ref, bwin_ref, blast_ref, bnew_ref, c31_ref,
     win_ref, wnew_ref, snew_ref, o_ref, sel_s, imp_s, oc_s, ow_s, m_s, l_s, acc_s) = refs[gp:]
    st = pl.program_id(1)
    rows = q_ref.shape[1]
    rq = rows // GRP
    qb = (q_ref[0] * SCALE).astype(MXU)
    c31 = c31_ref[...]

    @pl.when(st == 0)
    def _():
        kc = kc_ref[0].astype(MXU)
        vc = vc_ref[0].astype(MXU)
        pc = _softmax_rows(_dot_nt(qb, kc) + bcmp_ref[...])
        oc_s[...] = _dot(pc.astype(MXU), vc)
        pcs = pc[0:rq]
        for g in range(1, GRP):
            pcs = pcs + pc[g * rq:(g + 1) * rq]
        pcs = jnp.concatenate([pcs, jnp.zeros((LANE - rq, pcs.shape[1]), F32)], axis=0)
        imp = _dot_nt(ovt_ref[...], pcs.astype(MXU))
        jr = _iota((njp, LANE), 0)
        qblk = ((nj - 1) * SEL_BLOCK + _iota((njp, LANE), 1) // KVH) // SEL_BLOCK
        forced = (jr == 0) | (jr == qblk) | (jr == qblk - 1)
        imp = jnp.where(forced, jnp.inf, jnp.where(jr > qblk, NEG_INF, imp))
        imp_s[...] = imp

        def rank_step(jp, cnt):
            row = imp_s[pl.ds(jp, 1), :]
            beats = (row > imp) | ((row == imp) & (jp < jr))
            return cnt + beats.astype(F32)

        cnt = lax.fori_loop(0, nj, rank_step, jnp.zeros((njp, LANE), F32))
        sel_t = ((cnt < N_SELECT) & (imp > NEG_INF)).astype(BF16)
        eye = (_iota((LANE, LANE), 0) == _iota((LANE, LANE), 1)).astype(BF16)
        sel = _dot_nt(eye, sel_t).astype(BF16)
        sel_s[...] = jnp.concatenate([sel[0:rq]] * GRP, axis=0)

        win = win_ref[0].astype(MXU)
        wnew = wnew_ref[0].astype(MXU)
        bw = bwin_ref[...]
        sw = jnp.concatenate([_dot_nt(qb, win[:, 0:LANE]) + bw[:, 0:WINDOW],
                              _dot_nt(qb, wnew[:, 0:LANE]) + bw[:, WINDOW:WINDOW + PAGE]], axis=1)
        pw = _softmax_rows(sw)
        ow_s[...] = (_dot(pw[:, 0:WINDOW].astype(MXU), win[:, LANE:2 * LANE])
                     + _dot(pw[:, WINDOW:WINDOW + PAGE].astype(MXU), wnew[:, LANE:2 * LANE]))

        m_s[...] = jnp.full_like(m_s, NEG_INF)
        l_s[...] = jnp.zeros_like(l_s)
        acc_s[...] = jnp.zeros_like(acc_s)

    @pl.when(st < ns)
    def _():
        kv = [pg[0].astype(MXU) for pg in pages]
        s = jnp.concatenate([_dot_nt(qb, x[:, 0:LANE]) for x in kv], axis=1)
        width = gp * PAGE
        expand = (_iota((njp, width), 0) == (st * width + _iota((njp, width), 1)) // SEL_BLOCK).astype(BF16)
        mexp = _dot(sel_s[...], expand)
        tail = jnp.where(st == ns - 1, blast_ref[...], c31)
        bias = jnp.concatenate([c31] * (gp - 1) + [tail], axis=1)
        lg = jnp.where(mexp > 0.5, s + bias, NEG_INF)
        alpha, p = _online_update(lg, m_s, l_s)
        pv = jnp.zeros((rows, LANE), F32)
        for w in range(gp):
            pv = pv + _dot(p[:, w * PAGE:(w + 1) * PAGE].astype(MXU), kv[w][:, LANE:2 * LANE])
        acc_s[...] = alpha * acc_s[...] + pv

    @pl.when(st == ns)
    def _():
        x = snew_ref[0].astype(MXU)
        s = _dot_nt(qb, x[:, 0:LANE]) + bnew_ref[...]
        picked = sel_s[:, nj - 1:nj].astype(F32) > 0.5
        lg = jnp.where(picked, s, NEG_INF)
        alpha, p = _online_update(lg, m_s, l_s)
        acc = alpha * acc_s[...] + _dot(p.astype(MXU), x[:, LANE:2 * LANE])
        osl = acc * (1.0 / jnp.maximum(l_s[...], 1e-30))
        gt = g_ref[0]
        of = gt[:, 0:1] * oc_s[...] + gt[:, 1:2] * osl + gt[:, 2:3] * ow_s[...]
        head = _iota((rows, LANE), 0) % KVH
        of = jnp.where(head == 0, of, pltpu.roll(of, HEAD_DIM, 1))
        o_ref[0] = of[:, 0:HEAD_DIM]


def _decode_nsa(cache2d, page_table, qrows, gates, cmp2, ovt, bcmp, bwin, blast, bnew, c31, win2d, wnew_pad,
                snew_pad, gp, nj):
    b, npg = page_table.shape
    ns = npg // gp
    rows = qrows.shape[1]
    ncp = cmp2.shape[1]
    njp = ovt.shape[0]
    stc = lambda st: jnp.minimum(st, ns - 1)
    page_spec = lambda w: pl.BlockSpec((1, PAGE, 2 * LANE), lambda i, st, pt: (pt[i, stc(st) * gp + w], 0, 1))
    const2 = lambda a: pl.BlockSpec(a.shape, lambda i, st, pt: (0, 0))
    grid_spec = pltpu.PrefetchScalarGridSpec(
        num_scalar_prefetch=1,
        grid=(b, ns + 1),
        in_specs=[page_spec(w) for w in range(gp)]
                 + [pl.BlockSpec((1, rows, LANE), lambda i, st, pt: (i, 0, 0)),
                    pl.BlockSpec((1, rows, 3), lambda i, st, pt: (i, 0, 0)),
                    pl.BlockSpec((1, ncp, LANE), lambda i, st, pt: (i, 0, 0)),
                    pl.BlockSpec((1, ncp, LANE), lambda i, st, pt: (i, 0, 1)),
                    const2(ovt), const2(bcmp), const2(bwin), const2(blast), const2(bnew), const2(c31),
                    pl.BlockSpec((1, WINDOW, 2 * LANE), lambda i, st, pt: (i, 0, 0)),
                    pl.BlockSpec((1, PAGE, 2 * LANE), lambda i, st, pt: (i, 0, 0)),
                    pl.BlockSpec((1, PAGE, 2 * LANE), lambda i, st, pt: (i, 0, 1))],
        out_specs=pl.BlockSpec((1, rows, HEAD_DIM), lambda i, st, pt: (i, 0, 0)),
        scratch_shapes=[pltpu.VMEM((rows, njp), BF16), pltpu.VMEM((njp, LANE), F32),
                        pltpu.VMEM((rows, LANE), F32), pltpu.VMEM((rows, LANE), F32),
                        pltpu.VMEM((rows, 1), F32), pltpu.VMEM((rows, 1), F32), pltpu.VMEM((rows, LANE), F32)],
    )
    return pl.pallas_call(
        functools.partial(_dnsa_body, gp=gp, ns=ns, nj=nj, njp=njp),
        grid_spec=grid_spec,
        out_shape=jax.ShapeDtypeStruct((b, rows, HEAD_DIM), F32),
        compiler_params=_cparams(("parallel", "arbitrary")),
        name="decode_nsa",
    )(page_table, *([cache2d] * gp), qrows, gates, cmp2, cmp2, ovt, bcmp, bwin, blast, bnew, c31, win2d, wnew_pad,
      snew_pad)


def _merge_body(on_ref, of_ref, ga_ref, gb_ref, x_ref, wa, wb, wo, g1, b1, o_ref):
    a = _dot(on_ref[...].astype(MXU), wa[...])
    bb = _dot(of_ref[...].astype(MXU), wb[...])
    mix = _dot((ga_ref[...] * a + gb_ref[...] * bb).astype(MXU), wo[...])
    o_ref[...] = _layer_norm(ALPHA * x_ref[...] + mix, g1[...], b1[...])


def _merge(o_nsa, o_fox, sga, sgb, x2d, wa, wb, wo, g1, b1, tm):
    m, d = x2d.shape
    row = lambda n: pl.BlockSpec((tm, n), lambda i: (i, 0))
    full = lambda a: pl.BlockSpec(a.shape, lambda i: (0, 0))
    wa, wb, wo = wa.astype(MXU), wb.astype(MXU), wo.astype(MXU)
    g1, b1 = g1.reshape(1, d), b1.reshape(1, d)
    return pl.pallas_call(
        _merge_body,
        grid=(m // tm,),
        in_specs=[row(NSA_Q_W), row(FOX_W), row(d), row(d), row(d), full(wa), full(wb), full(wo), full(g1), full(b1)],
        out_specs=row(d),
        out_shape=jax.ShapeDtypeStruct((m, d), F32),
        compiler_params=_cparams(("parallel",)),
        name="merge_ln1",
    )(o_nsa, o_fox, sga, sgb, x2d, wa, wb, wo, g1, b1)


def _ffn_up_long_body(x_ref, prev_ref, wu, wg, cw, cb, h_ref, st_ref, carry_s, *, tiles_per_seq):
    i = pl.program_id(0)
    tm = x_ref.shape[0]
    xb = x_ref[...].astype(MXU)
    u = _dot(xb, wu[...])

    @pl.when(i % tiles_per_seq == 0)
    def _():
        carry_s[...] = prev_ref[0]

    r = _iota(u.shape, 0)
    um1 = jnp.where(r >= 1, pltpu.roll(u, 1, 0), carry_s[1:2, :])
    um2 = jnp.where(r >= 2, pltpu.roll(u, 2, 0), jnp.where(r == 0, carry_s[0:1, :], carry_s[1:2, :]))
    conv = cb[...] + cw[0:1, :] * um2 + cw[1:2, :] * um1 + cw[2:3, :] * u
    h_ref[...] = (jax.nn.gelu(conv) * _dot(xb, wg[...])).astype(h_ref.dtype)
    last2 = u[tm - 2:tm, :]
    carry_s[...] = last2
    st_ref[0] = last2


def _ffn_up_short_body(x_ref, p1_ref, p2_ref, wu, wg, cw, cb, h_ref, u_ref, *, seq):
    xb = x_ref[...].astype(MXU)
    u = _dot(xb, wu[...])
    t = _iota(u.shape, 0) % seq
    um1 = jnp.where(t >= 1, pltpu.roll(u, 1, 0), p1_ref[...])
    um2 = jnp.where(t >= 2, pltpu.roll(u, 2, 0), p2_ref[...])
    conv = cb[...] + cw[0:1, :] * um2 + cw[1:2, :] * um1 + cw[2:3, :] * u
    h_ref[...] = (jax.nn.gelu(conv) * _dot(xb, wg[...])).astype(h_ref.dtype)
    u_ref[...] = u


def _ffn_up_long(x1, conv_prev, wu, wg, cw, cb, seq, tm):
    m, d = x1.shape
    f = wu.shape[1]
    nseq = m // seq
    tps = seq // tm
    wu, wg = wu.astype(MXU), wg.astype(MXU)
    cb = cb.reshape(1, f)
    full = lambda a: pl.BlockSpec(a.shape, lambda i: (0, 0))
    return pl.pallas_call(
        functools.partial(_ffn_up_long_body, tiles_per_seq=tps),
        grid=(m // tm,),
        in_specs=[pl.BlockSpec((tm, d), lambda i: (i, 0)),
                  pl.BlockSpec((1, 2, f), lambda i: (i // tps, 0, 0)),
                  full(wu), full(wg), full(cw), full(cb)],
        out_specs=[pl.BlockSpec((tm, f), lambda i: (i, 0)),
                   pl.BlockSpec((1, 2, f), lambda i: (i // tps, 0, 0))],
        out_shape=[jax.ShapeDtypeStruct((m, f), MXU), jax.ShapeDtypeStruct((nseq, 2, f), F32)],
        scratch_shapes=[pltpu.VMEM((2, f), F32)],
        compiler_params=_cparams(("arbitrary",)),
        name="ffn_up_long",
    )(x1, conv_prev, wu, wg, cw, cb)


def _ffn_up_short(x1, conv_prev, wu, wg, cw, cb, seq):
    m, d = x1.shape
    f = wu.shape[1]
    nseq = m // seq
    wu, wg = wu.astype(MXU), wg.astype(MXU)
    cb = cb.reshape(1, f)
    zeros = jnp.zeros((nseq, seq, f), F32)
    p1 = zeros.at[:, 0].set(conv_prev[:, 1]).reshape(m, f)
    p2 = zeros.at[:, 0].set(conv_prev[:, 0]).at[:, 1].set(conv_prev[:, 1]).reshape(m, f)
    full = lambda a: pl.BlockSpec(a.shape, lambda i: (0, 0))
    h, u = pl.pallas_call(
        functools.partial(_ffn_up_short_body, seq=seq),
        grid=(1,),
        in_specs=[full(x1), full(p1), full(p2), full(wu), full(wg), full(cw), full(cb)],
        out_specs=[pl.BlockSpec((m, f), lambda i: (0, 0)), pl.BlockSpec((m, f), lambda i: (0, 0))],
        out_shape=[jax.ShapeDtypeStruct((m, f), MXU), jax.ShapeDtypeStruct((m, f), F32)],
        compiler_params=_cparams(("arbitrary",)),
        name="ffn_up_short",
    )(x1, p1, p2, wu, wg, cw, cb)
    return h, u.reshape(nseq, seq, f)[:, seq - 2:]


def _ffn_down_body(h_ref, x1_ref, p_ref, wd, wpg, wp, g2, b2, o_ref):
    f = _dot(h_ref[...], wd[...])
    x2 = _layer_norm(ALPHA * x1_ref[...] + f, g2[...], b2[...])
    gate = jax.nn.sigmoid(_dot(x2.astype(MXU), wpg[...]))
    o_ref[...] = x2 + gate * _dot(p_ref[...].astype(MXU), wp[...])


def _ffn_down(h, x1, p2d, wd, wpg, wp, g2, b2, tm):
    m, d = x1.shape
    row = lambda n: pl.BlockSpec((tm, n), lambda i: (i, 0))
    full = lambda a: pl.BlockSpec(a.shape, lambda i: (0, 0))
    wd, wpg, wp = wd.astype(MXU), wpg.astype(MXU), wp.astype(MXU)
    g2, b2 = g2.reshape(1, d), b2.reshape(1, d)
    return pl.pallas_call(
        _ffn_down_body,
        grid=(m // tm,),
        in_specs=[row(h.shape[1]), row(d), row(p2d.shape[1]), full(wd), full(wpg), full(wp), full(g2), full(b2)],
        out_specs=row(d),
        out_shape=jax.ShapeDtypeStruct((m, d), F32),
        compiler_params=_cparams(("parallel",)),
        name="ffn_down_ln2_ple",
    )(h, x1, p2d, wd, wpg, wp, g2, b2)


def _overlap_t(n_sel, n_sel_pad, n_c, n_c_pad):
    start = np.arange(n_c_pad)[None, :] * CMP_STRIDE
    j = np.arange(n_sel_pad)[:, None]
    ov = (start < (j + 1) * SEL_BLOCK) & (start + CMP_BLOCK > j * SEL_BLOCK)
    ov &= (np.arange(n_c_pad)[None, :] < n_c) & (j < n_sel)
    return jnp.asarray(ov, dtype=MXU)


def _prompt_tables(rel_bias):
    tb = rel_bias.astype(F32)
    i = np.arange(QB)[:, None]
    c = np.arange(2 * QB)[None, :]
    band = jnp.transpose(tb[_bucket_np(i + QB - c)], (2, 0, 1))
    far = tb[_bucket_np(np.array(MAX_DISTANCE))]
    c31 = jnp.broadcast_to(far.reshape(KVH, GRP, 1, 1), (KVH, GRP, QB, LANE)).reshape(KVH, GRP * QB, LANE)
    mcol = np.arange(LANE)[None, :]
    dc = i + 113 - CMP_STRIDE * mcol
    valid = (dc >= 0) & (mcol < 16)
    bc = jnp.transpose(tb[_bucket_np(dc)], (2, 0, 1)) - far.reshape(-1, 1, 1)
    bc = jnp.where(jnp.asarray(valid)[None], bc, 0.0)
    bandc = bc.reshape(KVH, GRP * QB, LANE)
    return band, c31, bandc


def _decode_tables(rel_bias, p_len, s_new, n_c, ncp):
    tb = rel_bias.astype(F32)
    rows = GRP * s_new * KVH
    r = np.arange(rows)
    g, i, h = r // (s_new * KVH), (r // KVH) % s_new, r % KVH
    hg = h * GRP + g
    qpos = (p_len + i)[:, None]

    def table(kpos, ok):
        d = qpos - kpos[None, :]
        vals = tb[_bucket_np(d), hg[:, None]]
        return jnp.where(jnp.asarray(ok & (d >= 0)), vals, NEG_INF)

    n = np.arange(ncp)
    bcmp = table(CMP_STRIDE * n + CMP_BLOCK - 1, (n < n_c)[None, :] & np.ones((rows, 1), bool))
    wb = min(WINDOW, p_len)
    wpos = np.concatenate([p_len - wb + np.arange(wb), p_len + np.arange(PAGE)])
    wok = np.concatenate([np.ones(wb, bool), np.arange(PAGE) < s_new])[None, :]
    dwin = qpos - wpos[None, :]
    bwin = table(wpos, wok & (dwin < WINDOW))
    blast = table(p_len - PAGE + np.arange(PAGE), np.ones((rows, PAGE), bool))
    bnew = table(p_len + np.arange(PAGE), (np.arange(PAGE) < s_new)[None, :] & np.ones((rows, 1), bool))
    c31 = jnp.broadcast_to(tb[_bucket_np(np.array(MAX_DISTANCE)), hg][:, None], (rows, LANE))
    return bcmp, bwin, blast, bnew, c31


def _pick(n, prefs):
    for p in prefs:
        if n % p == 0:
            return p
    return n


def _finish(x2d, o_nsa, o_fox, sga, sgb, p2d, conv_prev, seq, W):
    m = x2d.shape[0]
    tm = _pick(m, (256, 128))
    x1 = _merge(o_nsa, o_fox, sga, sgb, x2d, W['w_branch_a'], W['w_branch_b'], W['w_out'], W['ln1_g'], W['ln1_b'], tm)
    if seq >= 128:
        h, conv_state = _ffn_up_long(x1, conv_prev, W['w_ffn_up'], W['w_ffn_gate'], W['ffn_conv_w'], W['ffn_conv_b'],
                                     seq, _pick(seq, (256, 128)))
    else:
        h, conv_state = _ffn_up_short(x1, conv_prev, W['w_ffn_up'], W['w_ffn_gate'], W['ffn_conv_w'],
                                      W['ffn_conv_b'], seq)
    y = _ffn_down(h, x1, p2d, W['w_ffn_down'], W['w_ple_gate'], W['w_ple'], W['ln2_g'], W['ln2_b'], tm)
    return y, conv_state


def _prompt_layer(x, p_emb, W, rel_bias):
    b, t, d = x.shape
    m = b * t
    x2d = x.reshape(m, d)
    nq, nkv, kw, fq, fkv, sga, sgb, ng, logf = _proj(x2d, W['w_in'], W['b_forget'], _pick(m, (256, 128)))
    npg = t // PAGE
    ident = jnp.arange(b * npg, dtype=jnp.int32).reshape(b, npg)

    c, ct = _paged_cumsum(logf.reshape(b * npg, PAGE, FOX_HEADS), ident)
    hp = FOX_HEADS // 2
    c4 = jnp.transpose(c.reshape(b, t, hp, 2), (0, 2, 1, 3))
    ct4 = ct.reshape(b, hp, 2, t)
    o_fox = _prompt_fox(fq.reshape(b, t, FOX_W), fkv.reshape(b, t, 2 * FOX_W), c4, ct4,
                        _pick(t, (256, 128)), _pick(t, (512, 256, 128)))

    cmp2 = _compress(nkv.reshape(b * npg, PAGE, 4 * NSA_KV_W), ident, W['nsa_cmp_pe'], W['nsa_cmp_w1'],
                     W['nsa_cmp_w2'])
    nc = cmp2.shape[1]
    cmp4 = jnp.transpose(cmp2.reshape(b, nc, 2, KVH, HEAD_DIM), (0, 3, 1, 2, 4)).reshape(b, KVH, nc, LANE)
    nkv5 = nkv.reshape(b, t, 4, KVH, HEAD_DIM)
    ksv = jnp.transpose(nkv5[:, :, 2:4], (0, 3, 1, 2, 4)).reshape(b, KVH, t, LANE)
    kwv = jnp.transpose(kw.reshape(b, t, 2, KVH, HEAD_DIM), (0, 3, 1, 2, 4)).reshape(b, KVH, t, LANE)
    gates = jnp.transpose(ng[:, FOX_HEADS:FOX_HEADS + KVH * GRP * 3].reshape(b, t, KVH, GRP * 3), (0, 2, 1, 3))
    n_c = (t - CMP_BLOCK) // CMP_STRIDE + 1
    ovt = _overlap_t(t // SEL_BLOCK, t // SEL_BLOCK, n_c, nc)
    band, c31, bandc = _prompt_tables(rel_bias)
    o_nsa = _prompt_nsa(nq.reshape(b, t, NSA_Q_W), gates, cmp4, ksv, kwv, ovt, band, c31, bandc,
                        _pick(t, (512, 256, 128)))

    conv_prev = jnp.zeros((b, 2, W['w_ffn_up'].shape[1]), F32)
    y, conv_state = _finish(x2d, o_nsa.reshape(m, NSA_Q_W), o_fox.reshape(m, FOX_W), sga, sgb,
                            p_emb.reshape(m, -1), conv_prev, t, W)
    wb = min(WINDOW, t)
    state = (fkv.reshape(b, t, 2, FOX_HEADS, HEAD_DIM), logf.reshape(b, t, FOX_HEADS),
             nkv.reshape(b, t, 4, KVH, HEAD_DIM), kw.reshape(b, t, 2, KVH, HEAD_DIM)[:, t - wb:], conv_state)
    return y.reshape(b, t, d), state


def _sample_layer(x, p_emb, c_fox_kv, c_fox_logf, c_nsa, win, conv_prev, page_table, W, rel_bias):
    b, s, d = x.shape
    m = b * s
    npg = page_table.shape[1]
    p_len = npg * PAGE
    n_pool = c_nsa.shape[0]
    x2d = x.reshape(m, d)
    nq, nkv, kw, fq, fkv, sga, sgb, ng, logf = _proj(x2d, W['w_in'], W['b_forget'], _pick(m, (256, 128, 8)))
    gp = _pick(npg, (8, 4, 2, 1))
    pad_rows = lambda a: jnp.pad(a.reshape(b, s, -1), ((0, 0), (0, PAGE - s), (0, 0)))

    pool = jnp.concatenate([c_fox_logf.astype(F32), pad_rows(logf)], axis=0)
    pt_ext = jnp.concatenate([page_table, n_pool + jnp.arange(b, dtype=jnp.int32)[:, None]], axis=1)
    _, ct_all = _paged_cumsum(pool, pt_ext)
    cq = jnp.transpose(ct_all[:, :, p_len:p_len + s], (0, 2, 1)).reshape(b, s * FOX_HEADS, 1)
    own = (np.arange(FOX_W)[None, :] // HEAD_DIM) == (np.arange(s * FOX_HEADS)[:, None] % FOX_HEADS)
    qrows_f = jnp.where(jnp.asarray(own)[None], jnp.repeat(fq.reshape(b, s, FOX_W), FOX_HEADS, axis=1), 0.0)
    o_fox = _decode_fox(c_fox_kv.reshape(n_pool, PAGE, 2 * FOX_W), page_table, qrows_f, cq, ct_all,
                        pad_rows(fkv), gp)

    cache2d = c_nsa.reshape(n_pool, PAGE, 4 * NSA_KV_W)
    cmp2 = _compress(cache2d, page_table, W['nsa_cmp_pe'], W['nsa_cmp_w1'], W['nsa_cmp_w2'])
    ncp = cmp2.shape[1]
    n_c = (p_len + s - CMP_BLOCK) // CMP_STRIDE + 1
    nj = -(-(p_len + s) // SEL_BLOCK)
    njp = -(-nj // LANE) * LANE
    ovt = _overlap_t(nj, njp, n_c, ncp)
    bcmp, bwin, blast, bnew, c31 = _decode_tables(rel_bias, p_len, s, n_c, ncp)
    rows = GRP * s * KVH
    r = np.arange(rows)
    g_, i_, h_ = r // (s * KVH), (r // KVH) % s, r % KVH
    nq5 = nq.reshape(b, s, KVH, GRP, HEAD_DIM)
    qsel = nq5[:, i_, h_, g_]
    half = jnp.asarray((np.arange(LANE)[None, :] // HEAD_DIM) == h_[:, None])
    qrows_n = jnp.where(half[None], jnp.concatenate([qsel, qsel], axis=-1), 0.0)
    gates = ng[:, FOX_HEADS:FOX_HEADS + KVH * GRP * 3].reshape(b, s, KVH, GRP, 3)[:, i_, h_, g_]
    o_rows = _decode_nsa(cache2d, page_table, qrows_n, gates, cmp2, ovt, bcmp, bwin, blast, bnew, c31,
                         win.reshape(b, win.shape[1], 2 * NSA_KV_W), pad_rows(kw), pad_rows(nkv), gp, nj)
    o_nsa = jnp.transpose(o_rows.reshape(b, GRP, s, KVH, HEAD_DIM), (0, 2, 3, 1, 4)).reshape(m, NSA_Q_W)

    y, conv_state = _finish(x2d, o_nsa, o_fox.reshape(m, FOX_W), sga, sgb, p_emb.reshape(m, -1), conv_prev, s, W)
    win_all = jnp.concatenate([win, kw.reshape(b, s, 2, KVH, HEAD_DIM)], axis=1)
    state = (fkv.reshape(b, s, 2, FOX_HEADS, HEAD_DIM), logf.reshape(b, s, FOX_HEADS),
             nkv.reshape(b, s, 4, KVH, HEAD_DIM), win_all[:, s:], conv_state)
    return y.reshape(b, s, d), state


def kernel(x_prompt, x_sample, p_prompt, p_sample, cache_fox_kv, cache_fox_logf, cache_nsa_kv, state_nsa_win,
           state_ffn_conv, page_table, w_in, b_forget, nsa_cmp_pe, nsa_cmp_w1, nsa_cmp_w2, rel_bias, w_branch_a,
           w_branch_b, w_out, ln1_g, ln1_b, ln2_g, ln2_b, w_ffn_up, w_ffn_gate, ffn_conv_w, ffn_conv_b, w_ffn_down,
           w_ple, w_ple_gate):
    depth = w_in.shape[0]
    xp, xs = x_prompt, x_sample
    st_p, st_s = [], []
    for i in range(depth):
        W = {
            'w_in': w_in[i], 'b_forget': b_forget[i], 'nsa_cmp_pe': nsa_cmp_pe[i], 'nsa_cmp_w1': nsa_cmp_w1[i],
            'nsa_cmp_w2': nsa_cmp_w2[i], 'w_branch_a': w_branch_a[i], 'w_branch_b': w_branch_b[i], 'w_out': w_out[i],
            'ln1_g': ln1_g[i], 'ln1_b': ln1_b[i], 'ln2_g': ln2_g[i], 'ln2_b': ln2_b[i], 'w_ffn_up': w_ffn_up[i],
            'w_ffn_gate': w_ffn_gate[i], 'ffn_conv_w': ffn_conv_w[i], 'ffn_conv_b': ffn_conv_b[i],
            'w_ffn_down': w_ffn_down[i], 'w_ple': w_ple[i], 'w_ple_gate': w_ple_gate[i],
        }
        xp, sp = _prompt_layer(xp, p_prompt[i], W, rel_bias)
        xs, ss = _sample_layer(xs, p_sample[i], cache_fox_kv[i], cache_fox_logf[i], cache_nsa_kv[i],
                               state_nsa_win[i], state_ffn_conv[i], page_table, W, rel_bias)
        st_p.append(sp)
        st_s.append(ss)

    def stk(lst, j):
        return jnp.stack([s[j] for s in lst])

    return (xp, xs, stk(st_p, 0), stk(st_s, 0), stk(st_p, 1), stk(st_s, 1), stk(st_p, 2), stk(st_s, 2),
            stk(st_p, 3), stk(st_s, 3), stk(st_p, 4), stk(st_s, 4))
```

```python
import functools
import math

import numpy as np
import jax
import jax.numpy as jnp
from jax import lax
from jax.experimental import pallas as pl
from jax.experimental.pallas import tpu as pltpu

F32 = jnp.float32
BF16 = jnp.bfloat16
MXU = jnp.bfloat16

HEAD_DIM = 64
KVH = 2
GRP = 4
FOX_HEADS = 8
CMP_BLOCK = 32
CMP_STRIDE = 16
SEL_BLOCK = 64
N_SELECT = 16
WINDOW = 512
N_BUCKETS = 32
MAX_DISTANCE = 128
PAGE = 128
LN_EPS = 1e-5
SCALE = HEAD_DIM ** -0.5
LOG2E = math.log2(math.e)
QSCALE = SCALE * LOG2E
NSA_Q_W = KVH * GRP * HEAD_DIM
NSA_KV_W = KVH * HEAD_DIM
FOX_W = FOX_HEADS * HEAD_DIM
LANE = 128
VMEM_LIMIT = 56 * 1024 * 1024
NEG_INF = float("-inf")
MASK_BIG = 1e9


def _cparams(sem):
    return pltpu.CompilerParams(dimension_semantics=sem, vmem_limit_bytes=VMEM_LIMIT)


def _dot(a, b):
    return jnp.dot(a, b, preferred_element_type=F32)


def _dot_nt(a, b):
    return lax.dot_general(a, b, (((1,), (1,)), ((), ())), preferred_element_type=F32)


def _split3(x):
    hi = x.astype(BF16).astype(F32)
    r = x - hi
    mid = r.astype(BF16).astype(F32)
    lo = (r - mid).astype(BF16).astype(F32)
    return hi, mid, lo


def _iota(shape, dim):
    return lax.broadcasted_iota(jnp.int32, shape, dim)


def _softmax2_rows(lg):
    m = jnp.max(lg, axis=-1, keepdims=True)
    m = jnp.where(m == NEG_INF, 0.0, m)
    e = jnp.exp2(lg - m)
    d = jnp.maximum(jnp.sum(e, axis=-1, keepdims=True), 1e-30)
    return e * (1.0 / d)


def _online_update(lg, m_ref, l_ref):
    m_old = m_ref[...]
    m_new = jnp.maximum(m_old, jnp.max(lg, axis=-1, keepdims=True))
    m_safe = jnp.where(m_new == NEG_INF, 0.0, m_new)
    alpha = jnp.exp2(m_old - m_safe)
    p = jnp.exp2(lg - m_safe)
    l_ref[...] = alpha * l_ref[...] + jnp.sum(p, axis=-1, keepdims=True)
    m_ref[...] = m_new
    return alpha, p


def _layer_norm(x, g, b):
    mu = jnp.mean(x, axis=-1, keepdims=True)
    xc = x - mu
    var = jnp.mean(xc * xc, axis=-1, keepdims=True)
    return xc * lax.rsqrt(var + LN_EPS) * g + b


def _bucket_np(dist):
    n = np.maximum(np.asarray(dist), 0)
    max_exact = N_BUCKETS // 2
    nf = np.maximum(n, 1).astype(np.float32)
    large = max_exact + (np.log(nf / np.float32(max_exact)) / np.float32(math.log(MAX_DISTANCE / max_exact))
                         * np.float32(N_BUCKETS - max_exact)).astype(np.int32)
    large = np.minimum(large, N_BUCKETS - 1)
    return np.where(n < max_exact, n, large).astype(np.int32)


FAR_BUCKET = int(_bucket_np(np.array(MAX_DISTANCE)))


def _proj_body(x_ref, wnq, wnkv, wkw, wfq, wfkv, wga, wgb, wsm, bsm,
               nq_o, nkv_o, kw_o, fq_o, fkv_o, ga_o, gb_o, ng_o, logf_o):
    xb = x_ref[...].astype(MXU)
    nq_o[...] = _dot(xb, wnq[...])
    nkv_o[...] = _dot(xb, wnkv[...])
    kw_o[...] = _dot(xb, wkw[...])
    fq_o[...] = _dot(xb, wfq[...])
    fkv_o[...] = _dot(xb, wfkv[...])
    ga_o[...] = jax.nn.sigmoid(_dot(xb, wga[...]))
    gb_o[...] = jax.nn.sigmoid(_dot(xb, wgb[...]))
    sm = _dot(xb, wsm[...])
    ng_o[...] = jax.nn.sigmoid(sm)
    z = sm + bsm[...]
    ls = jnp.minimum(z, 0.0) - jnp.log1p(jnp.exp(-jnp.abs(z)))
    logf_o[...] = ls[:, :FOX_HEADS]


def _proj(x2d, w_in, b_forget, tm):
    m, d = x2d.shape
    o = np.cumsum([0, NSA_Q_W, 6 * NSA_KV_W, KVH * GRP * 3, FOX_W, FOX_W, FOX_W, FOX_HEADS, d, d])
    wb = w_in.astype(MXU)
    wnq = wb[:, o[0]:o[1]]
    wnkv = wb[:, o[1]:o[1] + 4 * NSA_KV_W]
    wkw = wb[:, o[1] + 4 * NSA_KV_W:o[2]]
    wfq = wb[:, o[3]:o[4]]
    wfkv = wb[:, o[4]:o[6]]
    wga = wb[:, o[7]:o[8]]
    wgb = wb[:, o[8]:o[9]]
    nsm = FOX_HEADS + KVH * GRP * 3
    wsm = jnp.concatenate([wb[:, o[6]:o[7]], wb[:, o[2]:o[3]], jnp.zeros((d, LANE - nsm), MXU)], axis=1)
    bsm = jnp.concatenate([b_forget.astype(F32), jnp.zeros((LANE - FOX_HEADS,), F32)]).reshape(1, LANE)
    ws = [wnq, wnkv, wkw, wfq, wfkv, wga, wgb, wsm]
    widths = [w.shape[1] for w in ws] + [FOX_HEADS]
    row = lambda n: pl.BlockSpec((tm, n), lambda i: (i, 0))
    full = lambda a: pl.BlockSpec(a.shape, lambda i: (0, 0))
    out_shapes = [jax.ShapeDtypeStruct((m, n), F32) for n in widths]
    return pl.pallas_call(
        _proj_body,
        grid=(m // tm,),
        in_specs=[row(d)] + [full(w) for w in ws] + [full(bsm)],
        out_specs=[row(n) for n in widths],
        out_shape=out_shapes,
        compiler_params=_cparams(("parallel",)),
        name="proj",
    )(x2d, *ws, bsm)


def _bias_body(tb_ref, idx_ref, o_ref):
    idx = idx_ref[...]
    nh = tb_ref.shape[1]

    def step(k, acc):
        hd = k % nh
        val = (tb_ref[k // nh, hd] - tb_ref[FAR_BUCKET, hd]) * LOG2E
        return jnp.where(idx == k, val, acc)

    acc = lax.fori_loop(0, tb_ref.shape[0] * nh, step, jnp.zeros(idx.shape, F32))
    o_ref[...] = jnp.where(idx == -1, NEG_INF, acc)


def _bias_lookup(rel_bias, idx_np):
    idx = jnp.asarray(idx_np, dtype=jnp.int32)
    return pl.pallas_call(
        _bias_body,
        in_specs=[pl.BlockSpec(memory_space=pltpu.SMEM), pl.BlockSpec(idx.shape, lambda: (0, 0))],
        out_specs=pl.BlockSpec(idx.shape, lambda: (0, 0)),
        out_shape=jax.ShapeDtypeStruct(idx.shape, F32),
        name="bias_tables",
    )(rel_bias.astype(F32), idx)


def _bias_idx(dist, head, ok):
    nh = KVH * GRP
    return np.where(ok & (dist >= 0), _bucket_np(dist) * nh + head, -1).astype(np.int32)


def _cumsum_body(pt_ref, *refs, gp, ns, has_new):
    pages = refs[:gp]
    rest = refs[gp:]
    new_ref = rest[0] if has_new else None
    c_o, ct_o, pad_s, carry_s = rest[1:] if has_new else rest
    st = pl.program_id(1)

    @pl.when(st == 0)
    def _():
        pad_s[...] = jnp.zeros_like(pad_s)
        carry_s[...] = jnp.zeros_like(carry_s)

    ltri = (_iota((PAGE, PAGE), 1) <= _iota((PAGE, PAGE), 0)).astype(BF16)

    def one_page(x, w):
        pad_s[:, 0:FOX_HEADS] = x
        xp = pad_s[...]
        cs = sum(_dot(ltri, t.astype(BF16)) for t in _split3(xp)) + carry_s[...]
        c_o[0, w * PAGE:(w + 1) * PAGE, :] = cs[:, 0:FOX_HEADS]
        ct_o[0, :, w * PAGE:(w + 1) * PAGE] = cs.T[0:FOX_HEADS, :]
        carry_s[...] = cs[PAGE - 1:PAGE, :]

    @pl.when(st < ns)
    def _():
        for w in range(gp):
            one_page(pages[w][0], w)

    if has_new:
        @pl.when(st == ns)
        def _():
            one_page(new_ref[0], 0)


def _paged_cumsum(pool, page_table, gp, new_page=None):
    b, npg = page_table.shape
    ns = npg // gp
    has_new = new_page is not None
    nsteps = ns + (1 if has_new else 0)
    stc = lambda st: jnp.minimum(st, ns - 1)
    in_specs = [pl.BlockSpec((1, PAGE, FOX_HEADS), functools.partial(
        lambda i, st, pt, w: (pt[i, stc(st) * gp + w], 0, 0), w=w)) for w in range(gp)]
    args = [pool] * gp
    if has_new:
        in_specs.append(pl.BlockSpec((1, PAGE, FOX_HEADS), lambda i, st, pt: (i, 0, 0)))
        args.append(new_page)
    width = gp * PAGE
    grid_spec = pltpu.PrefetchScalarGridSpec(
        num_scalar_prefetch=1,
        grid=(b, nsteps),
        in_specs=in_specs,
        out_specs=[pl.BlockSpec((1, width, FOX_HEADS), lambda i, st, pt: (i, st, 0)),
                   pl.BlockSpec((1, FOX_HEADS, width), lambda i, st, pt: (i, 0, st))],
        scratch_shapes=[pltpu.VMEM((PAGE, LANE), F32), pltpu.VMEM((1, LANE), F32)],
    )
    return pl.pallas_call(
        functools.partial(_cumsum_body, gp=gp, ns=ns, has_new=has_new),
        grid_spec=grid_spec,
        out_shape=[jax.ShapeDtypeStruct((b, nsteps * width, FOX_HEADS), F32),
                   jax.ShapeDtypeStruct((b, FOX_HEADS, nsteps * width), F32)],
        compiler_params=_cparams(("arbitrary", "arbitrary")),
        name="logf_cumsum",
    )(page_table, *args)


def _compress_body(pt_ref, *refs, gp, nrows):
    pages = refs[:gp]
    w1_ref, pe_ref, w2_ref, out_ref, buf_k, buf_v = refs[gp:]
    st = pl.program_id(1)
    ns = nrows // (gp * PAGE)
    nb = nrows // CMP_STRIDE
    bufs = (buf_k, buf_v)

    @pl.when(st == 0)
    def _():
        for buf in bufs:
            buf[nrows:nrows + CMP_BLOCK, :] = jnp.zeros((CMP_BLOCK, LANE), F32)

    for w in range(gp):
        base = pl.multiple_of((st * gp + w) * PAGE, PAGE)
        for typ, buf in enumerate(bufs):
            buf[pl.ds(base, PAGE), :] = pages[w][0, :, typ * LANE:(typ + 1) * LANE]

    @pl.when(st == ns - 1)
    def _():
        for typ, buf in enumerate(bufs):
            acc = jnp.zeros((nb, 2 * LANE), F32)
            for lp in range(CMP_BLOCK // 2):
                xa = buf[pl.ds(2 * lp, nb, stride=CMP_STRIDE), :] + pe_ref[typ, 2 * lp:2 * lp + 1, :]
                xb = buf[pl.ds(2 * lp + 1, nb, stride=CMP_STRIDE), :] + pe_ref[typ, 2 * lp + 1:2 * lp + 2, :]
                x2 = jnp.concatenate([xa, xb], axis=1).astype(MXU)
                acc = acc + _dot(x2, w1_ref[typ, lp])
            hid = jax.nn.gelu(acc)
            out_ref[0, :, typ * LANE:(typ + 1) * LANE] = _dot(hid.astype(MXU), w2_ref[typ])


def _compress(pool, page_table, pe, w1, w2, gp):
    b, npg = page_table.shape
    nrows = npg * PAGE
    nb = nrows // CMP_STRIDE
    z = jnp.zeros_like(w1)
    w1bd = jnp.concatenate([jnp.concatenate([w1, z], axis=-1), jnp.concatenate([z, w1], axis=-1)], axis=-2)
    w1pair = w1bd.reshape(2, CMP_BLOCK // 2, 4 * HEAD_DIM, w1bd.shape[-1]).astype(MXU)
    z2 = jnp.zeros_like(w2)
    w2bd = jnp.concatenate([jnp.concatenate([w2, z2], axis=-1), jnp.concatenate([z2, w2], axis=-1)], axis=-2).astype(MXU)
    pe2 = jnp.concatenate([pe, pe], axis=-1).astype(F32)
    page_spec = lambda w: pl.BlockSpec((1, PAGE, 2 * LANE), lambda i, st, pt: (pt[i, st * gp + w], 0, 0))
    grid_spec = pltpu.PrefetchScalarGridSpec(
        num_scalar_prefetch=1,
        grid=(b, npg // gp),
        in_specs=[page_spec(w) for w in range(gp)]
                 + [pl.BlockSpec(w1pair.shape, lambda i, st, pt: (0, 0, 0, 0)),
                    pl.BlockSpec(pe2.shape, lambda i, st, pt: (0, 0, 0)),
                    pl.BlockSpec(w2bd.shape, lambda i, st, pt: (0, 0, 0))],
        out_specs=pl.BlockSpec((1, nb, 2 * LANE), lambda i, st, pt: (i, 0, 0)),
        scratch_shapes=[pltpu.VMEM((nrows + CMP_BLOCK, LANE), F32), pltpu.VMEM((nrows + CMP_BLOCK, LANE), F32)],
    )
    return pl.pallas_call(
        functools.partial(_compress_body, gp=gp, nrows=nrows),
        grid_spec=grid_spec,
        out_shape=jax.ShapeDtypeStruct((b, nb, 2 * LANE), F32),
        compiler_params=_cparams(("arbitrary", "arbitrary")),
        name="nsa_compress",
    )(page_table, *([pool] * gp), w1pair, pe2, w2bd)


def _fox_prep_body(q_ref, k_ref, c_ref, qa_o, ka_o):
    tp = q_ref.shape[1]
    lane = _iota((tp, LANE), 1)
    q = q_ref[0]
    k = k_ref[0]
    c = c_ref[0] * LOG2E
    for h in range(FOX_HEADS):
        qs = q[:, (h // 2) * LANE:(h // 2 + 1) * LANE]
        ks = k[:, (h // 2) * LANE:(h // 2 + 1) * LANE]
        if h % 2 == 1:
            qs = pltpu.roll(qs, HEAD_DIM, 1)
            ks = pltpu.roll(ks, HEAD_DIM, 1)
        chi, cmid, clo = _split3(jnp.broadcast_to(c[:, h:h + 1], (tp, LANE)))
        one = jnp.ones((tp, LANE), F32)
        zero = jnp.zeros((tp, LANE), F32)
        qaug = jnp.where(lane == HEAD_DIM, chi, jnp.where(lane == HEAD_DIM + 1, cmid, jnp.where(
            lane == HEAD_DIM + 2, clo, jnp.where(lane < HEAD_DIM + 6, one, zero))))
        kaug = jnp.where(lane == HEAD_DIM + 3, -chi, jnp.where(lane == HEAD_DIM + 4, -cmid, jnp.where(
            lane == HEAD_DIM + 5, -clo, jnp.where(lane < HEAD_DIM + 3, one, zero))))
        qa_o[0, h] = jnp.where(lane < HEAD_DIM, qs * QSCALE, qaug).astype(qa_o.dtype)
        ka_o[0, h] = jnp.where(lane < HEAD_DIM, ks, kaug).astype(ka_o.dtype)


def _fox_prep(fq, fkv, c, tp):
    b, t, _ = fq.shape
    return pl.pallas_call(
        _fox_prep_body,
        grid=(b, t // tp),
        in_specs=[pl.BlockSpec((1, tp, FOX_W), lambda i, j: (i, j, 0)),
                  pl.BlockSpec((1, tp, FOX_W), lambda i, j: (i, j, 0)),
                  pl.BlockSpec((1, tp, FOX_HEADS), lambda i, j: (i, j, 0))],
        out_specs=[pl.BlockSpec((1, FOX_HEADS, tp, LANE), lambda i, j: (i, 0, j, 0)),
                   pl.BlockSpec((1, FOX_HEADS, tp, LANE), lambda i, j: (i, 0, j, 0))],
        out_shape=[jax.ShapeDtypeStruct((b, FOX_HEADS, t, LANE), MXU), jax.ShapeDtypeStruct((b, FOX_HEADS, t, LANE), MXU)],
        compiler_params=_cparams(("parallel", "parallel")),
        name="fox_prep",
    )(fq, fkv, c)


def _pfox_body(q_ref, k_ref, v_ref, o_ref, m_s, l_s, acc_s, *, tq, tk, nk):
    qi = pl.program_id(2)
    kj = pl.program_id(3)
    last = ((qi + 1) * tq - 1) // tk

    @pl.when(kj == 0)
    def _():
        m_s[...] = jnp.full_like(m_s, NEG_INF)
        l_s[...] = jnp.zeros_like(l_s)
        acc_s[...] = jnp.zeros_like(acc_s)

    def update(masked):
        v = v_ref[0].astype(MXU)
        for hh in range(2):
            s = _dot_nt(q_ref[0, hh], k_ref[0, hh])
            if masked:
                causal = (kj * tk + _iota((tq, tk), 1)) <= (qi * tq + _iota((tq, tk), 0))
                s = jnp.where(causal, s, NEG_INF)
            alpha, p = _online_update(s, m_s.at[hh], l_s.at[hh])
            acc_s[hh] = alpha * acc_s[hh] + _dot(p.astype(MXU), v)

    crosses = (kj + 1) * tk - 1 > qi * tq

    @pl.when((kj <= last) & jnp.logical_not(crosses))
    def _():
        update(False)

    @pl.when((kj <= last) & crosses)
    def _():
        update(True)

    @pl.when(kj == nk - 1)
    def _():
        lane = _iota((tq, LANE), 1)
        o0 = acc_s[0] * (1.0 / l_s[0])
        o1 = acc_s[1] * (1.0 / l_s[1])
        o_ref[0] = jnp.where(lane < HEAD_DIM, o0, o1)


def _prompt_fox(qa, ka, fkv, tq, tk):
    b, _, t, _ = qa.shape
    nq, nk = t // tq, t // tk
    hp = FOX_HEADS // 2
    lastf = lambda qi: ((qi + 1) * tq - 1) // tk
    return pl.pallas_call(
        functools.partial(_pfox_body, tq=tq, tk=tk, nk=nk),
        grid=(b, hp, nq, nk),
        in_specs=[pl.BlockSpec((1, 2, tq, LANE), lambda i, h, qi, kj: (i, h, qi, 0)),
                  pl.BlockSpec((1, 2, tk, LANE), lambda i, h, qi, kj: (i, h, jnp.minimum(kj, lastf(qi)), 0)),
                  pl.BlockSpec((1, tk, LANE), lambda i, h, qi, kj: (i, jnp.minimum(kj, lastf(qi)), hp + h))],
        out_specs=pl.BlockSpec((1, tq, LANE), lambda i, h, qi, kj: (i, qi, h)),
        out_shape=jax.ShapeDtypeStruct((b, t, FOX_W), F32),
        scratch_shapes=[pltpu.VMEM((2, tq, 1), F32), pltpu.VMEM((2, tq, 1), F32), pltpu.VMEM((2, tq, LANE), F32)],
        compiler_params=_cparams(("parallel", "parallel", "parallel", "arbitrary")),
        name="prompt_fox",
    )(qa, ka, fkv)


QB = 128
NWIN = WINDOW // QB + 1


def _pnsa_body(q_ref, g_ref, kc_ref, vc_ref, ks_ref, vs_ref, e_ref, *refs, t_len, tk, nk):
    wk = refs[:NWIN]
    wv = refs[NWIN:2 * NWIN]
    ovt_ref, band_ref, bandc_ref, o_ref, qa_s, imp_s, oc_s, ow_s, m_s, l_s, acc_s = refs[2 * NWIN:]
    qi = pl.program_id(1)
    kj = pl.program_id(2)
    t0 = qi * QB
    last = (t0 + QB - 1) // tk
    njp = imp_s.shape[0]
    nc = t_len // CMP_STRIDE
    n_c = (t_len - CMP_BLOCK) // CMP_STRIDE + 1
    nhg = KVH * GRP
    rows = nhg * QB
    band_lo = lambda: band_ref[:, 0:QB]
    band_hi = lambda: band_ref[:, QB:2 * QB]

    @pl.when(kj == 0)
    def _():
        q = q_ref[0]
        lane = _iota((QB, LANE), 1)
        parts = []
        for hg in range(nhg):
            h = hg // GRP
            sl = q[:, (hg // 2) * LANE:(hg // 2 + 1) * LANE]
            if hg % 2 != h:
                sl = pltpu.roll(sl, HEAD_DIM, 1)
            parts.append(jnp.where((lane // HEAD_DIM) == h, sl, 0.0))
        qf = (jnp.concatenate(parts, axis=0) * QSCALE).astype(MXU)

        kcb = kc_ref[0].astype(MXU)
        vcb = vc_ref[0].astype(MXU)
        shift_t = ((_iota((nc, LANE), 0) == 8 * qi - 9 + _iota((nc, LANE), 1))
                   & (_iota((nc, LANE), 1) < 16)).astype(MXU)
        lhs = jnp.concatenate([qf] + [t.astype(MXU) for t in _split3(bandc_ref[...])], axis=1)
        rhs = jnp.concatenate([kcb, shift_t, shift_t, shift_t], axis=1)
        s = _dot_nt(lhs, rhs)
        trow = t0 + (_iota((rows, nc), 0) % QB)
        ncol = _iota((rows, nc), 1)
        maskc = (CMP_STRIDE * ncol + (CMP_BLOCK - 1) <= trow) & (ncol < n_c)
        pc = _softmax2_rows(jnp.where(maskc, s, NEG_INF))
        oc_s[...] = _dot(pc.astype(MXU), vcb)
        pcs = []
        for h in range(KVH):
            a = pc[h * GRP * QB:(h * GRP + 1) * QB]
            for g in range(1, GRP):
                a = a + pc[(h * GRP + g) * QB:(h * GRP + g + 1) * QB]
            pcs.append(a)
        pcs = jnp.concatenate(pcs, axis=0)
        imp = _dot_nt(ovt_ref[...], pcs.astype(MXU))
        width = KVH * QB
        jr = _iota((njp, width), 0)
        qblk = (t0 + _iota((njp, width), 1) % QB) // SEL_BLOCK
        forced = (jr == 0) | (jr == qblk) | (jr == qblk - 1)
        imp = jnp.where(forced, jnp.inf, jnp.where(jr > qblk, NEG_INF, imp))
        imp_s[...] = imp

        def rank_step(jp, cnt):
            row = imp_s[pl.ds(jp, 1), :]
            beats = (row > imp) | ((row == imp) & (jp < jr))
            return cnt + beats.astype(F32)

        n_live = jnp.minimum(njp, (t0 + QB - 1) // SEL_BLOCK + 1)
        cnt = lax.fori_loop(0, n_live, rank_step, jnp.zeros((njp, width), F32))
        pen_t = jnp.where((cnt < N_SELECT) & (imp > NEG_INF), 0.0, -MASK_BIG).astype(MXU)
        eye = (_iota((width, width), 0) == _iota((width, width), 1)).astype(MXU)
        pen = _dot_nt(eye, pen_t).astype(MXU)
        pens = jnp.concatenate([pen[h * QB:(h + 1) * QB] for h in range(KVH) for _ in range(GRP)], axis=0)
        qa_s[...] = jnp.concatenate([qf, pens], axis=1)

        ri = _iota((rows, QB), 0) % QB
        ci = _iota((rows, QB), 1)
        scores = []
        for w in range(NWIN):
            sw = _dot_nt(qf, wk[w][0].astype(MXU))
            if w == NWIN - 1:
                sw = sw + band_hi()
            elif w == NWIN - 2:
                sw = sw + band_lo()
            dw = WINDOW - QB * w + ri - ci
            mask = (dw >= 0) & (dw < WINDOW) & (qi - (NWIN - 1) + w >= 0)
            scores.append(jnp.where(mask, sw, NEG_INF))
        pw = _softmax2_rows(jnp.concatenate(scores, axis=1))
        ow = jnp.zeros((rows, LANE), F32)
        for w in range(NWIN):
            ow = ow + _dot(pw[:, w * QB:(w + 1) * QB].astype(MXU), wv[w][0].astype(MXU))
        ow_s[...] = ow

        m_s[...] = jnp.full_like(m_s, NEG_INF)
        l_s[...] = jnp.zeros_like(l_s)
        acc_s[...] = jnp.zeros_like(acc_s)

    def step(near):
        kcat = jnp.concatenate([ks_ref[0].astype(MXU), e_ref[...]], axis=1)
        s = _dot_nt(qa_s[...], kcat)
        if near:
            trow = t0 + (_iota((rows, tk), 0) % QB)
            scol = kj * tk + _iota((rows, tk), 1)
            chunks = []
            for c in range(tk // QB):
                delta = qi - (kj * (tk // QB) + c)
                chunks.append(jnp.where(delta == 0, band_hi(), jnp.where(delta == 1, band_lo(), 0.0)))
            s = jnp.where(scol <= trow, s + jnp.concatenate(chunks, axis=1), NEG_INF)
        alpha, p = _online_update(s, m_s, l_s)
        acc_s[...] = alpha * acc_s[...] + _dot(p.astype(MXU), vs_ref[0].astype(MXU))

    @pl.when(kj < last - 1)
    def _():
        step(False)

    @pl.when((kj >= last - 1) & (kj <= last))
    def _():
        step(True)

    @pl.when(kj == nk - 1)
    def _():
        osl = acc_s[...] * (1.0 / l_s[...])
        gt = g_ref[0]

        def gate(k):
            off = FOX_HEADS + k
            return jnp.concatenate([gt[:, off + hg * 3:off + hg * 3 + 1] for hg in range(nhg)], axis=0)

        of = gate(0) * oc_s[...] + gate(1) * osl + gate(2) * ow_s[...]
        lane = _iota((QB, LANE), 1)
        for pr in range(nhg // 2):
            h = (2 * pr) // GRP
            ev = of[(2 * pr) * QB:(2 * pr + 1) * QB]
            od = of[(2 * pr + 1) * QB:(2 * pr + 2) * QB]
            if h == 0:
                od = pltpu.roll(od, HEAD_DIM, 1)
            else:
                ev = pltpu.roll(ev, HEAD_DIM, 1)
            o_ref[0, :, pr * LANE:(pr + 1) * LANE] = jnp.where(lane < HEAD_DIM, ev, od)


def _prompt_nsa(nq, ng, cmp2, nkv, kw, e_mat, ovt, band, bandc, tk):
    b, t, _ = nq.shape
    nqb, nk = t // QB, t // tk
    nc = cmp2.shape[1]
    njp = ovt.shape[0]
    rows = KVH * GRP * QB
    lastf = lambda qi: (qi * QB + QB - 1) // tk
    kjc = lambda qi, kj: jnp.minimum(kj, lastf(qi))
    wspec = lambda w, part: pl.BlockSpec((1, QB, LANE), lambda i, qi, kj: (i, jnp.maximum(qi - (NWIN - 1) + w, 0), part))
    const = lambda a: pl.BlockSpec(a.shape, lambda i, qi, kj: (0,) * a.ndim)
    return pl.pallas_call(
        functools.partial(_pnsa_body, t_len=t, tk=tk, nk=nk),
        grid=(b, nqb, nk),
        in_specs=[pl.BlockSpec((1, QB, NSA_Q_W), lambda i, qi, kj: (i, qi, 0)),
                  pl.BlockSpec((1, QB, LANE), lambda i, qi, kj: (i, qi, 0)),
                  pl.BlockSpec((1, nc, LANE), lambda i, qi, kj: (i, 0, 0)),
                  pl.BlockSpec((1, nc, LANE), lambda i, qi, kj: (i, 0, 1)),
                  pl.BlockSpec((1, tk, LANE), lambda i, qi, kj: (i, kjc(qi, kj), 2)),
                  pl.BlockSpec((1, tk, LANE), lambda i, qi, kj: (i, kjc(qi, kj), 3)),
                  pl.BlockSpec((tk, njp), lambda i, qi, kj: (kjc(qi, kj), 0))]
                 + [wspec(w, 0) for w in range(NWIN)] + [wspec(w, 1) for w in range(NWIN)]
                 + [const(ovt), const(band), const(bandc)],
        out_specs=pl.BlockSpec((1, QB, NSA_Q_W), lambda i, qi, kj: (i, qi, 0)),
        out_shape=jax.ShapeDtypeStruct((b, t, NSA_Q_W), F32),
        scratch_shapes=[pltpu.VMEM((rows, LANE + njp), MXU), pltpu.VMEM((njp, KVH * QB), F32),
                        pltpu.VMEM((rows, LANE), F32), pltpu.VMEM((rows, LANE), F32),
                        pltpu.VMEM((rows, 1), F32), pltpu.VMEM((rows, 1), F32), pltpu.VMEM((rows, LANE), F32)],
        compiler_params=_cparams(("parallel", "parallel", "arbitrary")),
        name="prompt_nsa",
    )(nq, ng, cmp2, cmp2, nkv, nkv, e_mat, *([kw] * (2 * NWIN)), ovt, band, bandc)


def _dfox_body(pt_ref, *refs, gp, ns, s_new):
    pages = refs[:gp]
    q_ref, cq_ref, ck_ref, ckn_ref, new_ref, o_ref, m_s, l_s, acc_s = refs[gp:]
    st = pl.program_id(1)
    rows = FOX_HEADS * s_new
    qb = (q_ref[0] * QSCALE).astype(MXU)

    @pl.when(st == 0)
    def _():
        m_s[...] = jnp.full_like(m_s, NEG_INF)
        l_s[...] = jnp.zeros_like(l_s)
        acc_s[...] = jnp.zeros_like(acc_s)

    @pl.when(st < ns)
    def _():
        kv = [pg[0].astype(MXU) for pg in pages]
        s = jnp.concatenate([_dot_nt(qb, x[:, 0:FOX_W]) for x in kv], axis=1)
        ck = jnp.concatenate([ck_ref[0]] * s_new, axis=0)
        lg = s + (cq_ref[0] - ck) * LOG2E
        alpha, p = _online_update(lg, m_s, l_s)
        pv = jnp.zeros((rows, FOX_W), F32)
        for w in range(gp):
            pv = pv + _dot(p[:, w * PAGE:(w + 1) * PAGE].astype(MXU), kv[w][:, FOX_W:2 * FOX_W])
        acc_s[...] = alpha * acc_s[...] + pv

    @pl.when(st == ns)
    def _():
        x = new_ref[0].astype(MXU)
        s = _dot_nt(qb, x[:, 0:FOX_W])
        ck = jnp.concatenate([ckn_ref[0]] * s_new, axis=0)
        qrow = _iota((rows, PAGE), 0) // FOX_HEADS
        kcol = _iota((rows, PAGE), 1)
        lg = jnp.where(kcol <= qrow, s + (cq_ref[0] - ck) * LOG2E, NEG_INF)
        alpha, p = _online_update(lg, m_s, l_s)
        acc = alpha * acc_s[...] + _dot(p.astype(MXU), x[:, FOX_W:2 * FOX_W])
        o = acc * (1.0 / l_s[...])
        own = (_iota((rows, FOX_W), 1) // HEAD_DIM) == (_iota((rows, FOX_W), 0) % FOX_HEADS)
        o = jnp.where(own, o, 0.0)
        o_ref[0] = jnp.sum(o.reshape(s_new, FOX_HEADS, FOX_W), axis=1)


def _decode_fox(cache2d, page_table, qrows, cq, ct_all, new_pad, gp):
    b, npg = page_table.shape
    ns = npg // gp
    rows = qrows.shape[1]
    s_new = rows // FOX_HEADS
    stc = lambda st: jnp.minimum(st, ns - 1)
    page_spec = lambda w: pl.BlockSpec((1, PAGE, 2 * FOX_W), lambda i, st, pt: (pt[i, stc(st) * gp + w], 0, 0))
    grid_spec = pltpu.PrefetchScalarGridSpec(
        num_scalar_prefetch=1,
        grid=(b, ns + 1),
        in_specs=[page_spec(w) for w in range(gp)]
                 + [pl.BlockSpec((1, rows, FOX_W), lambda i, st, pt: (i, 0, 0)),
                    pl.BlockSpec((1, rows, 1), lambda i, st, pt: (i, 0, 0)),
                    pl.BlockSpec((1, FOX_HEADS, gp * PAGE), lambda i, st, pt: (i, 0, stc(st))),
                    pl.BlockSpec((1, FOX_HEADS, PAGE), lambda i, st, pt: (i, 0, npg)),
                    pl.BlockSpec((1, PAGE, 2 * FOX_W), lambda i, st, pt: (i, 0, 0))],
        out_specs=pl.BlockSpec((1, s_new, FOX_W), lambda i, st, pt: (i, 0, 0)),
        scratch_shapes=[pltpu.VMEM((rows, 1), F32), pltpu.VMEM((rows, 1), F32), pltpu.VMEM((rows, FOX_W), F32)],
    )
    return pl.pallas_call(
        functools.partial(_dfox_body, gp=gp, ns=ns, s_new=s_new),
        grid_spec=grid_spec,
        out_shape=jax.ShapeDtypeStruct((b, s_new, FOX_W), F32),
        compiler_params=_cparams(("parallel", "arbitrary")),
        name="decode_fox",
    )(page_table, *([cache2d] * gp), qrows, cq, ct_all, ct_all, new_pad)


def _dnsa_body(pt_ref, *refs, gp, ns, nj, njp):
    pages = refs[:gp]
    (q_ref, g_ref, kc_ref, vc_ref, ovt_ref, bcmp_ref, bwin_ref, blast_ref, bnew_ref,
     win_ref, wnew_ref, snew_ref, o_ref, sel_s, imp_s, oc_s, ow_s, m_s, l_s, acc_s) = refs[gp:]
    st = pl.program_id(1)
    rows = q_ref.shape[1]
    rq = rows // GRP
    qb = (q_ref[0] * QSCALE).astype(MXU)

    @pl.when(st == 0)
    def _():
        kc = kc_ref[0].astype(MXU)
        vc = vc_ref[0].astype(MXU)
        pc = _softmax2_rows(_dot_nt(qb, kc) + bcmp_ref[...])
        oc_s[...] = _dot(pc.astype(MXU), vc)
        pcs = pc[0:rq]
        for g in range(1, GRP):
            pcs = pcs + pc[g * rq:(g + 1) * rq]
        pcs = jnp.concatenate([pcs, jnp.zeros((LANE - rq, pcs.shape[1]), F32)], axis=0)
        imp = _dot_nt(ovt_ref[...], pcs.astype(MXU))
        jr = _iota((njp, LANE), 0)
        qblk = ((nj - 1) * SEL_BLOCK + _iota((njp, LANE), 1) // KVH) // SEL_BLOCK
        forced = (jr == 0) | (jr == qblk) | (jr == qblk - 1)
        imp = jnp.where(forced, jnp.inf, jnp.where(jr > qblk, NEG_INF, imp))
        imp_s[...] = imp

        def rank_step(jp, cnt):
            row = imp_s[pl.ds(jp, 1), :]
            beats = (row > imp) | ((row == imp) & (jp < jr))
            return cnt + beats.astype(F32)

        cnt = lax.fori_loop(0, nj, rank_step, jnp.zeros((njp, LANE), F32))
        sel_t = ((cnt < N_SELECT) & (imp > NEG_INF)).astype(BF16)
        eye = (_iota((LANE, LANE), 0) == _iota((LANE, LANE), 1)).astype(BF16)
        sel = _dot_nt(eye, sel_t).astype(BF16)
        sel_s[...] = jnp.concatenate([sel[0:rq]] * GRP, axis=0)

        win = win_ref[0].astype(MXU)
        wnew = wnew_ref[0].astype(MXU)
        bw = bwin_ref[...]
        sw = jnp.concatenate([_dot_nt(qb, win[:, 0:LANE]) + bw[:, 0:WINDOW],
                              _dot_nt(qb, wnew[:, 0:LANE]) + bw[:, WINDOW:WINDOW + PAGE]], axis=1)
        pw = _softmax2_rows(sw)
        ow_s[...] = (_dot(pw[:, 0:WINDOW].astype(MXU), win[:, LANE:2 * LANE])
                     + _dot(pw[:, WINDOW:WINDOW + PAGE].astype(MXU), wnew[:, LANE:2 * LANE]))

        m_s[...] = jnp.full_like(m_s, NEG_INF)
        l_s[...] = jnp.zeros_like(l_s)
        acc_s[...] = jnp.zeros_like(acc_s)

    @pl.when(st < ns)
    def _():
        kv = [pg[0].astype(MXU) for pg in pages]
        s = jnp.concatenate([_dot_nt(qb, x[:, 0:LANE]) for x in kv], axis=1)
        width = gp * PAGE
        expand = (_iota((njp, width), 0) == (st * width + _iota((njp, width), 1)) // SEL_BLOCK).astype(BF16)
        mexp = _dot(sel_s[...], expand)
        tail = jnp.where(st == ns - 1, blast_ref[...], 0.0)
        if gp > 1:
            tail = jnp.concatenate([jnp.zeros((rows, width - PAGE), F32), tail], axis=1)
        lg = jnp.where(mexp > 0.5, s + tail, NEG_INF)
        alpha, p = _online_update(lg, m_s, l_s)
        pv = jnp.zeros((rows, LANE), F32)
        for w in range(gp):
            pv = pv + _dot(p[:, w * PAGE:(w + 1) * PAGE].astype(MXU), kv[w][:, LANE:2 * LANE])
        acc_s[...] = alpha * acc_s[...] + pv

    @pl.when(st == ns)
    def _():
        x = snew_ref[0].astype(MXU)
        s = _dot_nt(qb, x[:, 0:LANE]) + bnew_ref[...]
        picked = sel_s[:, nj - 1:nj].astype(F32) > 0.5
        lg = jnp.where(picked, s, NEG_INF)
        alpha, p = _online_update(lg, m_s, l_s)
        acc = alpha * acc_s[...] + _dot(p.astype(MXU), x[:, LANE:2 * LANE])
        osl = acc * (1.0 / jnp.maximum(l_s[...], 1e-30))
        gt = g_ref[0]
        of = gt[:, 0:1] * oc_s[...] + gt[:, 1:2] * osl + gt[:, 2:3] * ow_s[...]
        head = _iota((rows, LANE), 0) % KVH
        of = jnp.where(head == 0, of, pltpu.roll(of, HEAD_DIM, 1))
        o_ref[0] = of[:, 0:HEAD_DIM]


def _decode_nsa(cache2d, page_table, qrows, gates, cmp2, ovt, bcmp, bwin, blast, bnew, win2d, wnew_pad,
                snew_pad, gp, nj):
    b, npg = page_table.shape
    ns = npg // gp
    rows = qrows.shape[1]
    ncp = cmp2.shape[1]
    njp = ovt.shape[0]
    stc = lambda st: jnp.minimum(st, ns - 1)
    page_spec = lambda w: pl.BlockSpec((1, PAGE, 2 * LANE), lambda i, st, pt: (pt[i, stc(st) * gp + w], 0, 1))
    const2 = lambda a: pl.BlockSpec(a.shape, lambda i, st, pt: (0, 0))
    grid_spec = pltpu.PrefetchScalarGridSpec(
        num_scalar_prefetch=1,
        grid=(b, ns + 1),
        in_specs=[page_spec(w) for w in range(gp)]
                 + [pl.BlockSpec((1, rows, LANE), lambda i, st, pt: (i, 0, 0)),
                    pl.BlockSpec((1, rows, 3), lambda i, st, pt: (i, 0, 0)),
                    pl.BlockSpec((1, ncp, LANE), lambda i, st, pt: (i, 0, 0)),
                    pl.BlockSpec((1, ncp, LANE), lambda i, st, pt: (i, 0, 1)),
                    const2(ovt), const2(bcmp), const2(bwin), const2(blast), const2(bnew),
                    pl.BlockSpec((1, WINDOW, 2 * LANE), lambda i, st, pt: (i, 0, 0)),
                    pl.BlockSpec((1, PAGE, 2 * LANE), lambda i, st, pt: (i, 0, 0)),
                    pl.BlockSpec((1, PAGE, 2 * LANE), lambda i, st, pt: (i, 0, 1))],
        out_specs=pl.BlockSpec((1, rows, HEAD_DIM), lambda i, st, pt: (i, 0, 0)),
        scratch_shapes=[pltpu.VMEM((rows, njp), BF16), pltpu.VMEM((njp, LANE), F32),
                        pltpu.VMEM((rows, LANE), F32), pltpu.VMEM((rows, LANE), F32),
                        pltpu.VMEM((rows, 1), F32), pltpu.VMEM((rows, 1), F32), pltpu.VMEM((rows, LANE), F32)],
    )
    return pl.pallas_call(
        functools.partial(_dnsa_body, gp=gp, ns=ns, nj=nj, njp=njp),
        grid_spec=grid_spec,
        out_shape=jax.ShapeDtypeStruct((b, rows, HEAD_DIM), F32),
        compiler_params=_cparams(("parallel", "arbitrary")),
        name="decode_nsa",
    )(page_table, *([cache2d] * gp), qrows, gates, cmp2, cmp2, ovt, bcmp, bwin, blast, bnew, win2d, wnew_pad,
      snew_pad)


def _merge_body(on_ref, of_ref, ga_ref, gb_ref, x_ref, wa, wb, wo, g1, b1, o_ref, *, alpha):
    a = _dot(on_ref[...].astype(MXU), wa[...])
    bb = _dot(of_ref[...].astype(MXU), wb[...])
    mix = _dot((ga_ref[...] * a + gb_ref[...] * bb).astype(MXU), wo[...])
    o_ref[...] = _layer_norm(alpha * x_ref[...] + mix, g1[...], b1[...])


def _merge(o_nsa, o_fox, sga, sgb, x2d, wa, wb, wo, g1, b1, tm, alpha):
    m, d = x2d.shape
    row = lambda n: pl.BlockSpec((tm, n), lambda i: (i, 0))
    full = lambda a: pl.BlockSpec(a.shape, lambda i: (0, 0))
    wa, wb, wo = wa.astype(MXU), wb.astype(MXU), wo.astype(MXU)
    g1, b1 = g1.reshape(1, d), b1.reshape(1, d)
    return pl.pallas_call(
        functools.partial(_merge_body, alpha=alpha),
        grid=(m // tm,),
        in_specs=[row(NSA_Q_W), row(FOX_W), row(d), row(d), row(d), full(wa), full(wb), full(wo), full(g1), full(b1)],
        out_specs=row(d),
        out_shape=jax.ShapeDtypeStruct((m, d), F32),
        compiler_params=_cparams(("parallel",)),
        name="merge_ln1",
    )(o_nsa, o_fox, sga, sgb, x2d, wa, wb, wo, g1, b1)


def _ffn_up_long_body(x_ref, prev_ref, wu, wg, cw, cb, h_ref, st_ref, carry_s, *, tiles_per_seq):
    i = pl.program_id(0)
    tm = x_ref.shape[0]
    xb = x_ref[...].astype(MXU)
    u = _dot(xb, wu[...])

    @pl.when(i % tiles_per_seq == 0)
    def _():
        carry_s[...] = prev_ref[0]

    r = _iota(u.shape, 0)
    um1 = jnp.where(r >= 1, pltpu.roll(u, 1, 0), carry_s[1:2, :])
    um2 = jnp.where(r >= 2, pltpu.roll(u, 2, 0), jnp.where(r == 0, carry_s[0:1, :], carry_s[1:2, :]))
    conv = cb[...] + cw[0:1, :] * um2 + cw[1:2, :] * um1 + cw[2:3, :] * u
    h_ref[...] = (jax.nn.gelu(conv) * _dot(xb, wg[...])).astype(h_ref.dtype)
    last2 = u[tm - 2:tm, :]
    carry_s[...] = last2
    st_ref[0] = last2


def _ffn_up_short_body(x_ref, p1_ref, p2_ref, wu, wg, cw, cb, h_ref, u_ref, *, seq):
    xb = x_ref[...].astype(MXU)
    u = _dot(xb, wu[...])
    t = _iota(u.shape, 0) % seq
    um1 = jnp.where(t >= 1, pltpu.roll(u, 1, 0), p1_ref[...])
    um2 = jnp.where(t >= 2, pltpu.roll(u, 2, 0), p2_ref[...])
    conv = cb[...] + cw[0:1, :] * um2 + cw[1:2, :] * um1 + cw[2:3, :] * u
    h_ref[...] = (jax.nn.gelu(conv) * _dot(xb, wg[...])).astype(h_ref.dtype)
    u_ref[...] = u


def _ffn_up_long(x1, conv_prev, wu, wg, cw, cb, seq, tm):
    m, d = x1.shape
    f = wu.shape[1]
    nseq = m // seq
    tps = seq // tm
    wu, wg = wu.astype(MXU), wg.astype(MXU)
    cb = cb.reshape(1, f)
    full = lambda a: pl.BlockSpec(a.shape, lambda i: (0, 0))
    return pl.pallas_call(
        functools.partial(_ffn_up_long_body, tiles_per_seq=tps),
        grid=(m // tm,),
        in_specs=[pl.BlockSpec((tm, d), lambda i: (i, 0)),
                  pl.BlockSpec((1, 2, f), lambda i: (i // tps, 0, 0)),
                  full(wu), full(wg), full(cw), full(cb)],
        out_specs=[pl.BlockSpec((tm, f), lambda i: (i, 0)),
                   pl.BlockSpec((1, 2, f), lambda i: (i // tps, 0, 0))],
        out_shape=[jax.ShapeDtypeStruct((m, f), MXU), jax.ShapeDtypeStruct((nseq, 2, f), F32)],
        scratch_shapes=[pltpu.VMEM((2, f), F32)],
        compiler_params=_cparams(("arbitrary",)),
        name="ffn_up_long",
    )(x1, conv_prev, wu, wg, cw, cb)


def _ffn_up_short(x1, conv_prev, wu, wg, cw, cb, seq):
    m, d = x1.shape
    f = wu.shape[1]
    nseq = m // seq
    wu, wg = wu.astype(MXU), wg.astype(MXU)
    cb = cb.reshape(1, f)
    zeros = jnp.zeros((nseq, seq, f), F32)
    p1 = zeros.at[:, 0].set(conv_prev[:, 1]).reshape(m, f)
    p2 = zeros.at[:, 0].set(conv_prev[:, 0]).at[:, 1].set(conv_prev[:, 1]).reshape(m, f)
    full = lambda a: pl.BlockSpec(a.shape, lambda i: (0, 0))
    h, u = pl.pallas_call(
        functools.partial(_ffn_up_short_body, seq=seq),
        grid=(1,),
        in_specs=[full(x1), full(p1), full(p2), full(wu), full(wg), full(cw), full(cb)],
        out_specs=[pl.BlockSpec((m, f), lambda i: (0, 0)), pl.BlockSpec((m, f), lambda i: (0, 0))],
        out_shape=[jax.ShapeDtypeStruct((m, f), MXU), jax.ShapeDtypeStruct((m, f), F32)],
        compiler_params=_cparams(("arbitrary",)),
        name="ffn_up_short",
    )(x1, p1, p2, wu, wg, cw, cb)
    return h, u.reshape(nseq, seq, f)[:, seq - 2:]


def _ffn_down_body(h_ref, x1_ref, p_ref, wd, wpg, wp, g2, b2, o_ref, *, alpha):
    f = _dot(h_ref[...], wd[...])
    x2 = _layer_norm(alpha * x1_ref[...] + f, g2[...], b2[...])
    gate = jax.nn.sigmoid(_dot(x2.astype(MXU), wpg[...]))
    o_ref[...] = x2 + gate * _dot(p_ref[...].astype(MXU), wp[...])


def _ffn_down(h, x1, p2d, wd, wpg, wp, g2, b2, tm, alpha):
    m, d = x1.shape
    row = lambda n: pl.BlockSpec((tm, n), lambda i: (i, 0))
    full = lambda a: pl.BlockSpec(a.shape, lambda i: (0, 0))
    wd, wpg, wp = wd.astype(MXU), wpg.astype(MXU), wp.astype(MXU)
    g2, b2 = g2.reshape(1, d), b2.reshape(1, d)
    return pl.pallas_call(
        functools.partial(_ffn_down_body, alpha=alpha),
        grid=(m // tm,),
        in_specs=[row(h.shape[1]), row(d), row(p2d.shape[1]), full(wd), full(wpg), full(wp), full(g2), full(b2)],
        out_specs=row(d),
        out_shape=jax.ShapeDtypeStruct((m, d), F32),
        compiler_params=_cparams(("parallel",)),
        name="ffn_down_ln2_ple",
    )(h, x1, p2d, wd, wpg, wp, g2, b2)


def _overlap_t(n_sel, n_sel_pad, n_c, n_c_pad):
    start = np.arange(n_c_pad)[None, :] * CMP_STRIDE
    j = np.arange(n_sel_pad)[:, None]
    ov = (start < (j + 1) * SEL_BLOCK) & (start + CMP_BLOCK > j * SEL_BLOCK)
    ov &= (np.arange(n_c_pad)[None, :] < n_c) & (j < n_sel)
    return jnp.asarray(ov, dtype=MXU)


def _block_onehot(t_len, njp):
    e = (np.arange(t_len)[:, None] // SEL_BLOCK) == np.arange(njp)[None, :]
    return jnp.asarray(e, dtype=MXU)


def _prompt_bias_idx():
    nhg = KVH * GRP
    hg = np.repeat(np.arange(nhg), QB)[:, None]
    i = np.tile(np.arange(QB), nhg)[:, None]
    c = np.arange(2 * QB)[None, :]
    band = _bias_idx(i + QB - c, hg, np.ones((nhg * QB, 2 * QB), bool))
    band = np.where(band < 0, -2, band)
    mcol = np.arange(LANE)[None, :]
    dc = i + 113 - CMP_STRIDE * mcol
    bandc = _bias_idx(dc, hg, mcol < 16)
    bandc = np.where(bandc < 0, -2, bandc)
    return band, bandc


def _decode_bias_idx(p_len, s_new, n_c, ncp):
    rows = GRP * s_new * KVH
    r = np.arange(rows)
    g, i, h = r // (s_new * KVH), (r // KVH) % s_new, r % KVH
    hg = (h * GRP + g)[:, None]
    qpos = (p_len + i)[:, None]
    ones = np.ones((rows, 1), bool)
    n = np.arange(ncp)
    bcmp = _bias_idx(qpos - (CMP_STRIDE * n + CMP_BLOCK - 1)[None, :], hg, (n < n_c)[None, :] & ones)
    wb = min(WINDOW, p_len)
    wpos = np.concatenate([p_len - wb + np.arange(wb), p_len + np.arange(PAGE)])
    wok = np.concatenate([np.ones(wb, bool), np.arange(PAGE) < s_new])[None, :]
    dwin = qpos - wpos[None, :]
    bwin = _bias_idx(dwin, hg, wok & (dwin < WINDOW))
    blast = _bias_idx(qpos - (p_len - PAGE + np.arange(PAGE))[None, :], hg, np.ones((rows, PAGE), bool))
    bnew = _bias_idx(qpos - (p_len + np.arange(PAGE))[None, :], hg, (np.arange(PAGE) < s_new)[None, :] & ones)
    return bcmp, bwin, blast, bnew


def _pick(n, prefs):
    for p in prefs:
        if n % p == 0:
            return p
    return n


def _finish(x2d, o_nsa, o_fox, sga, sgb, p2d, conv_prev, seq, W, alpha):
    m = x2d.shape[0]
    tm = _pick(m, (256, 128))
    x1 = _merge(o_nsa, o_fox, sga, sgb, x2d, W['w_branch_a'], W['w_branch_b'], W['w_out'], W['ln1_g'], W['ln1_b'],
                tm, alpha)
    if seq >= 128:
        h, conv_state = _ffn_up_long(x1, conv_prev, W['w_ffn_up'], W['w_ffn_gate'], W['ffn_conv_w'], W['ffn_conv_b'],
                                     seq, _pick(seq, (256, 128)))
    else:
        h, conv_state = _ffn_up_short(x1, conv_prev, W['w_ffn_up'], W['w_ffn_gate'], W['ffn_conv_w'],
                                      W['ffn_conv_b'], seq)
    y = _ffn_down(h, x1, p2d, W['w_ffn_down'], W['w_ple_gate'], W['w_ple'], W['ln2_g'], W['ln2_b'], tm, alpha)
    return y, conv_state


def _prompt_layer(x, p_emb, W, rel_bias, alpha):
    b, t, d = x.shape
    m = b * t
    x2d = x.reshape(m, d)
    nq, nkv, kw, fq, fkv, sga, sgb, ng, logf = _proj(x2d, W['w_in'], W['b_forget'], _pick(m, (256, 128)))
    npg = t // PAGE
    gp = _pick(npg, (8, 4, 2, 1))
    ident = jnp.arange(b * npg, dtype=jnp.int32).reshape(b, npg)
    tile = _pick(t, (512, 256, 128))

    c, _ = _paged_cumsum(logf.reshape(b * npg, PAGE, FOX_HEADS), ident, gp)
    fq3, fkv3 = fq.reshape(b, t, FOX_W), fkv.reshape(b, t, 2 * FOX_W)
    qa, ka = _fox_prep(fq3, fkv3, c, tile)
    o_fox = _prompt_fox(qa, ka, fkv3, tile, tile)

    nkv3 = nkv.reshape(b, t, 4 * NSA_KV_W)
    cmp2 = _compress(nkv3.reshape(b * npg, PAGE, 4 * NSA_KV_W), ident, W['nsa_cmp_pe'], W['nsa_cmp_w1'],
                     W['nsa_cmp_w2'], gp)
    nc = cmp2.shape[1]
    n_c = (t - CMP_BLOCK) // CMP_STRIDE + 1
    nj = t // SEL_BLOCK
    njp = -(-nj // LANE) * LANE
    band_idx, bandc_idx = _prompt_bias_idx()
    o_nsa = _prompt_nsa(nq.reshape(b, t, NSA_Q_W), ng.reshape(b, t, LANE), cmp2, nkv3, kw.reshape(b, t, 2 * NSA_KV_W),
                        _block_onehot(t, njp), _overlap_t(nj, njp, n_c, nc), _bias_lookup(rel_bias, band_idx),
                        _bias_lookup(rel_bias, bandc_idx), tile)

    conv_prev = jnp.zeros((b, 2, W['w_ffn_up'].shape[1]), F32)
    y, conv_state = _finish(x2d, o_nsa.reshape(m, NSA_Q_W), o_fox.reshape(m, FOX_W), sga, sgb,
                            p_emb.reshape(m, -1), conv_prev, t, W, alpha)
    wb = min(WINDOW, t)
    state = (fkv.reshape(b, t, 2, FOX_HEADS, HEAD_DIM), logf.reshape(b, t, FOX_HEADS),
             nkv.reshape(b, t, 4, KVH, HEAD_DIM), kw.reshape(b, t, 2, KVH, HEAD_DIM)[:, t - wb:], conv_state)
    return y.reshape(b, t, d), state


def _sample_layer(x, p_emb, c_fox_kv, c_fox_logf, c_nsa, win, conv_prev, page_table, W, rel_bias, alpha):
    b, s, d = x.shape
    m = b * s
    npg = page_table.shape[1]
    p_len = npg * PAGE
    n_pool = c_nsa.shape[0]
    x2d = x.reshape(m, d)
    nq, nkv, kw, fq, fkv, sga, sgb, ng, logf = _proj(x2d, W['w_in'], W['b_forget'], _pick(m, (256, 128, 8)))
    gp = _pick(npg, (8, 4, 2, 1))
    pad_rows = lambda a: jnp.pad(a.reshape(b, s, -1), ((0, 0), (0, PAGE - s), (0, 0)))

    _, ct_all = _paged_cumsum(c_fox_logf.astype(F32), page_table, gp, new_page=pad_rows(logf))
    cq = jnp.transpose(ct_all[:, :, p_len:p_len + s], (0, 2, 1)).reshape(b, s * FOX_HEADS, 1)
    own = (np.arange(FOX_W)[None, :] // HEAD_DIM) == (np.arange(s * FOX_HEADS)[:, None] % FOX_HEADS)
    qrows_f = jnp.where(jnp.asarray(own)[None], jnp.repeat(fq.reshape(b, s, FOX_W), FOX_HEADS, axis=1), 0.0)
    o_fox = _decode_fox(c_fox_kv.reshape(n_pool, PAGE, 2 * FOX_W), page_table, qrows_f, cq, ct_all,
                        pad_rows(fkv), gp)

    cache2d = c_nsa.reshape(n_pool, PAGE, 4 * NSA_KV_W)
    cmp2 = _compress(cache2d, page_table, W['nsa_cmp_pe'], W['nsa_cmp_w1'], W['nsa_cmp_w2'], gp)
    ncp = cmp2.shape[1]
    n_c = (p_len + s - CMP_BLOCK) // CMP_STRIDE + 1
    nj = -(-(p_len + s) // SEL_BLOCK)
    njp = -(-nj // LANE) * LANE
    tables = [_bias_lookup(rel_bias, ix) for ix in _decode_bias_idx(p_len, s, n_c, ncp)]
    rows = GRP * s * KVH
    h_of_row = np.arange(rows) % KVH
    qsel = jnp.transpose(nq.reshape(b, s, KVH, GRP, HEAD_DIM), (0, 3, 1, 2, 4)).reshape(b, rows, HEAD_DIM)
    half = jnp.asarray((np.arange(LANE)[None, :] // HEAD_DIM) == h_of_row[:, None])
    qrows_n = jnp.where(half[None], jnp.concatenate([qsel, qsel], axis=-1), 0.0)
    gates = jnp.transpose(ng[:, FOX_HEADS:FOX_HEADS + KVH * GRP * 3].reshape(b, s, KVH, GRP, 3),
                          (0, 3, 1, 2, 4)).reshape(b, rows, 3)
    o_rows = _decode_nsa(cache2d, page_table, qrows_n, gates, cmp2, _overlap_t(nj, njp, n_c, ncp), *tables,
                         win.reshape(b, win.shape[1], 2 * NSA_KV_W), pad_rows(kw), pad_rows(nkv), gp, nj)
    o_nsa = jnp.transpose(o_rows.reshape(b, GRP, s, KVH, HEAD_DIM), (0, 2, 3, 1, 4)).reshape(m, NSA_Q_W)

    y, conv_state = _finish(x2d, o_nsa, o_fox.reshape(m, FOX_W), sga, sgb, p_emb.reshape(m, -1), conv_prev, s, W,
                            alpha)
    win_all = jnp.concatenate([win, kw.reshape(b, s, 2, KVH, HEAD_DIM)], axis=1)
    state = (fkv.reshape(b, s, 2, FOX_HEADS, HEAD_DIM), logf.reshape(b, s, FOX_HEADS),
             nkv.reshape(b, s, 4, KVH, HEAD_DIM), win_all[:, s:], conv_state)
    return y.reshape(b, s, d), state


def kernel(x_prompt, x_sample, p_prompt, p_sample, cache_fox_kv, cache_fox_logf, cache_nsa_kv, state_nsa_win,
           state_ffn_conv, page_table, w_in, b_forget, nsa_cmp_pe, nsa_cmp_w1, nsa_cmp_w2, rel_bias, w_branch_a,
           w_branch_b, w_out, ln1_g, ln1_b, ln2_g, ln2_b, w_ffn_up, w_ffn_gate, ffn_conv_w, ffn_conv_b, w_ffn_down,
           w_ple, w_ple_gate):
    depth = w_in.shape[0]
    alpha = (2.0 * depth) ** 0.25
    xp, xs = x_prompt, x_sample
    st_p, st_s = [], []
    for i in range(depth):
        W = {
            'w_in': w_in[i], 'b_forget': b_forget[i], 'nsa_cmp_pe': nsa_cmp_pe[i], 'nsa_cmp_w1': nsa_cmp_w1[i],
            'nsa_cmp_w2': nsa_cmp_w2[i], 'w_branch_a': w_branch_a[i], 'w_branch_b': w_branch_b[i], 'w_out': w_out[i],
            'ln1_g': ln1_g[i], 'ln1_b': ln1_b[i], 'ln2_g': ln2_g[i], 'ln2_b': ln2_b[i], 'w_ffn_up': w_ffn_up[i],
            'w_ffn_gate': w_ffn_gate[i], 'ffn_conv_w': ffn_conv_w[i], 'ffn_conv_b': ffn_conv_b[i],
            'w_ffn_down': w_ffn_down[i], 'w_ple': w_ple[i], 'w_ple_gate': w_ple_gate[i],
        }
        xp, sp = _prompt_layer(xp, p_prompt[i], W, rel_bias, alpha)
        xs, ss = _sample_layer(xs, p_sample[i], cache_fox_kv[i], cache_fox_logf[i], cache_nsa_kv[i],
                               state_nsa_win[i], state_ffn_conv[i], page_table, W, rel_bias, alpha)
        st_p.append(sp)
        st_s.append(ss)

    def stk(lst, j):
        return jnp.stack([s[j] for s in lst])

    return (xp, xs, stk(st_p, 0), stk(st_s, 0), stk(st_p, 1), stk(st_s, 1), stk(st_p, 2), stk(st_s, 2),
            stk(st_p, 3), stk(st_s, 3), stk(st_p, 4), stk(st_s, 4))
```

```python
import functools
import math

import numpy as np
import jax
import jax.numpy as jnp
from jax import lax
from jax.experimental import pallas as pl
from jax.experimental.pallas import tpu as pltpu

F32 = jnp.float32
BF16 = jnp.bfloat16
MXU = jnp.bfloat16

HEAD_DIM = 64
KVH = 2
GRP = 4
FOX_HEADS = 8
CMP_BLOCK = 32
CMP_STRIDE = 16
SEL_BLOCK = 64
N_SELECT = 16
WINDOW = 512
N_BUCKETS = 32
MAX_DISTANCE = 128
PAGE = 128
LN_EPS = 1e-5
SCALE = HEAD_DIM ** -0.5
LOG2E = math.log2(math.e)
QSCALE = SCALE * LOG2E
NSA_Q_W = KVH * GRP * HEAD_DIM
NSA_KV_W = KVH * HEAD_DIM
FOX_W = FOX_HEADS * HEAD_DIM
LANE = 128
VMEM_LIMIT = 56 * 1024 * 1024
NEG_INF = float("-inf")
MASK_BIG = 1e9


def _cparams(sem):
    return pltpu.CompilerParams(dimension_semantics=sem, vmem_limit_bytes=VMEM_LIMIT)


def _dot(a, b):
    return jnp.dot(a, b, preferred_element_type=F32)


def _dot_nt(a, b):
    return lax.dot_general(a, b, (((1,), (1,)), ((), ())), preferred_element_type=F32)


def _split3(x):
    hi = x.astype(BF16).astype(F32)
    r = x - hi
    mid = r.astype(BF16).astype(F32)
    lo = (r - mid).astype(BF16).astype(F32)
    return hi, mid, lo


def _iota(shape, dim):
    return lax.broadcasted_iota(jnp.int32, shape, dim)


def _softmax2_rows(lg):
    m = jnp.max(lg, axis=-1, keepdims=True)
    m = jnp.where(m == NEG_INF, 0.0, m)
    e = jnp.exp2(lg - m)
    d = jnp.maximum(jnp.sum(e, axis=-1, keepdims=True), 1e-30)
    return e * (1.0 / d)


def _online_update(lg, m_ref, l_ref):
    m_old = m_ref[...]
    m_new = jnp.maximum(m_old, jnp.max(lg, axis=-1, keepdims=True))
    m_safe = jnp.where(m_new == NEG_INF, 0.0, m_new)
    alpha = jnp.exp2(m_old - m_safe)
    p = jnp.exp2(lg - m_safe)
    l_ref[...] = alpha * l_ref[...] + jnp.sum(p, axis=-1, keepdims=True)
    m_ref[...] = m_new
    return alpha, p


def _pipelined(n_items, produce, consume, ahead):
    pending = {}
    for i in range(n_items + ahead):
        if i < n_items:
            pending[i] = produce(i)
        if i >= ahead:
            consume(i - ahead, pending.pop(i - ahead))


def _top_k_mask(x, k):
    n, w = x.shape
    bits = lax.bitcast_convert_type(x, jnp.int32)
    key = jnp.where(bits < 0, bits ^ jnp.int32(0x7FFFFFFF), bits)

    def count(mask):
        return jnp.sum(mask.astype(F32), axis=0, keepdims=True)

    kth = jnp.where(count(key >= 0) >= k, jnp.int32(0), jnp.int32(-2 ** 31))
    for bit in range(30, -1, -1):
        cand = kth | jnp.int32(1 << bit)
        kth = jnp.where(count(key >= cand) >= k, cand, kth)
    above = key > kth
    ties = key == kth
    need = k - count(above)
    lstrict = (_iota((n, n), 1) < _iota((n, n), 0)).astype(BF16)
    tie_rank = _dot(lstrict, ties.astype(BF16))
    return (above | (ties & (tie_rank < need))) & (x > NEG_INF)


def _layer_norm(x, g, b):
    mu = jnp.mean(x, axis=-1, keepdims=True)
    xc = x - mu
    var = jnp.mean(xc * xc, axis=-1, keepdims=True)
    return xc * lax.rsqrt(var + LN_EPS) * g + b


def _bucket_np(dist):
    n = np.maximum(np.asarray(dist), 0)
    max_exact = N_BUCKETS // 2
    nf = np.maximum(n, 1).astype(np.float32)
    large = max_exact + (np.log(nf / np.float32(max_exact)) / np.float32(math.log(MAX_DISTANCE / max_exact))
                         * np.float32(N_BUCKETS - max_exact)).astype(np.int32)
    large = np.minimum(large, N_BUCKETS - 1)
    return np.where(n < max_exact, n, large).astype(np.int32)


FAR_BUCKET = int(_bucket_np(np.array(MAX_DISTANCE)))


def _proj_body(x_ref, wnq, wnkv, wkw, wfq, wfkv, wga, wgb, wsm, bsm,
               nq_o, nkv_o, kw_o, fq_o, fkv_o, ga_o, gb_o, ng_o, logf_o):
    xb = x_ref[...].astype(MXU)
    nq_o[...] = _dot(xb, wnq[...])
    nkv_o[...] = _dot(xb, wnkv[...])
    kw_o[...] = _dot(xb, wkw[...])
    fq_o[...] = _dot(xb, wfq[...])
    fkv_o[...] = _dot(xb, wfkv[...])
    ga_o[...] = jax.nn.sigmoid(_dot(xb, wga[...]))
    gb_o[...] = jax.nn.sigmoid(_dot(xb, wgb[...]))
    sm = _dot(xb, wsm[...])
    ng_o[...] = jax.nn.sigmoid(sm)
    z = sm + bsm[...]
    ls = jnp.minimum(z, 0.0) - jnp.log1p(jnp.exp(-jnp.abs(z)))
    logf_o[...] = ls[:, :FOX_HEADS]


def _proj(x2d, w_in, b_forget, tm):
    m, d = x2d.shape
    o = np.cumsum([0, NSA_Q_W, 6 * NSA_KV_W, KVH * GRP * 3, FOX_W, FOX_W, FOX_W, FOX_HEADS, d, d])
    wb = w_in.astype(MXU)
    wnq = wb[:, o[0]:o[1]]
    wnkv = wb[:, o[1]:o[1] + 4 * NSA_KV_W]
    wkw = wb[:, o[1] + 4 * NSA_KV_W:o[2]]
    wfq = wb[:, o[3]:o[4]]
    wfkv = wb[:, o[4]:o[6]]
    wga = wb[:, o[7]:o[8]]
    wgb = wb[:, o[8]:o[9]]
    nsm = FOX_HEADS + KVH * GRP * 3
    wsm = jnp.concatenate([wb[:, o[6]:o[7]], wb[:, o[2]:o[3]], jnp.zeros((d, LANE - nsm), MXU)], axis=1)
    bsm = jnp.concatenate([b_forget.astype(F32), jnp.zeros((LANE - FOX_HEADS,), F32)]).reshape(1, LANE)
    ws = [wnq, wnkv, wkw, wfq, wfkv, wga, wgb, wsm]
    widths = [w.shape[1] for w in ws] + [FOX_HEADS]
    row = lambda n: pl.BlockSpec((tm, n), lambda i: (i, 0))
    full = lambda a: pl.BlockSpec(a.shape, lambda i: (0, 0))
    out_shapes = [jax.ShapeDtypeStruct((m, n), F32) for n in widths]
    return pl.pallas_call(
        _proj_body,
        grid=(m // tm,),
        in_specs=[row(d)] + [full(w) for w in ws] + [full(bsm)],
        out_specs=[row(n) for n in widths],
        out_shape=out_shapes,
        compiler_params=_cparams(("parallel",)),
        name="proj",
    )(x2d, *ws, bsm)


def _bias_body(tb_ref, idx_ref, o_ref):
    idx = idx_ref[...]
    nh = tb_ref.shape[1]

    def step(k, acc):
        hd = k % nh
        val = (tb_ref[k // nh, hd] - tb_ref[FAR_BUCKET, hd]) * LOG2E
        return jnp.where(idx == k, val, acc)

    acc = lax.fori_loop(0, tb_ref.shape[0] * nh, step, jnp.zeros(idx.shape, F32))
    o_ref[...] = jnp.where(idx == -1, NEG_INF, acc)


def _bias_lookup(rel_bias, idx_np):
    idx = jnp.asarray(idx_np, dtype=jnp.int32)
    return pl.pallas_call(
        _bias_body,
        in_specs=[pl.BlockSpec(memory_space=pltpu.SMEM), pl.BlockSpec(idx.shape, lambda: (0, 0))],
        out_specs=pl.BlockSpec(idx.shape, lambda: (0, 0)),
        out_shape=jax.ShapeDtypeStruct(idx.shape, F32),
        name="bias_tables",
    )(rel_bias.astype(F32), idx)


def _bias_idx(dist, head, ok):
    nh = KVH * GRP
    return np.where(ok & (dist >= 0), _bucket_np(dist) * nh + head, -1).astype(np.int32)


def _cumsum_body(pt_ref, *refs, gp, ns, has_new):
    pages = refs[:gp]
    rest = refs[gp:]
    new_ref = rest[0] if has_new else None
    c_o, ct_o, pad_s, carry_s = rest[1:] if has_new else rest
    st = pl.program_id(1)

    @pl.when(st == 0)
    def _():
        pad_s[...] = jnp.zeros_like(pad_s)
        carry_s[...] = jnp.zeros_like(carry_s)

    ltri = (_iota((PAGE, PAGE), 1) <= _iota((PAGE, PAGE), 0)).astype(BF16)

    def one_page(x, w):
        pad_s[:, 0:FOX_HEADS] = x
        xp = pad_s[...]
        cs = sum(_dot(ltri, t.astype(BF16)) for t in _split3(xp)) + carry_s[...]
        c_o[0, w * PAGE:(w + 1) * PAGE, :] = cs[:, 0:FOX_HEADS]
        ct_o[0, :, w * PAGE:(w + 1) * PAGE] = cs.T[0:FOX_HEADS, :]
        carry_s[...] = cs[PAGE - 1:PAGE, :]

    @pl.when(st < ns)
    def _():
        for w in range(gp):
            one_page(pages[w][0], w)

    if has_new:
        @pl.when(st == ns)
        def _():
            one_page(new_ref[0], 0)


def _paged_cumsum(pool, page_table, gp, new_page=None):
    b, npg = page_table.shape
    ns = npg // gp
    has_new = new_page is not None
    nsteps = ns + (1 if has_new else 0)
    stc = lambda st: jnp.minimum(st, ns - 1)
    in_specs = [pl.BlockSpec((1, PAGE, FOX_HEADS), functools.partial(
        lambda i, st, pt, w: (pt[i, stc(st) * gp + w], 0, 0), w=w)) for w in range(gp)]
    args = [pool] * gp
    if has_new:
        in_specs.append(pl.BlockSpec((1, PAGE, FOX_HEADS), lambda i, st, pt: (i, 0, 0)))
        args.append(new_page)
    width = gp * PAGE
    grid_spec = pltpu.PrefetchScalarGridSpec(
        num_scalar_prefetch=1,
        grid=(b, nsteps),
        in_specs=in_specs,
        out_specs=[pl.BlockSpec((1, width, FOX_HEADS), lambda i, st, pt: (i, st, 0)),
                   pl.BlockSpec((1, FOX_HEADS, width), lambda i, st, pt: (i, 0, st))],
        scratch_shapes=[pltpu.VMEM((PAGE, LANE), F32), pltpu.VMEM((1, LANE), F32)],
    )
    return pl.pallas_call(
        functools.partial(_cumsum_body, gp=gp, ns=ns, has_new=has_new),
        grid_spec=grid_spec,
        out_shape=[jax.ShapeDtypeStruct((b, nsteps * width, FOX_HEADS), F32),
                   jax.ShapeDtypeStruct((b, FOX_HEADS, nsteps * width), F32)],
        compiler_params=_cparams(("arbitrary", "arbitrary")),
        name="logf_cumsum",
    )(page_table, *args)


def _compress_body(pt_ref, *refs, gp, nrows):
    pages = refs[:gp]
    w1_ref, pe_ref, w2_ref, out_ref, buf_k, buf_v = refs[gp:]
    st = pl.program_id(1)
    ns = nrows // (gp * PAGE)
    nb = nrows // CMP_STRIDE
    bufs = (buf_k, buf_v)

    @pl.when(st == 0)
    def _():
        for buf in bufs:
            buf[nrows:nrows + CMP_BLOCK, :] = jnp.zeros((CMP_BLOCK, LANE), F32)

    for w in range(gp):
        base = pl.multiple_of((st * gp + w) * PAGE, PAGE)
        for typ, buf in enumerate(bufs):
            buf[pl.ds(base, PAGE), :] = pages[w][0, :, typ * LANE:(typ + 1) * LANE].astype(F32)

    @pl.when(st == ns - 1)
    def _():
        for typ, buf in enumerate(bufs):
            acc = jnp.zeros((nb, 2 * LANE), F32)
            bias = jnp.zeros((8, 2 * LANE), F32)
            for lp in range(CMP_BLOCK // 2):
                xa = buf[pl.ds(2 * lp, nb, stride=CMP_STRIDE), :]
                xb = buf[pl.ds(2 * lp + 1, nb, stride=CMP_STRIDE), :]
                x2 = jnp.concatenate([xa, xb], axis=1).astype(MXU)
                acc = acc + _dot(x2, w1_ref[typ, lp])
                pe2 = jnp.concatenate([pe_ref[typ, 2 * lp:2 * lp + 1, :], pe_ref[typ, 2 * lp + 1:2 * lp + 2, :]], axis=1)
                bias = bias + _dot(jnp.broadcast_to(pe2, (8, 2 * LANE)).astype(MXU), w1_ref[typ, lp])
            hid = jax.nn.gelu(acc + bias[0:1, :])
            out_ref[0, :, typ * LANE:(typ + 1) * LANE] = _dot(hid.astype(MXU), w2_ref[typ]).astype(out_ref.dtype)


def _compress(pool, page_table, pe, w1, w2, gp):
    b, npg = page_table.shape
    nrows = npg * PAGE
    nb = nrows // CMP_STRIDE
    z = jnp.zeros_like(w1)
    w1bd = jnp.concatenate([jnp.concatenate([w1, z], axis=-1), jnp.concatenate([z, w1], axis=-1)], axis=-2)
    w1pair = w1bd.reshape(2, CMP_BLOCK // 2, 4 * HEAD_DIM, w1bd.shape[-1]).astype(MXU)
    z2 = jnp.zeros_like(w2)
    w2bd = jnp.concatenate([jnp.concatenate([w2, z2], axis=-1), jnp.concatenate([z2, w2], axis=-1)], axis=-2).astype(MXU)
    pe2 = jnp.concatenate([pe, pe], axis=-1).astype(F32)
    page_spec = lambda w: pl.BlockSpec((1, PAGE, 2 * LANE), lambda i, st, pt: (pt[i, st * gp + w], 0, 0))
    grid_spec = pltpu.PrefetchScalarGridSpec(
        num_scalar_prefetch=1,
        grid=(b, npg // gp),
        in_specs=[page_spec(w) for w in range(gp)]
                 + [pl.BlockSpec(w1pair.shape, lambda i, st, pt: (0, 0, 0, 0)),
                    pl.BlockSpec(pe2.shape, lambda i, st, pt: (0, 0, 0)),
                    pl.BlockSpec(w2bd.shape, lambda i, st, pt: (0, 0, 0))],
        out_specs=pl.BlockSpec((1, nb, 2 * LANE), lambda i, st, pt: (i, 0, 0)),
        scratch_shapes=[pltpu.VMEM((nrows + CMP_BLOCK, LANE), F32), pltpu.VMEM((nrows + CMP_BLOCK, LANE), F32)],
    )
    return pl.pallas_call(
        functools.partial(_compress_body, gp=gp, nrows=nrows),
        grid_spec=grid_spec,
        out_shape=jax.ShapeDtypeStruct((b, nb, 2 * LANE), MXU),
        compiler_params=_cparams(("arbitrary", "arbitrary")),
        name="nsa_compress",
    )(page_table, *([pool] * gp), w1pair, pe2, w2bd)


def _fox_prep_body(q_ref, k_ref, v_ref, c_ref, qa_o, ka_o, va_o):
    tp = q_ref.shape[1]
    lane = _iota((tp, LANE), 1)
    q = q_ref[0]
    k = k_ref[0]
    v = v_ref[0]
    c = c_ref[0] * LOG2E
    one = jnp.ones((tp, LANE), F32)
    zero = jnp.zeros((tp, LANE), F32)
    for h in range(FOX_HEADS):
        qs = q[:, (h // 2) * LANE:(h // 2 + 1) * LANE]
        ks = k[:, (h // 2) * LANE:(h // 2 + 1) * LANE]
        vs = v[:, (h // 2) * LANE:(h // 2 + 1) * LANE]
        if h % 2 == 1:
            qs = pltpu.roll(qs, HEAD_DIM, 1)
            ks = pltpu.roll(ks, HEAD_DIM, 1)
            vs = pltpu.roll(vs, HEAD_DIM, 1)
        chi, cmid, clo = _split3(jnp.broadcast_to(c[:, h:h + 1], (tp, LANE)))
        qaug = jnp.where(lane == HEAD_DIM, chi, jnp.where(lane == HEAD_DIM + 1, cmid, jnp.where(
            lane == HEAD_DIM + 2, clo, jnp.where(lane < HEAD_DIM + 6, one, zero))))
        kaug = jnp.where(lane == HEAD_DIM + 3, -chi, jnp.where(lane == HEAD_DIM + 4, -cmid, jnp.where(
            lane == HEAD_DIM + 5, -clo, jnp.where(lane < HEAD_DIM + 3, one, zero))))
        qa_o[0, h] = jnp.where(lane < HEAD_DIM, qs * QSCALE, qaug).astype(qa_o.dtype)
        ka_o[0, h] = jnp.where(lane < HEAD_DIM, ks, kaug).astype(ka_o.dtype)
        va_o[0, h] = jnp.where(lane < HEAD_DIM, vs, jnp.where(lane == HEAD_DIM, one, zero)).astype(va_o.dtype)


def _fox_prep(fq, fkv, c, tp):
    b, t, _ = fq.shape
    hspec = pl.BlockSpec((1, FOX_HEADS, tp, LANE), lambda i, j: (i, 0, j, 0))
    hshape = jax.ShapeDtypeStruct((b, FOX_HEADS, t, LANE), MXU)
    return pl.pallas_call(
        _fox_prep_body,
        grid=(b, t // tp),
        in_specs=[pl.BlockSpec((1, tp, FOX_W), lambda i, j: (i, j, 0)),
                  pl.BlockSpec((1, tp, FOX_W), lambda i, j: (i, j, 0)),
                  pl.BlockSpec((1, tp, FOX_W), lambda i, j: (i, j, 1)),
                  pl.BlockSpec((1, tp, FOX_HEADS), lambda i, j: (i, j, 0))],
        out_specs=[hspec, hspec, hspec],
        out_shape=[hshape, hshape, hshape],
        compiler_params=_cparams(("parallel", "parallel")),
        name="fox_prep",
    )(fq, fkv, fkv, c)


RC = 128
KC = 512


def _flash_chunk(s, m_ref, acc_ref, r0, v):
    rep = s.shape[1] // LANE
    m_old = m_ref[r0:r0 + RC, :]
    m_new = jnp.maximum(m_old, jnp.max(s, axis=1, keepdims=True))
    alpha = jnp.exp2(m_old - m_new)
    p = jnp.exp2(s - jnp.concatenate([m_new] * rep, axis=1))
    acc = acc_ref[r0:r0 + RC, :]
    acc_ref[r0:r0 + RC, :] = jnp.concatenate([alpha] * (acc.shape[1] // LANE), axis=1) * acc + _dot(p.astype(MXU), v)
    m_ref[r0:r0 + RC, :] = m_new


def _pfox_body(q_ref, k_ref, v_ref, o_ref, m_s, acc_s, *, tq, tk, nk):
    qi = pl.program_id(2)
    kj = pl.program_id(3)
    last = ((qi + 1) * tq - 1) // tk

    @pl.when(kj == 0)
    def _():
        m_s[...] = jnp.full_like(m_s, -MASK_BIG)
        acc_s[...] = jnp.zeros_like(acc_s)

    kc = min(KC, tk)

    def update(diag):
        items = []
        for ks in range(tk // kc):
            for hh in range(2):
                for c in range(tq // RC):
                    r0 = c * RC
                    if diag and ks * kc > r0 + RC - 1:
                        continue
                    items.append((hh, r0, ks, diag and ks * kc + kc - 1 > r0))

        def logits(i):
            hh, r0, ks, masked = items[i]
            s = _dot_nt(q_ref[0, hh, r0:r0 + RC, :], k_ref[0, hh, ks * kc:(ks + 1) * kc, :])
            if masked:
                causal = (ks * kc + _iota((RC, kc), 1)) <= (r0 + _iota((RC, kc), 0))
                s = jnp.where(causal, s, -MASK_BIG)
            return s

        def finish(i, s):
            hh, r0, ks, _ = items[i]
            _flash_chunk(s, m_s.at[hh], acc_s.at[hh], r0, v_ref[0, hh, ks * kc:(ks + 1) * kc, :])

        _pipelined(len(items), logits, finish, 4)

    crosses = kj == qi

    @pl.when((kj <= last) & jnp.logical_not(crosses))
    def _():
        update(False)

    @pl.when((kj <= last) & crosses)
    def _():
        update(True)

    @pl.when(kj == nk - 1)
    def _():
        lane = _iota((tq, LANE), 1)
        outs = []
        for hh in range(2):
            acc = acc_s[hh]
            outs.append(acc * (1.0 / acc[:, HEAD_DIM:HEAD_DIM + 1]))
        o_ref[0] = jnp.where(lane < HEAD_DIM, outs[0], pltpu.roll(outs[1], HEAD_DIM, 1))


def _prompt_fox(qa, ka, va, tq, tk):
    assert tq == tk
    b, _, t, _ = qa.shape
    nq, nk = t // tq, t // tk
    hp = FOX_HEADS // 2
    lastf = lambda qi: ((qi + 1) * tq - 1) // tk
    kspec = pl.BlockSpec((1, 2, tk, LANE), lambda i, h, qi, kj: (i, h, jnp.minimum(kj, lastf(qi)), 0))
    return pl.pallas_call(
        functools.partial(_pfox_body, tq=tq, tk=tk, nk=nk),
        grid=(b, hp, nq, nk),
        in_specs=[pl.BlockSpec((1, 2, tq, LANE), lambda i, h, qi, kj: (i, h, qi, 0)), kspec, kspec],
        out_specs=pl.BlockSpec((1, tq, LANE), lambda i, h, qi, kj: (i, qi, h)),
        out_shape=jax.ShapeDtypeStruct((b, t, FOX_W), F32),
        scratch_shapes=[pltpu.VMEM((2, tq, LANE), F32), pltpu.VMEM((2, tq, LANE), F32)],
        compiler_params=_cparams(("parallel", "parallel", "parallel", "arbitrary")),
        name="prompt_fox",
    )(qa, ka, va)


QB = 128
NWIN = WINDOW // QB + 1


def _nsa_prep_body(k_ref, v_ref, e_ref, kcat_o, vcat_o):
    tp = k_ref.shape[1]
    lane = _iota((tp, LANE), 1)
    kcat_o[0] = jnp.concatenate([k_ref[0].astype(MXU), e_ref[...]], axis=1)
    ones_col = jnp.where(lane == 0, 1.0, 0.0).astype(MXU)
    vcat_o[0] = jnp.concatenate([v_ref[0].astype(MXU), ones_col], axis=1)


def _nsa_prep(nkv, e_mat, tp):
    b, t, _ = nkv.shape
    njp = e_mat.shape[1]
    return pl.pallas_call(
        _nsa_prep_body,
        grid=(b, t // tp),
        in_specs=[pl.BlockSpec((1, tp, LANE), lambda i, j: (i, j, 2)),
                  pl.BlockSpec((1, tp, LANE), lambda i, j: (i, j, 3)),
                  pl.BlockSpec((tp, njp), lambda i, j: (j, 0))],
        out_specs=[pl.BlockSpec((1, tp, LANE + njp), lambda i, j: (i, j, 0)),
                   pl.BlockSpec((1, tp, 2 * LANE), lambda i, j: (i, j, 0))],
        out_shape=[jax.ShapeDtypeStruct((b, t, LANE + njp), MXU), jax.ShapeDtypeStruct((b, t, 2 * LANE), MXU)],
        compiler_params=_cparams(("parallel", "parallel")),
        name="nsa_prep",
    )(nkv, nkv, e_mat)


def _pnsa_body(q_ref, g_ref, kc_ref, vc_ref, kcat_ref, vcat_ref, *refs, t_len, tk, nk):
    wk = refs[:NWIN]
    wv = refs[NWIN:2 * NWIN]
    ovt_ref, band_ref, bandc_ref, o_ref, qa_s, oc_s, ow_s, m_s, acc_s = refs[2 * NWIN:]
    qi = pl.program_id(1)
    kj = pl.program_id(2)
    t0 = qi * QB
    last = (t0 + QB - 1) // tk
    njp = ovt_ref.shape[0]
    nc = t_len // CMP_STRIDE
    n_c = (t_len - CMP_BLOCK) // CMP_STRIDE + 1
    nhg = KVH * GRP
    ahead = 4

    @pl.when(kj == 0)
    def _():
        q = q_ref[0]
        lane = _iota((QB, LANE), 1)
        for hg in range(nhg):
            h = hg // GRP
            sl = q[:, (hg // 2) * LANE:(hg // 2 + 1) * LANE]
            if hg % 2 != h:
                sl = pltpu.roll(sl, HEAD_DIM, 1)
            qa_s[hg * QB:(hg + 1) * QB, 0:LANE] = (jnp.where((lane // HEAD_DIM) == h, sl, 0.0) * QSCALE).astype(MXU)

        kcb = kc_ref[0]
        vcb = vc_ref[0]
        shift_t = ((_iota((nc, LANE), 0) == 8 * qi - 9 + _iota((nc, LANE), 1))
                   & (_iota((nc, LANE), 1) < 16)).astype(MXU)
        rhs = jnp.concatenate([kcb, shift_t, shift_t, shift_t], axis=1)
        trow = t0 + _iota((QB, nc), 0)
        ncol = _iota((QB, nc), 1)
        maskc = (CMP_STRIDE * ncol + (CMP_BLOCK - 1) <= trow) & (ncol < n_c)
        pcs = [None] * KVH

        def cmp_logits(hg):
            r0 = hg * QB
            lhs = jnp.concatenate([qa_s[r0:r0 + QB, 0:LANE]]
                                  + [t.astype(MXU) for t in _split3(bandc_ref[r0:r0 + QB, :])], axis=1)
            return _dot_nt(lhs, rhs)

        def cmp_finish(hg, s):
            r0 = hg * QB
            pc = _softmax2_rows(jnp.where(maskc, s, NEG_INF))
            oc_s[r0:r0 + QB, :] = _dot(pc.astype(MXU), vcb)
            h = hg // GRP
            pcs[h] = pc if pcs[h] is None else pcs[h] + pc

        _pipelined(nhg, cmp_logits, cmp_finish, ahead)
        pcsum = jnp.concatenate(pcs, axis=0)
        imp = _dot_nt(ovt_ref[...], pcsum.astype(MXU))
        width = KVH * QB
        jr = _iota((njp, width), 0)
        qblk = (t0 + _iota((njp, width), 1) % QB) // SEL_BLOCK
        forced = (jr == 0) | (jr == qblk) | (jr == qblk - 1)
        imp = jnp.where(forced, jnp.inf, jnp.where(jr > qblk, NEG_INF, imp))
        picked = _top_k_mask(imp, N_SELECT)
        pen_t = jnp.where(picked, 0.0, -MASK_BIG).astype(MXU)
        eye = (_iota((width, width), 0) == _iota((width, width), 1)).astype(MXU)
        pen = _dot_nt(eye, pen_t).astype(MXU)
        for hg in range(nhg):
            h = hg // GRP
            qa_s[hg * QB:(hg + 1) * QB, LANE:LANE + njp] = pen[h * QB:(h + 1) * QB]

        kwin = jnp.concatenate([r[0].astype(MXU) for r in wk], axis=0)
        vwin = jnp.concatenate([r[0].astype(MXU) for r in wv], axis=0)
        ri = _iota((QB, QB), 0)
        ci = _iota((QB, QB), 1)

        def win_logits(hg):
            return _dot_nt(qa_s[hg * QB:(hg + 1) * QB, 0:LANE], kwin)

        def win_finish(hg, s):
            r0 = hg * QB
            blocks = []
            for w in range(NWIN):
                sw = s[:, w * QB:(w + 1) * QB]
                ok = qi - (NWIN - 1) + w >= 0
                if w == NWIN - 1:
                    sw = jnp.where((ci <= ri) & ok, sw + band_ref[r0:r0 + QB, QB:2 * QB], NEG_INF)
                elif w == NWIN - 2:
                    sw = jnp.where(ok, sw + band_ref[r0:r0 + QB, 0:QB], NEG_INF)
                elif w == 0:
                    sw = jnp.where((ci > ri) & ok, sw, NEG_INF)
                else:
                    sw = jnp.where(ok, sw, NEG_INF)
                blocks.append(sw)
            pw = _softmax2_rows(jnp.concatenate(blocks, axis=1))
            ow_s[r0:r0 + QB, :] = _dot(pw.astype(MXU), vwin)

        _pipelined(nhg, win_logits, win_finish, ahead)

        m_s[...] = jnp.full_like(m_s, -MASK_BIG)
        acc_s[...] = jnp.zeros_like(acc_s)

    kc = min(KC, tk)
    nsub = tk // kc

    def step(near):
        def logits(i):
            ks, r0 = i // nhg, (i % nhg) * QB
            s = _dot_nt(qa_s[r0:r0 + QB, :], kcat_ref[0, ks * kc:(ks + 1) * kc, :])
            if near:
                trow = t0 + _iota((QB, kc), 0)
                scol = kj * tk + ks * kc + _iota((QB, kc), 1)
                chunks = []
                for c in range(kc // QB):
                    delta = qi - ((kj * tk + ks * kc) // QB + c)
                    chunks.append(jnp.where(delta == 0, band_ref[r0:r0 + QB, QB:2 * QB],
                                            jnp.where(delta == 1, band_ref[r0:r0 + QB, 0:QB], 0.0)))
                s = jnp.where(scol <= trow, s + jnp.concatenate(chunks, axis=1), -MASK_BIG)
            return s

        def finish(i, s):
            ks, r0 = i // nhg, (i % nhg) * QB
            _flash_chunk(s, m_s, acc_s, r0, vcat_ref[0, ks * kc:(ks + 1) * kc, :])

        _pipelined(nsub * nhg, logits, finish, ahead)

    near_from = (t0 - QB) // tk

    @pl.when(kj < near_from)
    def _():
        step(False)

    @pl.when((kj >= near_from) & (kj <= last))
    def _():
        step(True)

    @pl.when(kj == nk - 1)
    def _():
        gt = g_ref[0]
        lane = _iota((QB, LANE), 1)
        outs = []
        for hg in range(nhg):
            r0 = hg * QB
            acc = acc_s[r0:r0 + QB, :]
            osl = acc[:, 0:LANE] * (1.0 / acc[:, LANE:LANE + 1])
            off = FOX_HEADS + hg * 3
            outs.append(gt[:, off:off + 1] * oc_s[r0:r0 + QB, :] + gt[:, off + 1:off + 2] * osl
                        + gt[:, off + 2:off + 3] * ow_s[r0:r0 + QB, :])
        for pr in range(nhg // 2):
            h = (2 * pr) // GRP
            ev, od = outs[2 * pr], outs[2 * pr + 1]
            if h == 0:
                od = pltpu.roll(od, HEAD_DIM, 1)
            else:
                ev = pltpu.roll(ev, HEAD_DIM, 1)
            o_ref[0, :, pr * LANE:(pr + 1) * LANE] = jnp.where(lane < HEAD_DIM, ev, od)


def _prompt_nsa(nq, ng, cmp2, kcat, vcat, kw, ovt, band, bandc, tk):
    b, t, _ = nq.shape
    nqb, nk = t // QB, t // tk
    nc = cmp2.shape[1]
    njp = ovt.shape[0]
    rows = KVH * GRP * QB
    lastf = lambda qi: (qi * QB + QB - 1) // tk
    kjc = lambda qi, kj: jnp.minimum(kj, lastf(qi))
    wspec = lambda w, part: pl.BlockSpec((1, QB, LANE), lambda i, qi, kj: (i, jnp.maximum(qi - (NWIN - 1) + w, 0), part))
    const = lambda a: pl.BlockSpec(a.shape, lambda i, qi, kj: (0,) * a.ndim)
    return pl.pallas_call(
        functools.partial(_pnsa_body, t_len=t, tk=tk, nk=nk),
        grid=(b, nqb, nk),
        in_specs=[pl.BlockSpec((1, QB, NSA_Q_W), lambda i, qi, kj: (i, qi, 0)),
                  pl.BlockSpec((1, QB, LANE), lambda i, qi, kj: (i, qi, 0)),
                  pl.BlockSpec((1, nc, LANE), lambda i, qi, kj: (i, 0, 0)),
                  pl.BlockSpec((1, nc, LANE), lambda i, qi, kj: (i, 0, 1)),
                  pl.BlockSpec((1, tk, LANE + njp), lambda i, qi, kj: (i, kjc(qi, kj), 0)),
                  pl.BlockSpec((1, tk, 2 * LANE), lambda i, qi, kj: (i, kjc(qi, kj), 0))]
                 + [wspec(w, 0) for w in range(NWIN)] + [wspec(w, 1) for w in range(NWIN)]
                 + [const(ovt), const(band), const(bandc)],
        out_specs=pl.BlockSpec((1, QB, NSA_Q_W), lambda i, qi, kj: (i, qi, 0)),
        out_shape=jax.ShapeDtypeStruct((b, t, NSA_Q_W), F32),
        scratch_shapes=[pltpu.VMEM((rows, LANE + njp), MXU), pltpu.VMEM((rows, LANE), F32), pltpu.VMEM((rows, LANE), F32),
                        pltpu.VMEM((rows, LANE), F32), pltpu.VMEM((rows, 2 * LANE), F32)],
        compiler_params=_cparams(("parallel", "parallel", "arbitrary")),
        name="prompt_nsa",
    )(nq, ng, cmp2, cmp2, kcat, vcat, *([kw] * (2 * NWIN)), ovt, band, bandc)


def _dfox_body(pt_ref, *refs, gp, ns, s_new):
    pages = refs[:gp]
    q_ref, cq_ref, ck_ref, ckn_ref, new_ref, o_ref, m_s, l_s, acc_s = refs[gp:]
    st = pl.program_id(1)
    rows = FOX_HEADS * s_new
    qb = (q_ref[0] * QSCALE).astype(MXU)

    @pl.when(st == 0)
    def _():
        m_s[...] = jnp.full_like(m_s, NEG_INF)
        l_s[...] = jnp.zeros_like(l_s)
        acc_s[...] = jnp.zeros_like(acc_s)

    @pl.when(st < ns)
    def _():
        kv = [pg[0].astype(MXU) for pg in pages]
        s = jnp.concatenate([_dot_nt(qb, x[:, 0:FOX_W]) for x in kv], axis=1)
        ck = jnp.concatenate([ck_ref[0]] * s_new, axis=0)
        lg = s + (cq_ref[0] - ck) * LOG2E
        alpha, p = _online_update(lg, m_s, l_s)
        pv = jnp.zeros((rows, FOX_W), F32)
        for w in range(gp):
            pv = pv + _dot(p[:, w * PAGE:(w + 1) * PAGE].astype(MXU), kv[w][:, FOX_W:2 * FOX_W])
        acc_s[...] = alpha * acc_s[...] + pv

    @pl.when(st == ns)
    def _():
        x = new_ref[0].astype(MXU)
        s = _dot_nt(qb, x[:, 0:FOX_W])
        ck = jnp.concatenate([ckn_ref[0]] * s_new, axis=0)
        qrow = _iota((rows, PAGE), 0) // FOX_HEADS
        kcol = _iota((rows, PAGE), 1)
        lg = jnp.where(kcol <= qrow, s + (cq_ref[0] - ck) * LOG2E, NEG_INF)
        alpha, p = _online_update(lg, m_s, l_s)
        acc = alpha * acc_s[...] + _dot(p.astype(MXU), x[:, FOX_W:2 * FOX_W])
        o = acc * (1.0 / l_s[...])
        own = (_iota((rows, FOX_W), 1) // HEAD_DIM) == (_iota((rows, FOX_W), 0) % FOX_HEADS)
        o = jnp.where(own, o, 0.0)
        o_ref[0] = jnp.sum(o.reshape(s_new, FOX_HEADS, FOX_W), axis=1)


def _decode_fox(cache2d, page_table, qrows, cq, ct_all, new_pad, gp):
    b, npg = page_table.shape
    ns = npg // gp
    rows = qrows.shape[1]
    s_new = rows // FOX_HEADS
    stc = lambda st: jnp.minimum(st, ns - 1)
    page_spec = lambda w: pl.BlockSpec((1, PAGE, 2 * FOX_W), lambda i, st, pt: (pt[i, stc(st) * gp + w], 0, 0))
    grid_spec = pltpu.PrefetchScalarGridSpec(
        num_scalar_prefetch=1,
        grid=(b, ns + 1),
        in_specs=[page_spec(w) for w in range(gp)]
                 + [pl.BlockSpec((1, rows, FOX_W), lambda i, st, pt: (i, 0, 0)),
                    pl.BlockSpec((1, rows, 1), lambda i, st, pt: (i, 0, 0)),
                    pl.BlockSpec((1, FOX_HEADS, gp * PAGE), lambda i, st, pt: (i, 0, stc(st))),
                    pl.BlockSpec((1, FOX_HEADS, PAGE), lambda i, st, pt: (i, 0, npg)),
                    pl.BlockSpec((1, PAGE, 2 * FOX_W), lambda i, st, pt: (i, 0, 0))],
        out_specs=pl.BlockSpec((1, s_new, FOX_W), lambda i, st, pt: (i, 0, 0)),
        scratch_shapes=[pltpu.VMEM((rows, 1), F32), pltpu.VMEM((rows, 1), F32), pltpu.VMEM((rows, FOX_W), F32)],
    )
    return pl.pallas_call(
        functools.partial(_dfox_body, gp=gp, ns=ns, s_new=s_new),
        grid_spec=grid_spec,
        out_shape=jax.ShapeDtypeStruct((b, s_new, FOX_W), F32),
        compiler_params=_cparams(("parallel", "arbitrary")),
        name="decode_fox",
    )(page_table, *([cache2d] * gp), qrows, cq, ct_all, ct_all, new_pad)


def _dnsa_body(pt_ref, *refs, gp, ns, nj, njp):
    pages = refs[:gp]
    (q_ref, g_ref, kc_ref, vc_ref, ovt_ref, bcmp_ref, bwin_ref, blast_ref, bnew_ref,
     win_ref, wnew_ref, snew_ref, o_ref, sel_s, oc_s, ow_s, m_s, l_s, acc_s) = refs[gp:]
    st = pl.program_id(1)
    rows = q_ref.shape[1]
    rq = rows // GRP
    qb = (q_ref[0] * QSCALE).astype(MXU)

    @pl.when(st == 0)
    def _():
        kc = kc_ref[0].astype(MXU)
        vc = vc_ref[0].astype(MXU)
        pc = _softmax2_rows(_dot_nt(qb, kc) + bcmp_ref[...])
        oc_s[...] = _dot(pc.astype(MXU), vc)
        pcs = pc[0:rq]
        for g in range(1, GRP):
            pcs = pcs + pc[g * rq:(g + 1) * rq]
        pcs = jnp.concatenate([pcs, jnp.zeros((LANE - rq, pcs.shape[1]), F32)], axis=0)
        imp = _dot_nt(ovt_ref[...], pcs.astype(MXU))
        jr = _iota((njp, LANE), 0)
        qblk = ((nj - 1) * SEL_BLOCK + _iota((njp, LANE), 1) // KVH) // SEL_BLOCK
        forced = (jr == 0) | (jr == qblk) | (jr == qblk - 1)
        imp = jnp.where(forced, jnp.inf, jnp.where(jr > qblk, NEG_INF, imp))
        sel_t = _top_k_mask(imp, N_SELECT).astype(BF16)
        eye = (_iota((LANE, LANE), 0) == _iota((LANE, LANE), 1)).astype(BF16)
        sel = _dot_nt(eye, sel_t).astype(BF16)
        sel_s[...] = jnp.concatenate([sel[0:rq]] * GRP, axis=0)

        win = win_ref[0].astype(MXU)
        wnew = wnew_ref[0].astype(MXU)
        bw = bwin_ref[...]
        sw = jnp.concatenate([_dot_nt(qb, win[:, 0:LANE]) + bw[:, 0:WINDOW],
                              _dot_nt(qb, wnew[:, 0:LANE]) + bw[:, WINDOW:WINDOW + PAGE]], axis=1)
        pw = _softmax2_rows(sw)
        ow_s[...] = (_dot(pw[:, 0:WINDOW].astype(MXU), win[:, LANE:2 * LANE])
                     + _dot(pw[:, WINDOW:WINDOW + PAGE].astype(MXU), wnew[:, LANE:2 * LANE]))

        m_s[...] = jnp.full_like(m_s, NEG_INF)
        l_s[...] = jnp.zeros_like(l_s)
        acc_s[...] = jnp.zeros_like(acc_s)

    @pl.when(st < ns)
    def _():
        kv = [pg[0].astype(MXU) for pg in pages]
        s = jnp.concatenate([_dot_nt(qb, x[:, 0:LANE]) for x in kv], axis=1)
        width = gp * PAGE
        expand = (_iota((njp, width), 0) == (st * width + _iota((njp, width), 1)) // SEL_BLOCK).astype(BF16)
        mexp = _dot(sel_s[...], expand)
        tail = jnp.where(st == ns - 1, blast_ref[...], 0.0)
        if gp > 1:
            tail = jnp.concatenate([jnp.zeros((rows, width - PAGE), F32), tail], axis=1)
        lg = jnp.where(mexp > 0.5, s + tail, NEG_INF)
        alpha, p = _online_update(lg, m_s, l_s)
        pv = jnp.zeros((rows, LANE), F32)
        for w in range(gp):
            pv = pv + _dot(p[:, w * PAGE:(w + 1) * PAGE].astype(MXU), kv[w][:, LANE:2 * LANE])
        acc_s[...] = alpha * acc_s[...] + pv

    @pl.when(st == ns)
    def _():
        x = snew_ref[0].astype(MXU)
        s = _dot_nt(qb, x[:, 0:LANE]) + bnew_ref[...]
        picked = sel_s[:, nj - 1:nj].astype(F32) > 0.5
        lg = jnp.where(picked, s, NEG_INF)
        alpha, p = _online_update(lg, m_s, l_s)
        acc = alpha * acc_s[...] + _dot(p.astype(MXU), x[:, LANE:2 * LANE])
        osl = acc * (1.0 / jnp.maximum(l_s[...], 1e-30))
        gt = g_ref[0]
        of = gt[:, 0:1] * oc_s[...] + gt[:, 1:2] * osl + gt[:, 2:3] * ow_s[...]
        head = _iota((rows, LANE), 0) % KVH
        of = jnp.where(head == 0, of, pltpu.roll(of, HEAD_DIM, 1))
        o_ref[0] = of[:, 0:HEAD_DIM]


def _decode_nsa(cache2d, page_table, qrows, gates, cmp2, ovt, bcmp, bwin, blast, bnew, win2d, wnew_pad,
                snew_pad, gp, nj):
    b, npg = page_table.shape
    ns = npg // gp
    rows = qrows.shape[1]
    ncp = cmp2.shape[1]
    njp = ovt.shape[0]
    stc = lambda st: jnp.minimum(st, ns - 1)
    page_spec = lambda w: pl.BlockSpec((1, PAGE, 2 * LANE), lambda i, st, pt: (pt[i, stc(st) * gp + w], 0, 1))
    const2 = lambda a: pl.BlockSpec(a.shape, lambda i, st, pt: (0, 0))
    grid_spec = pltpu.PrefetchScalarGridSpec(
        num_scalar_prefetch=1,
        grid=(b, ns + 1),
        in_specs=[page_spec(w) for w in range(gp)]
                 + [pl.BlockSpec((1, rows, LANE), lambda i, st, pt: (i, 0, 0)),
                    pl.BlockSpec((1, rows, 3), lambda i, st, pt: (i, 0, 0)),
                    pl.BlockSpec((1, ncp, LANE), lambda i, st, pt: (i, 0, 0)),
                    pl.BlockSpec((1, ncp, LANE), lambda i, st, pt: (i, 0, 1)),
                    const2(ovt), const2(bcmp), const2(bwin), const2(blast), const2(bnew),
                    pl.BlockSpec((1, WINDOW, 2 * LANE), lambda i, st, pt: (i, 0, 0)),
                    pl.BlockSpec((1, PAGE, 2 * LANE), lambda i, st, pt: (i, 0, 0)),
                    pl.BlockSpec((1, PAGE, 2 * LANE), lambda i, st, pt: (i, 0, 1))],
        out_specs=pl.BlockSpec((1, rows, HEAD_DIM), lambda i, st, pt: (i, 0, 0)),
        scratch_shapes=[pltpu.VMEM((rows, njp), BF16), pltpu.VMEM((rows, LANE), F32), pltpu.VMEM((rows, LANE), F32),
                        pltpu.VMEM((rows, 1), F32), pltpu.VMEM((rows, 1), F32), pltpu.VMEM((rows, LANE), F32)],
    )
    return pl.pallas_call(
        functools.partial(_dnsa_body, gp=gp, ns=ns, nj=nj, njp=njp),
        grid_spec=grid_spec,
        out_shape=jax.ShapeDtypeStruct((b, rows, HEAD_DIM), F32),
        compiler_params=_cparams(("parallel", "arbitrary")),
        name="decode_nsa",
    )(page_table, *([cache2d] * gp), qrows, gates, cmp2, cmp2, ovt, bcmp, bwin, blast, bnew, win2d, wnew_pad,
      snew_pad)


def _merge_body(on_ref, of_ref, ga_ref, gb_ref, x_ref, wa, wb, wo, g1, b1, o_ref, *, alpha):
    a = _dot(on_ref[...].astype(MXU), wa[...])
    bb = _dot(of_ref[...].astype(MXU), wb[...])
    mix = _dot((ga_ref[...] * a + gb_ref[...] * bb).astype(MXU), wo[...])
    o_ref[...] = _layer_norm(alpha * x_ref[...] + mix, g1[...], b1[...])


def _merge(o_nsa, o_fox, sga, sgb, x2d, wa, wb, wo, g1, b1, tm, alpha):
    m, d = x2d.shape
    row = lambda n: pl.BlockSpec((tm, n), lambda i: (i, 0))
    full = lambda a: pl.BlockSpec(a.shape, lambda i: (0, 0))
    wa, wb, wo = wa.astype(MXU), wb.astype(MXU), wo.astype(MXU)
    g1, b1 = g1.reshape(1, d), b1.reshape(1, d)
    return pl.pallas_call(
        functools.partial(_merge_body, alpha=alpha),
        grid=(m // tm,),
        in_specs=[row(NSA_Q_W), row(FOX_W), row(d), row(d), row(d), full(wa), full(wb), full(wo), full(g1), full(b1)],
        out_specs=row(d),
        out_shape=jax.ShapeDtypeStruct((m, d), F32),
        compiler_params=_cparams(("parallel",)),
        name="merge_ln1",
    )(o_nsa, o_fox, sga, sgb, x2d, wa, wb, wo, g1, b1)


def _ffn_up_long_body(x_ref, prev_ref, wu, wg, cw, cb, h_ref, st_ref, carry_s, *, tiles_per_seq):
    i = pl.program_id(0)
    tm = x_ref.shape[0]
    xb = x_ref[...].astype(MXU)
    u = _dot(xb, wu[...])

    @pl.when(i % tiles_per_seq == 0)
    def _():
        carry_s[...] = prev_ref[0]

    r = _iota(u.shape, 0)
    um1 = jnp.where(r >= 1, pltpu.roll(u, 1, 0), carry_s[1:2, :])
    um2 = jnp.where(r >= 2, pltpu.roll(u, 2, 0), jnp.where(r == 0, carry_s[0:1, :], carry_s[1:2, :]))
    conv = cb[...] + cw[0:1, :] * um2 + cw[1:2, :] * um1 + cw[2:3, :] * u
    h_ref[...] = (jax.nn.gelu(conv) * _dot(xb, wg[...])).astype(h_ref.dtype)
    last2 = u[tm - 2:tm, :]
    carry_s[...] = last2
    st_ref[0] = last2


def _ffn_up_short_body(x_ref, p1_ref, p2_ref, wu, wg, cw, cb, h_ref, u_ref, *, seq):
    xb = x_ref[...].astype(MXU)
    u = _dot(xb, wu[...])
    t = _iota(u.shape, 0) % seq
    um1 = jnp.where(t >= 1, pltpu.roll(u, 1, 0), p1_ref[...])
    um2 = jnp.where(t >= 2, pltpu.roll(u, 2, 0), p2_ref[...])
    conv = cb[...] + cw[0:1, :] * um2 + cw[1:2, :] * um1 + cw[2:3, :] * u
    h_ref[...] = (jax.nn.gelu(conv) * _dot(xb, wg[...])).astype(h_ref.dtype)
    u_ref[...] = u


def _ffn_up_long(x1, conv_prev, wu, wg, cw, cb, seq, tm):
    m, d = x1.shape
    f = wu.shape[1]
    nseq = m // seq
    tps = seq // tm
    wu, wg = wu.astype(MXU), wg.astype(MXU)
    cb = cb.reshape(1, f)
    full = lambda a: pl.BlockSpec(a.shape, lambda i: (0, 0))
    return pl.pallas_call(
        functools.partial(_ffn_up_long_body, tiles_per_seq=tps),
        grid=(m // tm,),
        in_specs=[pl.BlockSpec((tm, d), lambda i: (i, 0)),
                  pl.BlockSpec((1, 2, f), lambda i: (i // tps, 0, 0)),
                  full(wu), full(wg), full(cw), full(cb)],
        out_specs=[pl.BlockSpec((tm, f), lambda i: (i, 0)),
                   pl.BlockSpec((1, 2, f), lambda i: (i // tps, 0, 0))],
        out_shape=[jax.ShapeDtypeStruct((m, f), MXU), jax.ShapeDtypeStruct((nseq, 2, f), F32)],
        scratch_shapes=[pltpu.VMEM((2, f), F32)],
        compiler_params=_cparams(("arbitrary",)),
        name="ffn_up_long",
    )(x1, conv_prev, wu, wg, cw, cb)


def _ffn_up_short(x1, conv_prev, wu, wg, cw, cb, seq):
    m, d = x1.shape
    f = wu.shape[1]
    nseq = m // seq
    wu, wg = wu.astype(MXU), wg.astype(MXU)
    cb = cb.reshape(1, f)
    zeros = jnp.zeros((nseq, seq, f), F32)
    p1 = zeros.at[:, 0].set(conv_prev[:, 1]).reshape(m, f)
    p2 = zeros.at[:, 0].set(conv_prev[:, 0]).at[:, 1].set(conv_prev[:, 1]).reshape(m, f)
    full = lambda a: pl.BlockSpec(a.shape, lambda i: (0, 0))
    h, u = pl.pallas_call(
        functools.partial(_ffn_up_short_body, seq=seq),
        grid=(1,),
        in_specs=[full(x1), full(p1), full(p2), full(wu), full(wg), full(cw), full(cb)],
        out_specs=[pl.BlockSpec((m, f), lambda i: (0, 0)), pl.BlockSpec((m, f), lambda i: (0, 0))],
        out_shape=[jax.ShapeDtypeStruct((m, f), MXU), jax.ShapeDtypeStruct((m, f), F32)],
        compiler_params=_cparams(("arbitrary",)),
        name="ffn_up_short",
    )(x1, p1, p2, wu, wg, cw, cb)
    return h, u.reshape(nseq, seq, f)[:, seq - 2:]


def _ffn_down_body(h_ref, x1_ref, p_ref, wd, wpg, wp, g2, b2, o_ref, *, alpha):
    f = _dot(h_ref[...], wd[...])
    x2 = _layer_norm(alpha * x1_ref[...] + f, g2[...], b2[...])
    gate = jax.nn.sigmoid(_dot(x2.astype(MXU), wpg[...]))
    o_ref[...] = x2 + gate * _dot(p_ref[...].astype(MXU), wp[...])


def _ffn_down(h, x1, p2d, wd, wpg, wp, g2, b2, tm, alpha):
    m, d = x1.shape
    row = lambda n: pl.BlockSpec((tm, n), lambda i: (i, 0))
    full = lambda a: pl.BlockSpec(a.shape, lambda i: (0, 0))
    wd, wpg, wp = wd.astype(MXU), wpg.astype(MXU), wp.astype(MXU)
    g2, b2 = g2.reshape(1, d), b2.reshape(1, d)
    return pl.pallas_call(
        functools.partial(_ffn_down_body, alpha=alpha),
        grid=(m // tm,),
        in_specs=[row(h.shape[1]), row(d), row(p2d.shape[1]), full(wd), full(wpg), full(wp), full(g2), full(b2)],
        out_specs=row(d),
        out_shape=jax.ShapeDtypeStruct((m, d), F32),
        compiler_params=_cparams(("parallel",)),
        name="ffn_down_ln2_ple",
    )(h, x1, p2d, wd, wpg, wp, g2, b2)


def _overlap_t(n_sel, n_sel_pad, n_c, n_c_pad):
    start = np.arange(n_c_pad)[None, :] * CMP_STRIDE
    j = np.arange(n_sel_pad)[:, None]
    ov = (start < (j + 1) * SEL_BLOCK) & (start + CMP_BLOCK > j * SEL_BLOCK)
    ov &= (np.arange(n_c_pad)[None, :] < n_c) & (j < n_sel)
    return jnp.asarray(ov, dtype=MXU)


def _block_onehot(t_len, njp):
    e = (np.arange(t_len)[:, None] // SEL_BLOCK) == np.arange(njp)[None, :]
    return jnp.asarray(e, dtype=MXU)


def _prompt_bias_idx():
    nhg = KVH * GRP
    hg = np.repeat(np.arange(nhg), QB)[:, None]
    i = np.tile(np.arange(QB), nhg)[:, None]
    c = np.arange(2 * QB)[None, :]
    band = _bias_idx(i + QB - c, hg, np.ones((nhg * QB, 2 * QB), bool))
    band = np.where(band < 0, -2, band)
    mcol = np.arange(LANE)[None, :]
    dc = i + 113 - CMP_STRIDE * mcol
    bandc = _bias_idx(dc, hg, mcol < 16)
    bandc = np.where(bandc < 0, -2, bandc)
    return band, bandc


def _decode_bias_idx(p_len, s_new, n_c, ncp):
    rows = GRP * s_new * KVH
    r = np.arange(rows)
    g, i, h = r // (s_new * KVH), (r // KVH) % s_new, r % KVH
    hg = (h * GRP + g)[:, None]
    qpos = (p_len + i)[:, None]
    ones = np.ones((rows, 1), bool)
    n = np.arange(ncp)
    bcmp = _bias_idx(qpos - (CMP_STRIDE * n + CMP_BLOCK - 1)[None, :], hg, (n < n_c)[None, :] & ones)
    wb = min(WINDOW, p_len)
    wpos = np.concatenate([p_len - wb + np.arange(wb), p_len + np.arange(PAGE)])
    wok = np.concatenate([np.ones(wb, bool), np.arange(PAGE) < s_new])[None, :]
    dwin = qpos - wpos[None, :]
    bwin = _bias_idx(dwin, hg, wok & (dwin < WINDOW))
    blast = _bias_idx(qpos - (p_len - PAGE + np.arange(PAGE))[None, :], hg, np.ones((rows, PAGE), bool))
    bnew = _bias_idx(qpos - (p_len + np.arange(PAGE))[None, :], hg, (np.arange(PAGE) < s_new)[None, :] & ones)
    return bcmp, bwin, blast, bnew


def _pick(n, prefs):
    for p in prefs:
        if n % p == 0:
            return p
    return n


def _finish(x2d, o_nsa, o_fox, sga, sgb, p2d, conv_prev, seq, W, alpha):
    m = x2d.shape[0]
    tm = _pick(m, (256, 128))
    x1 = _merge(o_nsa, o_fox, sga, sgb, x2d, W['w_branch_a'], W['w_branch_b'], W['w_out'], W['ln1_g'], W['ln1_b'],
                tm, alpha)
    if seq >= 128:
        h, conv_state = _ffn_up_long(x1, conv_prev, W['w_ffn_up'], W['w_ffn_gate'], W['ffn_conv_w'], W['ffn_conv_b'],
                                     seq, _pick(seq, (256, 128)))
    else:
        h, conv_state = _ffn_up_short(x1, conv_prev, W['w_ffn_up'], W['w_ffn_gate'], W['ffn_conv_w'],
                                      W['ffn_conv_b'], seq)
    y = _ffn_down(h, x1, p2d, W['w_ffn_down'], W['w_ple_gate'], W['w_ple'], W['ln2_g'], W['ln2_b'], tm, alpha)
    return y, conv_state


def _prompt_layer(x, p_emb, W, rel_bias, alpha):
    b, t, d = x.shape
    m = b * t
    x2d = x.reshape(m, d)
    nq, nkv, kw, fq, fkv, sga, sgb, ng, logf = _proj(x2d, W['w_in'], W['b_forget'], _pick(m, (256, 128)))
    npg = t // PAGE
    gp = _pick(npg, (8, 4, 2, 1))
    ident = jnp.arange(b * npg, dtype=jnp.int32).reshape(b, npg)
    tile = _pick(t, (1024, 512, 256, 128))

    c, _ = _paged_cumsum(logf.reshape(b * npg, PAGE, FOX_HEADS), ident, gp)
    fq3, fkv3 = fq.reshape(b, t, FOX_W), fkv.reshape(b, t, 2 * FOX_W)
    qa, ka, va = _fox_prep(fq3, fkv3, c, tile)
    o_fox = _prompt_fox(qa, ka, va, tile, tile)

    nkv3 = nkv.reshape(b, t, 4 * NSA_KV_W)
    cmp2 = _compress(nkv3.reshape(b * npg, PAGE, 4 * NSA_KV_W), ident, W['nsa_cmp_pe'], W['nsa_cmp_w1'],
                     W['nsa_cmp_w2'], gp)
    nc = cmp2.shape[1]
    n_c = (t - CMP_BLOCK) // CMP_STRIDE + 1
    nj = t // SEL_BLOCK
    njp = -(-nj // LANE) * LANE
    band_idx, bandc_idx = _prompt_bias_idx()
    kcat, vcat = _nsa_prep(nkv3, _block_onehot(t, njp), tile)
    o_nsa = _prompt_nsa(nq.reshape(b, t, NSA_Q_W), ng.reshape(b, t, LANE), cmp2, kcat, vcat,
                        kw.reshape(b, t, 2 * NSA_KV_W), _overlap_t(nj, njp, n_c, nc),
                        _bias_lookup(rel_bias, band_idx), _bias_lookup(rel_bias, bandc_idx), tile)

    conv_prev = jnp.zeros((b, 2, W['w_ffn_up'].shape[1]), F32)
    y, conv_state = _finish(x2d, o_nsa.reshape(m, NSA_Q_W), o_fox.reshape(m, FOX_W), sga, sgb,
                            p_emb.reshape(m, -1), conv_prev, t, W, alpha)
    wb = min(WINDOW, t)
    state = (fkv.reshape(b, t, 2, FOX_HEADS, HEAD_DIM), logf.reshape(b, t, FOX_HEADS),
             nkv.reshape(b, t, 4, KVH, HEAD_DIM), kw.reshape(b, t, 2, KVH, HEAD_DIM)[:, t - wb:], conv_state)
    return y.reshape(b, t, d), state


def _sample_layer(x, p_emb, c_fox_kv, c_fox_logf, c_nsa, win, conv_prev, page_table, W, rel_bias, alpha):
    b, s, d = x.shape
    m = b * s
    npg = page_table.shape[1]
    p_len = npg * PAGE
    n_pool = c_nsa.shape[0]
    x2d = x.reshape(m, d)
    nq, nkv, kw, fq, fkv, sga, sgb, ng, logf = _proj(x2d, W['w_in'], W['b_forget'], _pick(m, (256, 128, 8)))
    gp = _pick(npg, (8, 4, 2, 1))
    pad_rows = lambda a: jnp.pad(a.reshape(b, s, -1), ((0, 0), (0, PAGE - s), (0, 0)))

    _, ct_all = _paged_cumsum(c_fox_logf.astype(F32), page_table, gp, new_page=pad_rows(logf))
    cq = jnp.transpose(ct_all[:, :, p_len:p_len + s], (0, 2, 1)).reshape(b, s * FOX_HEADS, 1)
    own = (np.arange(FOX_W)[None, :] // HEAD_DIM) == (np.arange(s * FOX_HEADS)[:, None] % FOX_HEADS)
    qrows_f = jnp.where(jnp.asarray(own)[None], jnp.repeat(fq.reshape(b, s, FOX_W), FOX_HEADS, axis=1), 0.0)
    o_fox = _decode_fox(c_fox_kv.astype(MXU).reshape(n_pool, PAGE, 2 * FOX_W), page_table, qrows_f, cq, ct_all,
                        pad_rows(fkv), gp)

    cache2d = c_nsa.astype(MXU).reshape(n_pool, PAGE, 4 * NSA_KV_W)
    cmp2 = _compress(cache2d, page_table, W['nsa_cmp_pe'], W['nsa_cmp_w1'], W['nsa_cmp_w2'], gp)
    ncp = cmp2.shape[1]
    n_c = (p_len + s - CMP_BLOCK) // CMP_STRIDE + 1
    nj = -(-(p_len + s) // SEL_BLOCK)
    njp = -(-nj // LANE) * LANE
    tables = [_bias_lookup(rel_bias, ix) for ix in _decode_bias_idx(p_len, s, n_c, ncp)]
    rows = GRP * s * KVH
    h_of_row = np.arange(rows) % KVH
    qsel = jnp.transpose(nq.reshape(b, s, KVH, GRP, HEAD_DIM), (0, 3, 1, 2, 4)).reshape(b, rows, HEAD_DIM)
    half = jnp.asarray((np.arange(LANE)[None, :] // HEAD_DIM) == h_of_row[:, None])
    qrows_n = jnp.where(half[None], jnp.concatenate([qsel, qsel], axis=-1), 0.0)
    gates = jnp.transpose(ng[:, FOX_HEADS:FOX_HEADS + KVH * GRP * 3].reshape(b, s, KVH, GRP, 3),
                          (0, 3, 1, 2, 4)).reshape(b, rows, 3)
    o_rows = _decode_nsa(cache2d, page_table, qrows_n, gates, cmp2, _overlap_t(nj, njp, n_c, ncp), *tables,
                         win.reshape(b, win.shape[1], 2 * NSA_KV_W), pad_rows(kw), pad_rows(nkv), gp, nj)
    o_nsa = jnp.transpose(o_rows.reshape(b, GRP, s, KVH, HEAD_DIM), (0, 2, 3, 1, 4)).reshape(m, NSA_Q_W)

    y, conv_state = _finish(x2d, o_nsa, o_fox.reshape(m, FOX_W), sga, sgb, p_emb.reshape(m, -1), conv_prev, s, W,
                            alpha)
    win_all = jnp.concatenate([win, kw.reshape(b, s, 2, KVH, HEAD_DIM)], axis=1)
    state = (fkv.reshape(b, s, 2, FOX_HEADS, HEAD_DIM), logf.reshape(b, s, FOX_HEADS),
             nkv.reshape(b, s, 4, KVH, HEAD_DIM), win_all[:, s:], conv_state)
    return y.reshape(b, s, d), state


def kernel(x_prompt, x_sample, p_prompt, p_sample, cache_fox_kv, cache_fox_logf, cache_nsa_kv, state_nsa_win,
           state_ffn_conv, page_table, w_in, b_forget, nsa_cmp_pe, nsa_cmp_w1, nsa_cmp_w2, rel_bias, w_branch_a,
           w_branch_b, w_out, ln1_g, ln1_b, ln2_g, ln2_b, w_ffn_up, w_ffn_gate, ffn_conv_w, ffn_conv_b, w_ffn_down,
           w_ple, w_ple_gate):
    depth = w_in.shape[0]
    alpha = (2.0 * depth) ** 0.25
    xp, xs = x_prompt, x_sample
    st_p, st_s = [], []
    for i in range(depth):
        W = {
            'w_in': w_in[i], 'b_forget': b_forget[i], 'nsa_cmp_pe': nsa_cmp_pe[i], 'nsa_cmp_w1': nsa_cmp_w1[i],
            'nsa_cmp_w2': nsa_cmp_w2[i], 'w_branch_a': w_branch_a[i], 'w_branch_b': w_branch_b[i], 'w_out': w_out[i],
            'ln1_g': ln1_g[i], 'ln1_b': ln1_b[i], 'ln2_g': ln2_g[i], 'ln2_b': ln2_b[i], 'w_ffn_up': w_ffn_up[i],
            'w_ffn_gate': w_ffn_gate[i], 'ffn_conv_w': ffn_conv_w[i], 'ffn_conv_b': ffn_conv_b[i],
            'w_ffn_down': w_ffn_down[i], 'w_ple': w_ple[i], 'w_ple_gate': w_ple_gate[i],
        }
        xp, sp = _prompt_layer(xp, p_prompt[i], W, rel_bias, alpha)
        xs, ss = _sample_layer(xs, p_sample[i], cache_fox_kv[i], cache_fox_logf[i], cache_nsa_kv[i],
                               state_nsa_win[i], state_ffn_conv[i], page_table, W, rel_bias, alpha)
        st_p.append(sp)
        st_s.append(ss)

    def stk(lst, j):
        return jnp.stack([s[j] for s in lst])

    return (xp, xs, stk(st_p, 0), stk(st_s, 0), stk(st_p, 1), stk(st_s, 1), stk(st_p, 2), stk(st_s, 2),
            stk(st_p, 3), stk(st_s, 3), stk(st_p, 4), stk(st_s, 4))
```

```python
import functools
import math

import numpy as np
import jax
import jax.numpy as jnp
from jax import lax
from jax.experimental import pallas as pl
from jax.experimental.pallas import tpu as pltpu

F32 = jnp.float32
BF16 = jnp.bfloat16
MXU = jnp.bfloat16

HEAD_DIM = 64
KVH = 2
GRP = 4
FOX_HEADS = 8
CMP_BLOCK = 32
CMP_STRIDE = 16
SEL_BLOCK = 64
N_SELECT = 16
WINDOW = 512
N_BUCKETS = 32
MAX_DISTANCE = 128
PAGE = 128
LN_EPS = 1e-5
SCALE = HEAD_DIM ** -0.5
LOG2E = math.log2(math.e)
QSCALE = SCALE * LOG2E
NSA_Q_W = KVH * GRP * HEAD_DIM
NSA_KV_W = KVH * HEAD_DIM
FOX_W = FOX_HEADS * HEAD_DIM
LANE = 128
VMEM_LIMIT = 56 * 1024 * 1024
NEG_INF = float("-inf")
MASK_BIG = 1e9


def _cparams(sem):
    return pltpu.CompilerParams(dimension_semantics=sem, vmem_limit_bytes=VMEM_LIMIT)


def _dot(a, b):
    return jnp.dot(a, b, preferred_element_type=F32)


def _dot_nt(a, b):
    return lax.dot_general(a, b, (((1,), (1,)), ((), ())), preferred_element_type=F32)


def _split3(x):
    hi = x.astype(BF16).astype(F32)
    r = x - hi
    mid = r.astype(BF16).astype(F32)
    lo = (r - mid).astype(BF16).astype(F32)
    return hi, mid, lo


def _iota(shape, dim):
    return lax.broadcasted_iota(jnp.int32, shape, dim)


def _softmax2_rows(lg):
    m = jnp.max(lg, axis=-1, keepdims=True)
    m = jnp.where(m == NEG_INF, 0.0, m)
    e = jnp.exp2(lg - m)
    d = jnp.maximum(jnp.sum(e, axis=-1, keepdims=True), 1e-30)
    return e * (1.0 / d)


def _online_update(lg, m_ref, l_ref):
    m_old = m_ref[...]
    m_new = jnp.maximum(m_old, jnp.max(lg, axis=-1, keepdims=True))
    m_safe = jnp.where(m_new == NEG_INF, 0.0, m_new)
    alpha = jnp.exp2(m_old - m_safe)
    p = jnp.exp2(lg - m_safe)
    l_ref[...] = alpha * l_ref[...] + jnp.sum(p, axis=-1, keepdims=True)
    m_ref[...] = m_new
    return alpha, p


def _pipelined(n_items, produce, consume, ahead):
    pending = {}
    for i in range(n_items + ahead):
        if i < n_items:
            pending[i] = produce(i)
        if i >= ahead:
            consume(i - ahead, pending.pop(i - ahead))


def _top_k_mask(x, k):
    n, w = x.shape
    row = _iota((n, w), 0).astype(F32)
    picked = jnp.zeros((n, w), jnp.bool_)
    work = x
    for _ in range(k):
        top = jnp.max(work, axis=0, keepdims=True)
        first = jnp.min(jnp.where(work == top, row, float(n)), axis=0, keepdims=True)
        hit = row == first
        picked = picked | (hit & (top > NEG_INF))
        work = jnp.where(hit, NEG_INF, work)
    return picked


def _layer_norm(x, g, b):
    mu = jnp.mean(x, axis=-1, keepdims=True)
    xc = x - mu
    var = jnp.mean(xc * xc, axis=-1, keepdims=True)
    return xc * lax.rsqrt(var + LN_EPS) * g + b


def _bucket_np(dist):
    n = np.maximum(np.asarray(dist), 0)
    max_exact = N_BUCKETS // 2
    nf = np.maximum(n, 1).astype(np.float32)
    large = max_exact + (np.log(nf / np.float32(max_exact)) / np.float32(math.log(MAX_DISTANCE / max_exact))
                         * np.float32(N_BUCKETS - max_exact)).astype(np.int32)
    large = np.minimum(large, N_BUCKETS - 1)
    return np.where(n < max_exact, n, large).astype(np.int32)


FAR_BUCKET = int(_bucket_np(np.array(MAX_DISTANCE)))


def _proj_body(x_ref, wnq, wnkv, wkw, wfq, wfkv, wga, wgb, wsm, bsm,
               nq_o, nkv_o, kw_o, fq_o, fkv_o, ga_o, gb_o, ng_o, logf_o):
    xb = x_ref[...].astype(MXU)
    nq_o[...] = _dot(xb, wnq[...])
    nkv_o[...] = _dot(xb, wnkv[...])
    kw_o[...] = _dot(xb, wkw[...])
    fq_o[...] = _dot(xb, wfq[...])
    fkv_o[...] = _dot(xb, wfkv[...])
    ga_o[...] = jax.nn.sigmoid(_dot(xb, wga[...]))
    gb_o[...] = jax.nn.sigmoid(_dot(xb, wgb[...]))
    sm = _dot(xb, wsm[...])
    ng_o[...] = jax.nn.sigmoid(sm)
    z = sm + bsm[...]
    ls = jnp.minimum(z, 0.0) - jnp.log1p(jnp.exp(-jnp.abs(z)))
    logf_o[...] = ls[:, :FOX_HEADS]


def _proj(x2d, w_in, b_forget, tm):
    m, d = x2d.shape
    o = np.cumsum([0, NSA_Q_W, 6 * NSA_KV_W, KVH * GRP * 3, FOX_W, FOX_W, FOX_W, FOX_HEADS, d, d])
    wb = w_in.astype(MXU)
    wnq = wb[:, o[0]:o[1]]
    wnkv = wb[:, o[1]:o[1] + 4 * NSA_KV_W]
    wkw = wb[:, o[1] + 4 * NSA_KV_W:o[2]]
    wfq = wb[:, o[3]:o[4]]
    wfkv = wb[:, o[4]:o[6]]
    wga = wb[:, o[7]:o[8]]
    wgb = wb[:, o[8]:o[9]]
    nsm = FOX_HEADS + KVH * GRP * 3
    wsm = jnp.concatenate([wb[:, o[6]:o[7]], wb[:, o[2]:o[3]], jnp.zeros((d, LANE - nsm), MXU)], axis=1)
    bsm = jnp.concatenate([b_forget.astype(F32), jnp.zeros((LANE - FOX_HEADS,), F32)]).reshape(1, LANE)
    ws = [wnq, wnkv, wkw, wfq, wfkv, wga, wgb, wsm]
    widths = [w.shape[1] for w in ws] + [FOX_HEADS]
    row = lambda n: pl.BlockSpec((tm, n), lambda i: (i, 0))
    full = lambda a: pl.BlockSpec(a.shape, lambda i: (0, 0))
    out_shapes = [jax.ShapeDtypeStruct((m, n), F32) for n in widths]
    return pl.pallas_call(
        _proj_body,
        grid=(m // tm,),
        in_specs=[row(d)] + [full(w) for w in ws] + [full(bsm)],
        out_specs=[row(n) for n in widths],
        out_shape=out_shapes,
        compiler_params=_cparams(("parallel",)),
        name="proj",
    )(x2d, *ws, bsm)


def _bias_body(tb_ref, idx_ref, o_ref):
    idx = idx_ref[...]
    nh = tb_ref.shape[1]

    def step(k, acc):
        hd = k % nh
        val = (tb_ref[k // nh, hd] - tb_ref[FAR_BUCKET, hd]) * LOG2E
        return jnp.where(idx == k, val, acc)

    acc = lax.fori_loop(0, tb_ref.shape[0] * nh, step, jnp.zeros(idx.shape, F32))
    o_ref[...] = jnp.where(idx == -1, NEG_INF, acc)


def _bias_lookup(rel_bias, idx_np):
    idx = jnp.asarray(idx_np, dtype=jnp.int32)
    return pl.pallas_call(
        _bias_body,
        in_specs=[pl.BlockSpec(memory_space=pltpu.SMEM), pl.BlockSpec(idx.shape, lambda: (0, 0))],
        out_specs=pl.BlockSpec(idx.shape, lambda: (0, 0)),
        out_shape=jax.ShapeDtypeStruct(idx.shape, F32),
        name="bias_tables",
    )(rel_bias.astype(F32), idx)


def _bias_idx(dist, head, ok):
    nh = KVH * GRP
    return np.where(ok & (dist >= 0), _bucket_np(dist) * nh + head, -1).astype(np.int32)


def _cumsum_body(pt_ref, *refs, gp, ns, has_new):
    pages = refs[:gp]
    rest = refs[gp:]
    new_ref = rest[0] if has_new else None
    c_o, ct_o, pad_s, carry_s = rest[1:] if has_new else rest
    st = pl.program_id(1)

    @pl.when(st == 0)
    def _():
        pad_s[...] = jnp.zeros_like(pad_s)
        carry_s[...] = jnp.zeros_like(carry_s)

    ltri = (_iota((PAGE, PAGE), 1) <= _iota((PAGE, PAGE), 0)).astype(BF16)

    def one_page(x, w):
        pad_s[:, 0:FOX_HEADS] = x
        xp = pad_s[...]
        cs = sum(_dot(ltri, t.astype(BF16)) for t in _split3(xp)) + carry_s[...]
        c_o[0, w * PAGE:(w + 1) * PAGE, :] = cs[:, 0:FOX_HEADS]
        ct_o[0, :, w * PAGE:(w + 1) * PAGE] = cs.T[0:FOX_HEADS, :]
        carry_s[...] = cs[PAGE - 1:PAGE, :]

    @pl.when(st < ns)
    def _():
        for w in range(gp):
            one_page(pages[w][0], w)

    if has_new:
        @pl.when(st == ns)
        def _():
            one_page(new_ref[0], 0)


def _paged_cumsum(pool, page_table, gp, new_page=None):
    b, npg = page_table.shape
    ns = npg // gp
    has_new = new_page is not None
    nsteps = ns + (1 if has_new else 0)
    stc = lambda st: jnp.minimum(st, ns - 1)
    in_specs = [pl.BlockSpec((1, PAGE, FOX_HEADS), functools.partial(
        lambda i, st, pt, w: (pt[i, stc(st) * gp + w], 0, 0), w=w)) for w in range(gp)]
    args = [pool] * gp
    if has_new:
        in_specs.append(pl.BlockSpec((1, PAGE, FOX_HEADS), lambda i, st, pt: (i, 0, 0)))
        args.append(new_page)
    width = gp * PAGE
    grid_spec = pltpu.PrefetchScalarGridSpec(
        num_scalar_prefetch=1,
        grid=(b, nsteps),
        in_specs=in_specs,
        out_specs=[pl.BlockSpec((1, width, FOX_HEADS), lambda i, st, pt: (i, st, 0)),
                   pl.BlockSpec((1, FOX_HEADS, width), lambda i, st, pt: (i, 0, st))],
        scratch_shapes=[pltpu.VMEM((PAGE, LANE), F32), pltpu.VMEM((1, LANE), F32)],
    )
    return pl.pallas_call(
        functools.partial(_cumsum_body, gp=gp, ns=ns, has_new=has_new),
        grid_spec=grid_spec,
        out_shape=[jax.ShapeDtypeStruct((b, nsteps * width, FOX_HEADS), F32),
                   jax.ShapeDtypeStruct((b, FOX_HEADS, nsteps * width), F32)],
        compiler_params=_cparams(("arbitrary", "arbitrary")),
        name="logf_cumsum",
    )(page_table, *args)


def _cumsum_t_body(pt_ref, *refs, gp, ns):
    pages = refs[:gp]
    new_ref, ct_o, carry_s = refs[gp:]
    st = pl.program_id(1)

    @pl.when(st == 0)
    def _():
        carry_s[...] = jnp.zeros_like(carry_s)

    utri = (_iota((PAGE, PAGE), 0) <= _iota((PAGE, PAGE), 1)).astype(BF16)

    def one_page(xt, w):
        x16 = jnp.concatenate([xt, jnp.zeros_like(xt)], axis=0)
        cs = sum(_dot(t.astype(BF16), utri) for t in _split3(x16)) + carry_s[...]
        ct_o[0, :, w * PAGE:(w + 1) * PAGE] = cs[0:FOX_HEADS, :]
        carry_s[...] = jnp.broadcast_to(cs[:, PAGE - 1:PAGE], carry_s.shape)

    @pl.when(st < ns)
    def _():
        for w in range(gp):
            one_page(pages[w][0], w)

    @pl.when(st == ns)
    def _():
        one_page(new_ref[0], 0)


def _paged_cumsum_t(pool_t, page_table, gp, new_page_t):
    b, npg = page_table.shape
    ns = npg // gp
    stc = lambda st: jnp.minimum(st, ns - 1)
    page_spec = lambda w: pl.BlockSpec((1, FOX_HEADS, PAGE), lambda i, st, pt: (pt[i, stc(st) * gp + w], 0, 0))
    width = gp * PAGE
    grid_spec = pltpu.PrefetchScalarGridSpec(
        num_scalar_prefetch=1,
        grid=(b, ns + 1),
        in_specs=[page_spec(w) for w in range(gp)] + [pl.BlockSpec((1, FOX_HEADS, PAGE), lambda i, st, pt: (i, 0, 0))],
        out_specs=pl.BlockSpec((1, FOX_HEADS, width), lambda i, st, pt: (i, 0, st)),
        scratch_shapes=[pltpu.VMEM((2 * FOX_HEADS, LANE), F32)],
    )
    return pl.pallas_call(
        functools.partial(_cumsum_t_body, gp=gp, ns=ns),
        grid_spec=grid_spec,
        out_shape=jax.ShapeDtypeStruct((b, FOX_HEADS, (ns + 1) * width), F32),
        compiler_params=_cparams(("arbitrary", "arbitrary")),
        name="logf_cumsum_t",
    )(page_table, *([pool_t] * gp), new_page_t)


def _compress_body(pt_ref, *refs, gp, nrows, transposed):
    pages = refs[:gp]
    w1_ref, pe_ref, w2_ref, out_ref, buf_k, buf_v = refs[gp:]
    st = pl.program_id(1)
    ns = nrows // (gp * PAGE)
    nb = nrows // CMP_STRIDE
    bufs = (buf_k, buf_v)

    @pl.when(st == 0)
    def _():
        for buf in bufs:
            buf[nrows:nrows + CMP_BLOCK, :] = jnp.zeros((CMP_BLOCK, LANE), F32)

    eye = (_iota((PAGE, PAGE), 0) == _iota((PAGE, PAGE), 1)).astype(MXU)
    for w in range(gp):
        base = pl.multiple_of((st * gp + w) * PAGE, PAGE)
        for typ, buf in enumerate(bufs):
            if transposed:
                xt = pages[w][0, typ].reshape(LANE, PAGE).astype(MXU)
                buf[pl.ds(base, PAGE), :] = _dot_nt(eye, xt)
            else:
                buf[pl.ds(base, PAGE), :] = pages[w][0, :, typ * LANE:(typ + 1) * LANE].astype(F32)

    @pl.when(st == ns - 1)
    def _():
        for typ, buf in enumerate(bufs):
            acc = jnp.zeros((nb, 2 * LANE), F32)
            bias = jnp.zeros((8, 2 * LANE), F32)
            for lp in range(CMP_BLOCK // 2):
                xa = buf[pl.ds(2 * lp, nb, stride=CMP_STRIDE), :]
                xb = buf[pl.ds(2 * lp + 1, nb, stride=CMP_STRIDE), :]
                x2 = jnp.concatenate([xa, xb], axis=1).astype(MXU)
                acc = acc + _dot(x2, w1_ref[typ, lp])
                pe2 = jnp.concatenate([pe_ref[typ, 2 * lp:2 * lp + 1, :], pe_ref[typ, 2 * lp + 1:2 * lp + 2, :]], axis=1)
                bias = bias + _dot(jnp.broadcast_to(pe2, (8, 2 * LANE)).astype(MXU), w1_ref[typ, lp])
            hid = jax.nn.gelu(acc + bias[0:1, :])
            out_ref[0, :, typ * LANE:(typ + 1) * LANE] = _dot(hid.astype(MXU), w2_ref[typ]).astype(out_ref.dtype)


def _compress(pool, page_table, pe, w1, w2, gp, transposed=False):
    b, npg = page_table.shape
    nrows = npg * PAGE
    nb = nrows // CMP_STRIDE
    z = jnp.zeros_like(w1)
    w1bd = jnp.concatenate([jnp.concatenate([w1, z], axis=-1), jnp.concatenate([z, w1], axis=-1)], axis=-2)
    w1pair = w1bd.reshape(2, CMP_BLOCK // 2, 4 * HEAD_DIM, w1bd.shape[-1]).astype(MXU)
    z2 = jnp.zeros_like(w2)
    w2bd = jnp.concatenate([jnp.concatenate([w2, z2], axis=-1), jnp.concatenate([z2, w2], axis=-1)], axis=-2).astype(MXU)
    pe2 = jnp.concatenate([pe, pe], axis=-1).astype(F32)
    if transposed:
        page_spec = lambda w: pl.BlockSpec((1, 2, KVH, HEAD_DIM, PAGE), lambda i, st, pt: (pt[i, st * gp + w], 0, 0, 0, 0))
    else:
        page_spec = lambda w: pl.BlockSpec((1, PAGE, 2 * LANE), lambda i, st, pt: (pt[i, st * gp + w], 0, 0))
    grid_spec = pltpu.PrefetchScalarGridSpec(
        num_scalar_prefetch=1,
        grid=(b, npg // gp),
        in_specs=[page_spec(w) for w in range(gp)]
                 + [pl.BlockSpec(w1pair.shape, lambda i, st, pt: (0, 0, 0, 0)),
                    pl.BlockSpec(pe2.shape, lambda i, st, pt: (0, 0, 0)),
                    pl.BlockSpec(w2bd.shape, lambda i, st, pt: (0, 0, 0))],
        out_specs=pl.BlockSpec((1, nb, 2 * LANE), lambda i, st, pt: (i, 0, 0)),
        scratch_shapes=[pltpu.VMEM((nrows + CMP_BLOCK, LANE), F32), pltpu.VMEM((nrows + CMP_BLOCK, LANE), F32)],
    )
    return pl.pallas_call(
        functools.partial(_compress_body, gp=gp, nrows=nrows, transposed=transposed),
        grid_spec=grid_spec,
        out_shape=jax.ShapeDtypeStruct((b, nb, 2 * LANE), MXU),
        compiler_params=_cparams(("arbitrary", "arbitrary")),
        name="nsa_compress",
    )(page_table, *([pool] * gp), w1pair, pe2, w2bd)


def _fox_prep_body(q_ref, k_ref, v_ref, c_ref, qa_o, ka_o, va_o):
    tp = q_ref.shape[1]
    lane = _iota((tp, LANE), 1)
    q = q_ref[0]
    k = k_ref[0]
    v = v_ref[0]
    c = c_ref[0] * LOG2E
    one = jnp.ones((tp, LANE), F32)
    zero = jnp.zeros((tp, LANE), F32)
    for h in range(FOX_HEADS):
        qs = q[:, (h // 2) * LANE:(h // 2 + 1) * LANE]
        ks = k[:, (h // 2) * LANE:(h // 2 + 1) * LANE]
        vs = v[:, (h // 2) * LANE:(h // 2 + 1) * LANE]
        if h % 2 == 1:
            qs = pltpu.roll(qs, HEAD_DIM, 1)
            ks = pltpu.roll(ks, HEAD_DIM, 1)
            vs = pltpu.roll(vs, HEAD_DIM, 1)
        chi, cmid, clo = _split3(jnp.broadcast_to(c[:, h:h + 1], (tp, LANE)))
        qaug = jnp.where(lane == HEAD_DIM, chi, jnp.where(lane == HEAD_DIM + 1, cmid, jnp.where(
            lane == HEAD_DIM + 2, clo, jnp.where(lane < HEAD_DIM + 6, one, zero))))
        kaug = jnp.where(lane == HEAD_DIM + 3, -chi, jnp.where(lane == HEAD_DIM + 4, -cmid, jnp.where(
            lane == HEAD_DIM + 5, -clo, jnp.where(lane < HEAD_DIM + 3, one, zero))))
        qa_o[0, h] = jnp.where(lane < HEAD_DIM, qs * QSCALE, qaug).astype(qa_o.dtype)
        ka_o[0, h] = jnp.where(lane < HEAD_DIM, ks, kaug).astype(ka_o.dtype)
        va_o[0, h] = jnp.where(lane < HEAD_DIM, vs, jnp.where(lane == HEAD_DIM, one, zero)).astype(va_o.dtype)


def _fox_prep(fq, fkv, c, tp):
    b, t, _ = fq.shape
    hspec = pl.BlockSpec((1, FOX_HEADS, tp, LANE), lambda i, j: (i, 0, j, 0))
    hshape = jax.ShapeDtypeStruct((b, FOX_HEADS, t, LANE), MXU)
    return pl.pallas_call(
        _fox_prep_body,
        grid=(b, t // tp),
        in_specs=[pl.BlockSpec((1, tp, FOX_W), lambda i, j: (i, j, 0)),
                  pl.BlockSpec((1, tp, FOX_W), lambda i, j: (i, j, 0)),
                  pl.BlockSpec((1, tp, FOX_W), lambda i, j: (i, j, 1)),
                  pl.BlockSpec((1, tp, FOX_HEADS), lambda i, j: (i, j, 0))],
        out_specs=[hspec, hspec, hspec],
        out_shape=[hshape, hshape, hshape],
        compiler_params=_cparams(("parallel", "parallel")),
        name="fox_prep",
    )(fq, fkv, fkv, c)


RC = 128
KC = 512


def _flash_chunk(s, m_ref, acc_ref, r0, v):
    rep = s.shape[1] // LANE
    m_old = m_ref[r0:r0 + RC, :]
    m_new = jnp.maximum(m_old, jnp.max(s, axis=1, keepdims=True))
    alpha = jnp.exp2(m_old - m_new)
    p = jnp.exp2(s - jnp.concatenate([m_new] * rep, axis=1))
    acc = acc_ref[r0:r0 + RC, :]
    acc_ref[r0:r0 + RC, :] = jnp.concatenate([alpha] * (acc.shape[1] // LANE), axis=1) * acc + _dot(p.astype(MXU), v)
    m_ref[r0:r0 + RC, :] = m_new


def _pfox_body(q_ref, k_ref, v_ref, o_ref, m_s, acc_s, *, tq, tk, nk):
    qi = pl.program_id(2)
    kj = pl.program_id(3)
    last = ((qi + 1) * tq - 1) // tk

    @pl.when(kj == 0)
    def _():
        m_s[...] = jnp.full_like(m_s, -MASK_BIG)
        acc_s[...] = jnp.zeros_like(acc_s)

    kc = min(KC, tk)

    def update(diag):
        items = []
        for ks in range(tk // kc):
            for hh in range(2):
                for c in range(tq // RC):
                    r0 = c * RC
                    if diag and ks * kc > r0 + RC - 1:
                        continue
                    items.append((hh, r0, ks, diag and ks * kc + kc - 1 > r0))

        def logits(i):
            hh, r0, ks, masked = items[i]
            s = _dot_nt(q_ref[0, hh, r0:r0 + RC, :], k_ref[0, hh, ks * kc:(ks + 1) * kc, :])
            if masked:
                causal = (ks * kc + _iota((RC, kc), 1)) <= (r0 + _iota((RC, kc), 0))
                s = jnp.where(causal, s, -MASK_BIG)
            return s

        def finish(i, s):
            hh, r0, ks, _ = items[i]
            _flash_chunk(s, m_s.at[hh], acc_s.at[hh], r0, v_ref[0, hh, ks * kc:(ks + 1) * kc, :])

        _pipelined(len(items), logits, finish, 4)

    crosses = kj == qi

    @pl.when((kj <= last) & jnp.logical_not(crosses))
    def _():
        update(False)

    @pl.when((kj <= last) & crosses)
    def _():
        update(True)

    @pl.when(kj == nk - 1)
    def _():
        lane = _iota((tq, LANE), 1)
        outs = []
        for hh in range(2):
            acc = acc_s[hh]
            outs.append(acc * (1.0 / acc[:, HEAD_DIM:HEAD_DIM + 1]))
        o_ref[0] = jnp.where(lane < HEAD_DIM, outs[0], pltpu.roll(outs[1], HEAD_DIM, 1))


def _prompt_fox(qa, ka, va, tq, tk):
    assert tq == tk
    b, _, t, _ = qa.shape
    nq, nk = t // tq, t // tk
    hp = FOX_HEADS // 2
    lastf = lambda qi: ((qi + 1) * tq - 1) // tk
    kspec = pl.BlockSpec((1, 2, tk, LANE), lambda i, h, qi, kj: (i, h, jnp.minimum(kj, lastf(qi)), 0))
    return pl.pallas_call(
        functools.partial(_pfox_body, tq=tq, tk=tk, nk=nk),
        grid=(b, hp, nq, nk),
        in_specs=[pl.BlockSpec((1, 2, tq, LANE), lambda i, h, qi, kj: (i, h, qi, 0)), kspec, kspec],
        out_specs=pl.BlockSpec((1, tq, LANE), lambda i, h, qi, kj: (i, qi, h)),
        out_shape=jax.ShapeDtypeStruct((b, t, FOX_W), F32),
        scratch_shapes=[pltpu.VMEM((2, tq, LANE), F32), pltpu.VMEM((2, tq, LANE), F32)],
        compiler_params=_cparams(("parallel", "parallel", "parallel", "arbitrary")),
        name="prompt_fox",
    )(qa, ka, va)


QB = 128
NWIN = WINDOW // QB + 1


def _nsa_prep_body(k_ref, v_ref, e_ref, kcat_o, vcat_o):
    tp = k_ref.shape[1]
    lane = _iota((tp, LANE), 1)
    kcat_o[0] = jnp.concatenate([k_ref[0].astype(MXU), e_ref[...]], axis=1)
    ones_col = jnp.where(lane == 0, 1.0, 0.0).astype(MXU)
    vcat_o[0] = jnp.concatenate([v_ref[0].astype(MXU), ones_col], axis=1)


def _nsa_prep(nkv, e_mat, tp):
    b, t, _ = nkv.shape
    njp = e_mat.shape[1]
    return pl.pallas_call(
        _nsa_prep_body,
        grid=(b, t // tp),
        in_specs=[pl.BlockSpec((1, tp, LANE), lambda i, j: (i, j, 2)),
                  pl.BlockSpec((1, tp, LANE), lambda i, j: (i, j, 3)),
                  pl.BlockSpec((tp, njp), lambda i, j: (j, 0))],
        out_specs=[pl.BlockSpec((1, tp, LANE + njp), lambda i, j: (i, j, 0)),
                   pl.BlockSpec((1, tp, 2 * LANE), lambda i, j: (i, j, 0))],
        out_shape=[jax.ShapeDtypeStruct((b, t, LANE + njp), MXU), jax.ShapeDtypeStruct((b, t, 2 * LANE), MXU)],
        compiler_params=_cparams(("parallel", "parallel")),
        name="nsa_prep",
    )(nkv, nkv, e_mat)


def _pnsa_body(q_ref, g_ref, kc_ref, vc_ref, kcat_ref, vcat_ref, *refs, t_len, tk, nk):
    wk = refs[:NWIN]
    wv = refs[NWIN:2 * NWIN]
    ovt_ref, band_ref, bandc_ref, o_ref, qa_s, oc_s, ow_s, m_s, acc_s = refs[2 * NWIN:]
    qi = pl.program_id(1)
    kj = pl.program_id(2)
    t0 = qi * QB
    last = (t0 + QB - 1) // tk
    njp = ovt_ref.shape[0]
    nc = t_len // CMP_STRIDE
    n_c = (t_len - CMP_BLOCK) // CMP_STRIDE + 1
    nhg = KVH * GRP
    ahead = 4

    @pl.when(kj == 0)
    def _():
        q = q_ref[0]
        lane = _iota((QB, LANE), 1)
        for hg in range(nhg):
            h = hg // GRP
            sl = q[:, (hg // 2) * LANE:(hg // 2 + 1) * LANE]
            if hg % 2 != h:
                sl = pltpu.roll(sl, HEAD_DIM, 1)
            qa_s[hg * QB:(hg + 1) * QB, 0:LANE] = (jnp.where((lane // HEAD_DIM) == h, sl, 0.0) * QSCALE).astype(MXU)

        kcb = kc_ref[0]
        vcb = vc_ref[0]
        shift_t = ((_iota((nc, LANE), 0) == 8 * qi - 9 + _iota((nc, LANE), 1))
                   & (_iota((nc, LANE), 1) < 16)).astype(MXU)
        rhs = jnp.concatenate([kcb, shift_t, shift_t, shift_t], axis=1)
        trow = t0 + _iota((QB, nc), 0)
        ncol = _iota((QB, nc), 1)
        maskc = (CMP_STRIDE * ncol + (CMP_BLOCK - 1) <= trow) & (ncol < n_c)
        pcs = [None] * KVH

        def cmp_logits(hg):
            r0 = hg * QB
            lhs = jnp.concatenate([qa_s[r0:r0 + QB, 0:LANE]]
                                  + [t.astype(MXU) for t in _split3(bandc_ref[r0:r0 + QB, :])], axis=1)
            return _dot_nt(lhs, rhs)

        def cmp_finish(hg, s):
            r0 = hg * QB
            pc = _softmax2_rows(jnp.where(maskc, s, NEG_INF))
            oc_s[r0:r0 + QB, :] = _dot(pc.astype(MXU), vcb)
            h = hg // GRP
            pcs[h] = pc if pcs[h] is None else pcs[h] + pc

        _pipelined(nhg, cmp_logits, cmp_finish, ahead)
        pcsum = jnp.concatenate(pcs, axis=0)
        imp = _dot_nt(ovt_ref[...], pcsum.astype(MXU))
        width = KVH * QB
        jr = _iota((njp, width), 0)
        qblk = (t0 + _iota((njp, width), 1) % QB) // SEL_BLOCK
        forced = (jr == 0) | (jr == qblk) | (jr == qblk - 1)
        imp = jnp.where(forced, jnp.inf, jnp.where(jr > qblk, NEG_INF, imp))
        picked = _top_k_mask(imp, N_SELECT)
        pen_t = jnp.where(picked, 0.0, -MASK_BIG).astype(MXU)
        eye = (_iota((width, width), 0) == _iota((width, width), 1)).astype(MXU)
        pen = _dot_nt(eye, pen_t).astype(MXU)
        for hg in range(nhg):
            h = hg // GRP
            qa_s[hg * QB:(hg + 1) * QB, LANE:LANE + njp] = pen[h * QB:(h + 1) * QB]

        kwin = jnp.concatenate([r[0].astype(MXU) for r in wk], axis=0)
        vwin = jnp.concatenate([r[0].astype(MXU) for r in wv], axis=0)
        ri = _iota((QB, QB), 0)
        ci = _iota((QB, QB), 1)

        def win_logits(hg):
            return _dot_nt(qa_s[hg * QB:(hg + 1) * QB, 0:LANE], kwin)

        def win_finish(hg, s):
            r0 = hg * QB
            blocks = []
            for w in range(NWIN):
                sw = s[:, w * QB:(w + 1) * QB]
                ok = qi - (NWIN - 1) + w >= 0
                if w == NWIN - 1:
                    sw = jnp.where((ci <= ri) & ok, sw + band_ref[r0:r0 + QB, QB:2 * QB], NEG_INF)
                elif w == NWIN - 2:
                    sw = jnp.where(ok, sw + band_ref[r0:r0 + QB, 0:QB], NEG_INF)
                elif w == 0:
                    sw = jnp.where((ci > ri) & ok, sw, NEG_INF)
                else:
                    sw = jnp.where(ok, sw, NEG_INF)
                blocks.append(sw)
            pw = _softmax2_rows(jnp.concatenate(blocks, axis=1))
            ow_s[r0:r0 + QB, :] = _dot(pw.astype(MXU), vwin)

        _pipelined(nhg, win_logits, win_finish, ahead)

        m_s[...] = jnp.full_like(m_s, -MASK_BIG)
        acc_s[...] = jnp.zeros_like(acc_s)

    kc = min(KC, tk)
    nsub = tk // kc

    def step(near):
        def logits(i):
            ks, r0 = i // nhg, (i % nhg) * QB
            s = _dot_nt(qa_s[r0:r0 + QB, :], kcat_ref[0, ks * kc:(ks + 1) * kc, :])
            if near:
                trow = t0 + _iota((QB, kc), 0)
                scol = kj * tk + ks * kc + _iota((QB, kc), 1)
                chunks = []
                for c in range(kc // QB):
                    delta = qi - ((kj * tk + ks * kc) // QB + c)
                    chunks.append(jnp.where(delta == 0, band_ref[r0:r0 + QB, QB:2 * QB],
                                            jnp.where(delta == 1, band_ref[r0:r0 + QB, 0:QB], 0.0)))
                s = jnp.where(scol <= trow, s + jnp.concatenate(chunks, axis=1), -MASK_BIG)
            return s

        def finish(i, s):
            ks, r0 = i // nhg, (i % nhg) * QB
            _flash_chunk(s, m_s, acc_s, r0, vcat_ref[0, ks * kc:(ks + 1) * kc, :])

        _pipelined(nsub * nhg, logits, finish, ahead)

    near_from = (t0 - QB) // tk

    @pl.when(kj < near_from)
    def _():
        step(False)

    @pl.when((kj >= near_from) & (kj <= last))
    def _():
        step(True)

    @pl.when(kj == nk - 1)
    def _():
        gt = g_ref[0]
        lane = _iota((QB, LANE), 1)
        outs = []
        for hg in range(nhg):
            r0 = hg * QB
            acc = acc_s[r0:r0 + QB, :]
            osl = acc[:, 0:LANE] * (1.0 / acc[:, LANE:LANE + 1])
            off = FOX_HEADS + hg * 3
            outs.append(gt[:, off:off + 1] * oc_s[r0:r0 + QB, :] + gt[:, off + 1:off + 2] * osl
                        + gt[:, off + 2:off + 3] * ow_s[r0:r0 + QB, :])
        for pr in range(nhg // 2):
            h = (2 * pr) // GRP
            ev, od = outs[2 * pr], outs[2 * pr + 1]
            if h == 0:
                od = pltpu.roll(od, HEAD_DIM, 1)
            else:
                ev = pltpu.roll(ev, HEAD_DIM, 1)
            o_ref[0, :, pr * LANE:(pr + 1) * LANE] = jnp.where(lane < HEAD_DIM, ev, od)


def _prompt_nsa(nq, ng, cmp2, kcat, vcat, kw, ovt, band, bandc, tk):
    b, t, _ = nq.shape
    nqb, nk = t // QB, t // tk
    nc = cmp2.shape[1]
    njp = ovt.shape[0]
    rows = KVH * GRP * QB
    lastf = lambda qi: (qi * QB + QB - 1) // tk
    kjc = lambda qi, kj: jnp.minimum(kj, lastf(qi))
    wspec = lambda w, part: pl.BlockSpec((1, QB, LANE), lambda i, qi, kj: (i, jnp.maximum(qi - (NWIN - 1) + w, 0), part))
    const = lambda a: pl.BlockSpec(a.shape, lambda i, qi, kj: (0,) * a.ndim)
    return pl.pallas_call(
        functools.partial(_pnsa_body, t_len=t, tk=tk, nk=nk),
        grid=(b, nqb, nk),
        in_specs=[pl.BlockSpec((1, QB, NSA_Q_W), lambda i, qi, kj: (i, qi, 0)),
                  pl.BlockSpec((1, QB, LANE), lambda i, qi, kj: (i, qi, 0)),
                  pl.BlockSpec((1, nc, LANE), lambda i, qi, kj: (i, 0, 0)),
                  pl.BlockSpec((1, nc, LANE), lambda i, qi, kj: (i, 0, 1)),
                  pl.BlockSpec((1, tk, LANE + njp), lambda i, qi, kj: (i, kjc(qi, kj), 0)),
                  pl.BlockSpec((1, tk, 2 * LANE), lambda i, qi, kj: (i, kjc(qi, kj), 0))]
                 + [wspec(w, 0) for w in range(NWIN)] + [wspec(w, 1) for w in range(NWIN)]
                 + [const(ovt), const(band), const(bandc)],
        out_specs=pl.BlockSpec((1, QB, NSA_Q_W), lambda i, qi, kj: (i, qi, 0)),
        out_shape=jax.ShapeDtypeStruct((b, t, NSA_Q_W), F32),
        scratch_shapes=[pltpu.VMEM((rows, LANE + njp), MXU), pltpu.VMEM((rows, LANE), F32), pltpu.VMEM((rows, LANE), F32),
                        pltpu.VMEM((rows, LANE), F32), pltpu.VMEM((rows, 2 * LANE), F32)],
        compiler_params=_cparams(("parallel", "parallel", "arbitrary")),
        name="prompt_nsa",
    )(nq, ng, cmp2, cmp2, kcat, vcat, *([kw] * (2 * NWIN)), ovt, band, bandc)


def _dfox_body(pt_ref, *refs, gp, ns, s_new):
    kpages = refs[:gp]
    vpages = refs[gp:2 * gp]
    q_ref, cq_ref, ck_ref, ckn_ref, new_ref, o_ref, m_s, l_s, acc_s = refs[2 * gp:]
    st = pl.program_id(1)
    rows = FOX_HEADS * s_new
    qb = (q_ref[0] * QSCALE).astype(MXU)

    @pl.when(st == 0)
    def _():
        m_s[...] = jnp.full_like(m_s, NEG_INF)
        l_s[...] = jnp.zeros_like(l_s)
        acc_s[...] = jnp.zeros_like(acc_s)

    def flat(x):
        return x.reshape(FOX_W, PAGE).astype(MXU)

    @pl.when(st < ns)
    def _():
        s = jnp.concatenate([_dot(qb, flat(kp[0, 0])) for kp in kpages], axis=1)
        ck = jnp.concatenate([ck_ref[0]] * s_new, axis=0)
        lg = s + (cq_ref[0] - ck) * LOG2E
        alpha, p = _online_update(lg, m_s, l_s)
        pv = jnp.zeros((rows, FOX_W), F32)
        for w in range(gp):
            pv = pv + _dot_nt(p[:, w * PAGE:(w + 1) * PAGE].astype(MXU), flat(vpages[w][0, 0]))
        acc_s[...] = alpha * acc_s[...] + pv

    @pl.when(st == ns)
    def _():
        s = _dot(qb, flat(new_ref[0, 0]))
        ck = jnp.concatenate([ckn_ref[0]] * s_new, axis=0)
        qrow = _iota((rows, PAGE), 0) // FOX_HEADS
        kcol = _iota((rows, PAGE), 1)
        lg = jnp.where(kcol <= qrow, s + (cq_ref[0] - ck) * LOG2E, NEG_INF)
        alpha, p = _online_update(lg, m_s, l_s)
        acc = alpha * acc_s[...] + _dot_nt(p.astype(MXU), flat(new_ref[0, 1]))
        o = acc * (1.0 / l_s[...])
        own = (_iota((rows, FOX_W), 1) // HEAD_DIM) == (_iota((rows, FOX_W), 0) % FOX_HEADS)
        o = jnp.where(own, o, 0.0)
        o_ref[0] = jnp.sum(o.reshape(s_new, FOX_HEADS, FOX_W), axis=1)


def _decode_fox(cache_t, page_table, qrows, cq, ct_all, new_t, gp):
    b, npg = page_table.shape
    ns = npg // gp
    rows = qrows.shape[1]
    s_new = rows // FOX_HEADS
    stc = lambda st: jnp.minimum(st, ns - 1)
    page_spec = lambda w, part: pl.BlockSpec((1, 1, FOX_HEADS, HEAD_DIM, PAGE),
                                             lambda i, st, pt: (pt[i, stc(st) * gp + w], part, 0, 0, 0))
    grid_spec = pltpu.PrefetchScalarGridSpec(
        num_scalar_prefetch=1,
        grid=(b, ns + 1),
        in_specs=[page_spec(w, 0) for w in range(gp)] + [page_spec(w, 1) for w in range(gp)]
                 + [pl.BlockSpec((1, rows, FOX_W), lambda i, st, pt: (i, 0, 0)),
                    pl.BlockSpec((1, rows, 1), lambda i, st, pt: (i, 0, 0)),
                    pl.BlockSpec((1, FOX_HEADS, gp * PAGE), lambda i, st, pt: (i, 0, stc(st))),
                    pl.BlockSpec((1, FOX_HEADS, PAGE), lambda i, st, pt: (i, 0, npg)),
                    pl.BlockSpec((1, 2, FOX_HEADS, HEAD_DIM, PAGE), lambda i, st, pt: (i, 0, 0, 0, 0))],
        out_specs=pl.BlockSpec((1, s_new, FOX_W), lambda i, st, pt: (i, 0, 0)),
        scratch_shapes=[pltpu.VMEM((rows, 1), F32), pltpu.VMEM((rows, 1), F32), pltpu.VMEM((rows, FOX_W), F32)],
    )
    return pl.pallas_call(
        functools.partial(_dfox_body, gp=gp, ns=ns, s_new=s_new),
        grid_spec=grid_spec,
        out_shape=jax.ShapeDtypeStruct((b, s_new, FOX_W), F32),
        compiler_params=_cparams(("parallel", "arbitrary")),
        name="decode_fox",
    )(page_table, *([cache_t] * (2 * gp)), qrows, cq, ct_all, ct_all, new_t)


def _dnsa_body(pt_ref, *refs, gp, ns, nj, njp):
    kpages = refs[:gp]
    vpages = refs[gp:2 * gp]
    (q_ref, g_ref, kc_ref, vc_ref, ovt_ref, bcmp_ref, bwin_ref, blast_ref, bnew_ref,
     win_ref, wnew_ref, snew_ref, o_ref, sel_s, oc_s, ow_s, m_s, l_s, acc_s) = refs[2 * gp:]
    st = pl.program_id(1)
    rows = q_ref.shape[1]
    rq = rows // GRP
    qb = (q_ref[0] * QSCALE).astype(MXU)

    @pl.when(st == 0)
    def _():
        kc = kc_ref[0].astype(MXU)
        vc = vc_ref[0].astype(MXU)
        pc = _softmax2_rows(_dot_nt(qb, kc) + bcmp_ref[...])
        oc_s[...] = _dot(pc.astype(MXU), vc)
        pcs = pc[0:rq]
        for g in range(1, GRP):
            pcs = pcs + pc[g * rq:(g + 1) * rq]
        pcs = jnp.concatenate([pcs, jnp.zeros((LANE - rq, pcs.shape[1]), F32)], axis=0)
        imp = _dot_nt(ovt_ref[...], pcs.astype(MXU))
        jr = _iota((njp, LANE), 0)
        qblk = ((nj - 1) * SEL_BLOCK + _iota((njp, LANE), 1) // KVH) // SEL_BLOCK
        forced = (jr == 0) | (jr == qblk) | (jr == qblk - 1)
        imp = jnp.where(forced, jnp.inf, jnp.where(jr > qblk, NEG_INF, imp))
        sel_t = _top_k_mask(imp, N_SELECT).astype(BF16)
        eye = (_iota((LANE, LANE), 0) == _iota((LANE, LANE), 1)).astype(BF16)
        sel = _dot_nt(eye, sel_t).astype(BF16)
        sel_s[...] = jnp.concatenate([sel[0:rq]] * GRP, axis=0)

        flat = lambda x: x.reshape(LANE, x.shape[-1]).astype(MXU)
        bw = bwin_ref[...]
        sw = jnp.concatenate([_dot(qb, flat(win_ref[0, 0])) + bw[:, 0:WINDOW],
                              _dot(qb, flat(wnew_ref[0, 0])) + bw[:, WINDOW:WINDOW + PAGE]], axis=1)
        pw = _softmax2_rows(sw)
        ow_s[...] = (_dot_nt(pw[:, 0:WINDOW].astype(MXU), flat(win_ref[0, 1]))
                     + _dot_nt(pw[:, WINDOW:WINDOW + PAGE].astype(MXU), flat(wnew_ref[0, 1])))

        m_s[...] = jnp.full_like(m_s, NEG_INF)
        l_s[...] = jnp.zeros_like(l_s)
        acc_s[...] = jnp.zeros_like(acc_s)

    @pl.when(st < ns)
    def _():
        flat = lambda x: x.reshape(LANE, PAGE).astype(MXU)
        s = jnp.concatenate([_dot(qb, flat(kp[0, 0])) for kp in kpages], axis=1)
        width = gp * PAGE
        expand = (_iota((njp, width), 0) == (st * width + _iota((njp, width), 1)) // SEL_BLOCK).astype(BF16)
        mexp = _dot(sel_s[...], expand)
        tail = jnp.where(st == ns - 1, blast_ref[...], 0.0)
        if gp > 1:
            tail = jnp.concatenate([jnp.zeros((rows, width - PAGE), F32), tail], axis=1)
        lg = jnp.where(mexp > 0.5, s + tail, NEG_INF)
        alpha, p = _online_update(lg, m_s, l_s)
        pv = jnp.zeros((rows, LANE), F32)
        for w in range(gp):
            pv = pv + _dot_nt(p[:, w * PAGE:(w + 1) * PAGE].astype(MXU), flat(vpages[w][0, 0]))
        acc_s[...] = alpha * acc_s[...] + pv

    @pl.when(st == ns)
    def _():
        flat = lambda x: x.reshape(LANE, PAGE).astype(MXU)
        s = _dot(qb, flat(snew_ref[0, 0])) + bnew_ref[...]
        picked = sel_s[:, nj - 1:nj].astype(F32) > 0.5
        lg = jnp.where(picked, s, NEG_INF)
        alpha, p = _online_update(lg, m_s, l_s)
        acc = alpha * acc_s[...] + _dot_nt(p.astype(MXU), flat(snew_ref[0, 1]))
        osl = acc * (1.0 / jnp.maximum(l_s[...], 1e-30))
        gt = g_ref[0]
        of = gt[:, 0:1] * oc_s[...] + gt[:, 1:2] * osl + gt[:, 2:3] * ow_s[...]
        head = _iota((rows, LANE), 0) % KVH
        of = jnp.where(head == 0, of, pltpu.roll(of, HEAD_DIM, 1))
        o_ref[0] = of[:, 0:HEAD_DIM]


def _decode_nsa(cache_t, page_table, qrows, gates, cmp2, ovt, bcmp, bwin, blast, bnew, win_t, wnew_t, snew_t, gp, nj):
    b, npg = page_table.shape
    ns = npg // gp
    rows = qrows.shape[1]
    ncp = cmp2.shape[1]
    njp = ovt.shape[0]
    stc = lambda st: jnp.minimum(st, ns - 1)
    page_spec = lambda w, typ: pl.BlockSpec((1, 1, KVH, HEAD_DIM, PAGE),
                                            lambda i, st, pt: (pt[i, stc(st) * gp + w], typ, 0, 0, 0))
    const2 = lambda a: pl.BlockSpec(a.shape, lambda i, st, pt: (0, 0))
    own5 = lambda a: pl.BlockSpec((1,) + a.shape[1:], lambda i, st, pt: (i, 0, 0, 0, 0))
    grid_spec = pltpu.PrefetchScalarGridSpec(
        num_scalar_prefetch=1,
        grid=(b, ns + 1),
        in_specs=[page_spec(w, 2) for w in range(gp)] + [page_spec(w, 3) for w in range(gp)]
                 + [pl.BlockSpec((1, rows, LANE), lambda i, st, pt: (i, 0, 0)),
                    pl.BlockSpec((1, rows, 3), lambda i, st, pt: (i, 0, 0)),
                    pl.BlockSpec((1, ncp, LANE), lambda i, st, pt: (i, 0, 0)),
                    pl.BlockSpec((1, ncp, LANE), lambda i, st, pt: (i, 0, 1)),
                    const2(ovt), const2(bcmp), const2(bwin), const2(blast), const2(bnew),
                    own5(win_t), own5(wnew_t), own5(snew_t)],
        out_specs=pl.BlockSpec((1, rows, HEAD_DIM), lambda i, st, pt: (i, 0, 0)),
        scratch_shapes=[pltpu.VMEM((rows, njp), BF16), pltpu.VMEM((rows, LANE), F32), pltpu.VMEM((rows, LANE), F32),
                        pltpu.VMEM((rows, 1), F32), pltpu.VMEM((rows, 1), F32), pltpu.VMEM((rows, LANE), F32)],
    )
    return pl.pallas_call(
        functools.partial(_dnsa_body, gp=gp, ns=ns, nj=nj, njp=njp),
        grid_spec=grid_spec,
        out_shape=jax.ShapeDtypeStruct((b, rows, HEAD_DIM), F32),
        compiler_params=_cparams(("parallel", "arbitrary")),
        name="decode_nsa",
    )(page_table, *([cache_t] * (2 * gp)), qrows, gates, cmp2, cmp2, ovt, bcmp, bwin, blast, bnew, win_t, wnew_t, snew_t)


def _merge_body(on_ref, of_ref, ga_ref, gb_ref, x_ref, wa, wb, wo, g1, b1, o_ref, *, alpha):
    a = _dot(on_ref[...].astype(MXU), wa[...])
    bb = _dot(of_ref[...].astype(MXU), wb[...])
    mix = _dot((ga_ref[...] * a + gb_ref[...] * bb).astype(MXU), wo[...])
    o_ref[...] = _layer_norm(alpha * x_ref[...] + mix, g1[...], b1[...])


def _merge(o_nsa, o_fox, sga, sgb, x2d, wa, wb, wo, g1, b1, tm, alpha):
    m, d = x2d.shape
    row = lambda n: pl.BlockSpec((tm, n), lambda i: (i, 0))
    full = lambda a: pl.BlockSpec(a.shape, lambda i: (0, 0))
    wa, wb, wo = wa.astype(MXU), wb.astype(MXU), wo.astype(MXU)
    g1, b1 = g1.reshape(1, d), b1.reshape(1, d)
    return pl.pallas_call(
        functools.partial(_merge_body, alpha=alpha),
        grid=(m // tm,),
        in_specs=[row(NSA_Q_W), row(FOX_W), row(d), row(d), row(d), full(wa), full(wb), full(wo), full(g1), full(b1)],
        out_specs=row(d),
        out_shape=jax.ShapeDtypeStruct((m, d), F32),
        compiler_params=_cparams(("parallel",)),
        name="merge_ln1",
    )(o_nsa, o_fox, sga, sgb, x2d, wa, wb, wo, g1, b1)


def _ffn_up_long_body(x_ref, prev_ref, wu, wg, cw, cb, h_ref, st_ref, carry_s, *, tiles_per_seq):
    i = pl.program_id(0)
    tm = x_ref.shape[0]
    xb = x_ref[...].astype(MXU)
    u = _dot(xb, wu[...])

    @pl.when(i % tiles_per_seq == 0)
    def _():
        carry_s[...] = prev_ref[0]

    r = _iota(u.shape, 0)
    um1 = jnp.where(r >= 1, pltpu.roll(u, 1, 0), carry_s[1:2, :])
    um2 = jnp.where(r >= 2, pltpu.roll(u, 2, 0), jnp.where(r == 0, carry_s[0:1, :], carry_s[1:2, :]))
    conv = cb[...] + cw[0:1, :] * um2 + cw[1:2, :] * um1 + cw[2:3, :] * u
    h_ref[...] = (jax.nn.gelu(conv) * _dot(xb, wg[...])).astype(h_ref.dtype)
    last2 = u[tm - 2:tm, :]
    carry_s[...] = last2
    st_ref[0] = last2


def _ffn_up_short_body(x_ref, p1_ref, p2_ref, wu, wg, cw, cb, h_ref, u_ref, *, seq):
    xb = x_ref[...].astype(MXU)
    u = _dot(xb, wu[...])
    t = _iota(u.shape, 0) % seq
    um1 = jnp.where(t >= 1, pltpu.roll(u, 1, 0), p1_ref[...])
    um2 = jnp.where(t >= 2, pltpu.roll(u, 2, 0), p2_ref[...])
    conv = cb[...] + cw[0:1, :] * um2 + cw[1:2, :] * um1 + cw[2:3, :] * u
    h_ref[...] = (jax.nn.gelu(conv) * _dot(xb, wg[...])).astype(h_ref.dtype)
    u_ref[...] = u


def _ffn_up_long(x1, conv_prev, wu, wg, cw, cb, seq, tm):
    m, d = x1.shape
    f = wu.shape[1]
    nseq = m // seq
    tps = seq // tm
    wu, wg = wu.astype(MXU), wg.astype(MXU)
    cb = cb.reshape(1, f)
    full = lambda a: pl.BlockSpec(a.shape, lambda i: (0, 0))
    return pl.pallas_call(
        functools.partial(_ffn_up_long_body, tiles_per_seq=tps),
        grid=(m // tm,),
        in_specs=[pl.BlockSpec((tm, d), lambda i: (i, 0)),
                  pl.BlockSpec((1, 2, f), lambda i: (i // tps, 0, 0)),
                  full(wu), full(wg), full(cw), full(cb)],
        out_specs=[pl.BlockSpec((tm, f), lambda i: (i, 0)),
                   pl.BlockSpec((1, 2, f), lambda i: (i // tps, 0, 0))],
        out_shape=[jax.ShapeDtypeStruct((m, f), MXU), jax.ShapeDtypeStruct((nseq, 2, f), F32)],
        scratch_shapes=[pltpu.VMEM((2, f), F32)],
        compiler_params=_cparams(("arbitrary",)),
        name="ffn_up_long",
    )(x1, conv_prev, wu, wg, cw, cb)


def _ffn_up_short(x1, conv_prev, wu, wg, cw, cb, seq):
    m, d = x1.shape
    f = wu.shape[1]
    nseq = m // seq
    wu, wg = wu.astype(MXU), wg.astype(MXU)
    cb = cb.reshape(1, f)
    zeros = jnp.zeros((nseq, seq, f), F32)
    p1 = zeros.at[:, 0].set(conv_prev[:, 1]).reshape(m, f)
    p2 = zeros.at[:, 0].set(conv_prev[:, 0]).at[:, 1].set(conv_prev[:, 1]).reshape(m, f)
    full = lambda a: pl.BlockSpec(a.shape, lambda i: (0, 0))
    h, u = pl.pallas_call(
        functools.partial(_ffn_up_short_body, seq=seq),
        grid=(1,),
        in_specs=[full(x1), full(p1), full(p2), full(wu), full(wg), full(cw), full(cb)],
        out_specs=[pl.BlockSpec((m, f), lambda i: (0, 0)), pl.BlockSpec((m, f), lambda i: (0, 0))],
        out_shape=[jax.ShapeDtypeStruct((m, f), MXU), jax.ShapeDtypeStruct((m, f), F32)],
        compiler_params=_cparams(("arbitrary",)),
        name="ffn_up_short",
    )(x1, p1, p2, wu, wg, cw, cb)
    return h, u.reshape(nseq, seq, f)[:, seq - 2:]


def _ffn_down_body(h_ref, x1_ref, p_ref, wd, wpg, wp, g2, b2, o_ref, *, alpha):
    f = _dot(h_ref[...], wd[...])
    x2 = _layer_norm(alpha * x1_ref[...] + f, g2[...], b2[...])
    gate = jax.nn.sigmoid(_dot(x2.astype(MXU), wpg[...]))
    o_ref[...] = x2 + gate * _dot(p_ref[...].astype(MXU), wp[...])


def _ffn_down(h, x1, p2d, wd, wpg, wp, g2, b2, tm, alpha):
    m, d = x1.shape
    row = lambda n: pl.BlockSpec((tm, n), lambda i: (i, 0))
    full = lambda a: pl.BlockSpec(a.shape, lambda i: (0, 0))
    wd, wpg, wp = wd.astype(MXU), wpg.astype(MXU), wp.astype(MXU)
    g2, b2 = g2.reshape(1, d), b2.reshape(1, d)
    return pl.pallas_call(
        functools.partial(_ffn_down_body, alpha=alpha),
        grid=(m // tm,),
        in_specs=[row(h.shape[1]), row(d), row(p2d.shape[1]), full(wd), full(wpg), full(wp), full(g2), full(b2)],
        out_specs=row(d),
        out_shape=jax.ShapeDtypeStruct((m, d), F32),
        compiler_params=_cparams(("parallel",)),
        name="ffn_down_ln2_ple",
    )(h, x1, p2d, wd, wpg, wp, g2, b2)


def _overlap_t(n_sel, n_sel_pad, n_c, n_c_pad):
    start = np.arange(n_c_pad)[None, :] * CMP_STRIDE
    j = np.arange(n_sel_pad)[:, None]
    ov = (start < (j + 1) * SEL_BLOCK) & (start + CMP_BLOCK > j * SEL_BLOCK)
    ov &= (np.arange(n_c_pad)[None, :] < n_c) & (j < n_sel)
    return jnp.asarray(ov, dtype=MXU)


def _block_onehot(t_len, njp):
    e = (np.arange(t_len)[:, None] // SEL_BLOCK) == np.arange(njp)[None, :]
    return jnp.asarray(e, dtype=MXU)


def _prompt_bias_idx():
    nhg = KVH * GRP
    hg = np.repeat(np.arange(nhg), QB)[:, None]
    i = np.tile(np.arange(QB), nhg)[:, None]
    c = np.arange(2 * QB)[None, :]
    band = _bias_idx(i + QB - c, hg, np.ones((nhg * QB, 2 * QB), bool))
    band = np.where(band < 0, -2, band)
    mcol = np.arange(LANE)[None, :]
    dc = i + 113 - CMP_STRIDE * mcol
    bandc = _bias_idx(dc, hg, mcol < 16)
    bandc = np.where(bandc < 0, -2, bandc)
    return band, bandc


def _decode_bias_idx(p_len, s_new, n_c, ncp):
    rows = GRP * s_new * KVH
    r = np.arange(rows)
    g, i, h = r // (s_new * KVH), (r // KVH) % s_new, r % KVH
    hg = (h * GRP + g)[:, None]
    qpos = (p_len + i)[:, None]
    ones = np.ones((rows, 1), bool)
    n = np.arange(ncp)
    bcmp = _bias_idx(qpos - (CMP_STRIDE * n + CMP_BLOCK - 1)[None, :], hg, (n < n_c)[None, :] & ones)
    wb = min(WINDOW, p_len)
    wpos = np.concatenate([p_len - wb + np.arange(wb), p_len + np.arange(PAGE)])
    wok = np.concatenate([np.ones(wb, bool), np.arange(PAGE) < s_new])[None, :]
    dwin = qpos - wpos[None, :]
    bwin = _bias_idx(dwin, hg, wok & (dwin < WINDOW))
    blast = _bias_idx(qpos - (p_len - PAGE + np.arange(PAGE))[None, :], hg, np.ones((rows, PAGE), bool))
    bnew = _bias_idx(qpos - (p_len + np.arange(PAGE))[None, :], hg, (np.arange(PAGE) < s_new)[None, :] & ones)
    return bcmp, bwin, blast, bnew


def _pick(n, prefs):
    for p in prefs:
        if n % p == 0:
            return p
    return n


def _finish(x2d, o_nsa, o_fox, sga, sgb, p2d, conv_prev, seq, W, alpha):
    m = x2d.shape[0]
    tm = _pick(m, (256, 128))
    x1 = _merge(o_nsa, o_fox, sga, sgb, x2d, W['w_branch_a'], W['w_branch_b'], W['w_out'], W['ln1_g'], W['ln1_b'],
                tm, alpha)
    if seq >= 128:
        h, conv_state = _ffn_up_long(x1, conv_prev, W['w_ffn_up'], W['w_ffn_gate'], W['ffn_conv_w'], W['ffn_conv_b'],
                                     seq, _pick(seq, (256, 128)))
    else:
        h, conv_state = _ffn_up_short(x1, conv_prev, W['w_ffn_up'], W['w_ffn_gate'], W['ffn_conv_w'],
                                      W['ffn_conv_b'], seq)
    y = _ffn_down(h, x1, p2d, W['w_ffn_down'], W['w_ple_gate'], W['w_ple'], W['ln2_g'], W['ln2_b'], tm, alpha)
    return y, conv_state


def _prompt_layer(x, p_emb, W, rel_bias, alpha):
    b, t, d = x.shape
    m = b * t
    x2d = x.reshape(m, d)
    nq, nkv, kw, fq, fkv, sga, sgb, ng, logf = _proj(x2d, W['w_in'], W['b_forget'], _pick(m, (256, 128)))
    npg = t // PAGE
    gp = _pick(npg, (8, 4, 2, 1))
    ident = jnp.arange(b * npg, dtype=jnp.int32).reshape(b, npg)
    tile = _pick(t, (1024, 512, 256, 128))

    c, _ = _paged_cumsum(logf.reshape(b * npg, PAGE, FOX_HEADS), ident, gp)
    fq3, fkv3 = fq.reshape(b, t, FOX_W), fkv.reshape(b, t, 2 * FOX_W)
    qa, ka, va = _fox_prep(fq3, fkv3, c, tile)
    o_fox = _prompt_fox(qa, ka, va, tile, tile)

    nkv3 = nkv.reshape(b, t, 4 * NSA_KV_W)
    cmp2 = _compress(nkv3.reshape(b * npg, PAGE, 4 * NSA_KV_W), ident, W['nsa_cmp_pe'], W['nsa_cmp_w1'],
                     W['nsa_cmp_w2'], gp)
    nc = cmp2.shape[1]
    n_c = (t - CMP_BLOCK) // CMP_STRIDE + 1
    nj = t // SEL_BLOCK
    njp = -(-nj // LANE) * LANE
    band_idx, bandc_idx = _prompt_bias_idx()
    kcat, vcat = _nsa_prep(nkv3, _block_onehot(t, njp), tile)
    o_nsa = _prompt_nsa(nq.reshape(b, t, NSA_Q_W), ng.reshape(b, t, LANE), cmp2, kcat, vcat,
                        kw.reshape(b, t, 2 * NSA_KV_W), _overlap_t(nj, njp, n_c, nc),
                        _bias_lookup(rel_bias, band_idx), _bias_lookup(rel_bias, bandc_idx), tile)

    conv_prev = jnp.zeros((b, 2, W['w_ffn_up'].shape[1]), F32)
    y, conv_state = _finish(x2d, o_nsa.reshape(m, NSA_Q_W), o_fox.reshape(m, FOX_W), sga, sgb,
                            p_emb.reshape(m, -1), conv_prev, t, W, alpha)
    wb = min(WINDOW, t)
    state = (fkv.reshape(b, t, 2, FOX_HEADS, HEAD_DIM), logf.reshape(b, t, FOX_HEADS),
             nkv.reshape(b, t, 4, KVH, HEAD_DIM), kw.reshape(b, t, 2, KVH, HEAD_DIM)[:, t - wb:], conv_state)
    return y.reshape(b, t, d), state


def _sample_layer(x, p_emb, c_fox_kv, c_fox_logf, c_nsa, win, conv_prev, page_table, W, rel_bias, alpha):
    b, s, d = x.shape
    m = b * s
    npg = page_table.shape[1]
    p_len = npg * PAGE
    n_pool = c_nsa.shape[0]
    x2d = x.reshape(m, d)
    nq, nkv, kw, fq, fkv, sga, sgb, ng, logf = _proj(x2d, W['w_in'], W['b_forget'], _pick(m, (256, 128, 8)))
    gp = _pick(npg, (8, 4, 2, 1))

    def rows_last(a, lead):
        a = jnp.moveaxis(a.reshape((b, s) + lead), 1, -1)
        return jnp.pad(a, ((0, 0),) * (a.ndim - 1) + ((0, PAGE - s),))

    fox_t = jnp.transpose(c_fox_kv, (0, 2, 3, 4, 1))
    nsa_t = jnp.transpose(c_nsa, (0, 2, 3, 4, 1))
    logf_t = jnp.transpose(c_fox_logf.astype(F32), (0, 2, 1))
    win_t = jnp.transpose(win, (0, 2, 3, 4, 1))

    ct_all = _paged_cumsum_t(logf_t, page_table, gp, rows_last(logf, (FOX_HEADS,)))
    cq = jnp.transpose(ct_all[:, :, p_len:p_len + s], (0, 2, 1)).reshape(b, s * FOX_HEADS, 1)
    own = (np.arange(FOX_W)[None, :] // HEAD_DIM) == (np.arange(s * FOX_HEADS)[:, None] % FOX_HEADS)
    qrows_f = jnp.where(jnp.asarray(own)[None], jnp.repeat(fq.reshape(b, s, FOX_W), FOX_HEADS, axis=1), 0.0)
    o_fox = _decode_fox(fox_t, page_table, qrows_f, cq, ct_all, rows_last(fkv, (2, FOX_HEADS, HEAD_DIM)), gp)

    cmp2 = _compress(nsa_t, page_table, W['nsa_cmp_pe'], W['nsa_cmp_w1'], W['nsa_cmp_w2'], gp, transposed=True)
    ncp = cmp2.shape[1]
    n_c = (p_len + s - CMP_BLOCK) // CMP_STRIDE + 1
    nj = -(-(p_len + s) // SEL_BLOCK)
    njp = -(-nj // LANE) * LANE
    tables = [_bias_lookup(rel_bias, ix) for ix in _decode_bias_idx(p_len, s, n_c, ncp)]
    rows = GRP * s * KVH
    h_of_row = np.arange(rows) % KVH
    qsel = jnp.transpose(nq.reshape(b, s, KVH, GRP, HEAD_DIM), (0, 3, 1, 2, 4)).reshape(b, rows, HEAD_DIM)
    half = jnp.asarray((np.arange(LANE)[None, :] // HEAD_DIM) == h_of_row[:, None])
    qrows_n = jnp.where(half[None], jnp.concatenate([qsel, qsel], axis=-1), 0.0)
    gates = jnp.transpose(ng[:, FOX_HEADS:FOX_HEADS + KVH * GRP * 3].reshape(b, s, KVH, GRP, 3),
                          (0, 3, 1, 2, 4)).reshape(b, rows, 3)
    nkv_t = rows_last(nkv, (4, KVH, HEAD_DIM))
    o_rows = _decode_nsa(nsa_t, page_table, qrows_n, gates, cmp2, _overlap_t(nj, njp, n_c, ncp), *tables,
                         win_t, rows_last(kw, (2, KVH, HEAD_DIM)), nkv_t[:, 2:4], gp, nj)
    o_nsa = jnp.transpose(o_rows.reshape(b, GRP, s, KVH, HEAD_DIM), (0, 2, 3, 1, 4)).reshape(m, NSA_Q_W)

    y, conv_state = _finish(x2d, o_nsa, o_fox.reshape(m, FOX_W), sga, sgb, p_emb.reshape(m, -1), conv_prev, s, W,
                            alpha)
    win_all = jnp.concatenate([win, kw.reshape(b, s, 2, KVH, HEAD_DIM)], axis=1)
    state = (fkv.reshape(b, s, 2, FOX_HEADS, HEAD_DIM), logf.reshape(b, s, FOX_HEADS),
             nkv.reshape(b, s, 4, KVH, HEAD_DIM), win_all[:, s:], conv_state)
    return y.reshape(b, s, d), state


def kernel(x_prompt, x_sample, p_prompt, p_sample, cache_fox_kv, cache_fox_logf, cache_nsa_kv, state_nsa_win,
           state_ffn_conv, page_table, w_in, b_forget, nsa_cmp_pe, nsa_cmp_w1, nsa_cmp_w2, rel_bias, w_branch_a,
           w_branch_b, w_out, ln1_g, ln1_b, ln2_g, ln2_b, w_ffn_up, w_ffn_gate, ffn_conv_w, ffn_conv_b, w_ffn_down,
           w_ple, w_ple_gate):
    depth = w_in.shape[0]
    alpha = (2.0 * depth) ** 0.25
    xp, xs = x_prompt, x_sample
    st_p, st_s = [], []
    for i in range(depth):
        W = {
            'w_in': w_in[i], 'b_forget': b_forget[i], 'nsa_cmp_pe': nsa_cmp_pe[i], 'nsa_cmp_w1': nsa_cmp_w1[i],
            'nsa_cmp_w2': nsa_cmp_w2[i], 'w_branch_a': w_branch_a[i], 'w_branch_b': w_branch_b[i], 'w_out': w_out[i],
            'ln1_g': ln1_g[i], 'ln1_b': ln1_b[i], 'ln2_g': ln2_g[i], 'ln2_b': ln2_b[i], 'w_ffn_up': w_ffn_up[i],
            'w_ffn_gate': w_ffn_gate[i], 'ffn_conv_w': ffn_conv_w[i], 'ffn_conv_b': ffn_conv_b[i],
            'w_ffn_down': w_ffn_down[i], 'w_ple': w_ple[i], 'w_ple_gate': w_ple_gate[i],
        }
        xp, sp = _prompt_layer(xp, p_prompt[i], W, rel_bias, alpha)
        xs, ss = _sample_layer(xs, p_sample[i], cache_fox_kv[i], cache_fox_logf[i], cache_nsa_kv[i],
                               state_nsa_win[i], state_ffn_conv[i], page_table, W, rel_bias, alpha)
        st_p.append(sp)
        st_s.append(ss)

    def stk(lst, j):
        return jnp.stack([s[j] for s in lst])

    return (xp, xs, stk(st_p, 0), stk(st_s, 0), stk(st_p, 1), stk(st_s, 1), stk(st_p, 2), stk(st_s, 2),
            stk(st_p, 3), stk(st_s, 3), stk(st_p, 4), stk(st_s, 4))
```

```python
import functools
import math

import numpy as np
import jax
import jax.numpy as jnp
from jax import lax
from jax.experimental import pallas as pl
from jax.experimental.pallas import tpu as pltpu

F32 = jnp.float32
BF16 = jnp.bfloat16
MXU = jnp.bfloat16

HEAD_DIM = 64
KVH = 2
GRP = 4
FOX_HEADS = 8
CMP_BLOCK = 32
CMP_STRIDE = 16
SEL_BLOCK = 64
N_SELECT = 16
WINDOW = 512
N_BUCKETS = 32
MAX_DISTANCE = 128
PAGE = 128
LN_EPS = 1e-5
SCALE = HEAD_DIM ** -0.5
LOG2E = math.log2(math.e)
QSCALE = SCALE * LOG2E
NSA_Q_W = KVH * GRP * HEAD_DIM
NSA_KV_W = KVH * HEAD_DIM
FOX_W = FOX_HEADS * HEAD_DIM
LANE = 128
VMEM_LIMIT = 56 * 1024 * 1024
NEG_INF = float("-inf")
MASK_BIG = 1e9


def _cparams(sem):
    return pltpu.CompilerParams(dimension_semantics=sem, vmem_limit_bytes=VMEM_LIMIT)


def _dot(a, b):
    return jnp.dot(a, b, preferred_element_type=F32)


def _dot_nt(a, b):
    return lax.dot_general(a, b, (((1,), (1,)), ((), ())), preferred_element_type=F32)


def _split3(x):
    hi = x.astype(BF16).astype(F32)
    r = x - hi
    mid = r.astype(BF16).astype(F32)
    lo = (r - mid).astype(BF16).astype(F32)
    return hi, mid, lo


def _iota(shape, dim):
    return lax.broadcasted_iota(jnp.int32, shape, dim)


def _softmax2_rows(lg):
    m = jnp.max(lg, axis=-1, keepdims=True)
    m = jnp.where(m == NEG_INF, 0.0, m)
    e = jnp.exp2(lg - m)
    d = jnp.maximum(jnp.sum(e, axis=-1, keepdims=True), 1e-30)
    return e * (1.0 / d)


def _online_update(lg, m_ref, l_ref):
    m_old = m_ref[...]
    m_new = jnp.maximum(m_old, jnp.max(lg, axis=-1, keepdims=True))
    m_safe = jnp.where(m_new == NEG_INF, 0.0, m_new)
    alpha = jnp.exp2(m_old - m_safe)
    p = jnp.exp2(lg - m_safe)
    l_ref[...] = alpha * l_ref[...] + jnp.sum(p, axis=-1, keepdims=True)
    m_ref[...] = m_new
    return alpha, p


def _pipelined(n_items, produce, consume, ahead):
    pending = {}
    for i in range(n_items + ahead):
        if i < n_items:
            pending[i] = produce(i)
        if i >= ahead:
            consume(i - ahead, pending.pop(i - ahead))


def _top_k_mask(x, k):
    n, w = x.shape
    row = _iota((n, w), 0).astype(F32)
    picked = jnp.zeros((n, w), jnp.bool_)
    work = x
    for _ in range(k):
        top = jnp.max(work, axis=0, keepdims=True)
        first = jnp.min(jnp.where(work == top, row, float(n)), axis=0, keepdims=True)
        hit = row == first
        picked = picked | (hit & (top > NEG_INF))
        work = jnp.where(hit, NEG_INF, work)
    return picked


def _layer_norm(x, g, b):
    mu = jnp.mean(x, axis=-1, keepdims=True)
    xc = x - mu
    var = jnp.mean(xc * xc, axis=-1, keepdims=True)
    return xc * lax.rsqrt(var + LN_EPS) * g + b


def _bucket_np(dist):
    n = np.maximum(np.asarray(dist), 0)
    max_exact = N_BUCKETS // 2
    nf = np.maximum(n, 1).astype(np.float32)
    large = max_exact + (np.log(nf / np.float32(max_exact)) / np.float32(math.log(MAX_DISTANCE / max_exact))
                         * np.float32(N_BUCKETS - max_exact)).astype(np.int32)
    large = np.minimum(large, N_BUCKETS - 1)
    return np.where(n < max_exact, n, large).astype(np.int32)


FAR_BUCKET = int(_bucket_np(np.array(MAX_DISTANCE)))


def _proj_body(x_ref, wnq, wnkv, wkw, wfq, wfkv, wga, wgb, wsm, bsm,
               nq_o, nkv_o, kw_o, fq_o, fkv_o, ga_o, gb_o, ng_o, logf_o):
    xb = x_ref[...].astype(MXU)
    nq_o[...] = _dot(xb, wnq[...])
    nkv_o[...] = _dot(xb, wnkv[...])
    kw_o[...] = _dot(xb, wkw[...])
    fq_o[...] = _dot(xb, wfq[...])
    fkv_o[...] = _dot(xb, wfkv[...])
    ga_o[...] = jax.nn.sigmoid(_dot(xb, wga[...]))
    gb_o[...] = jax.nn.sigmoid(_dot(xb, wgb[...]))
    sm = _dot(xb, wsm[...])
    ng_o[...] = jax.nn.sigmoid(sm)
    z = sm + bsm[...]
    ls = jnp.minimum(z, 0.0) - jnp.log1p(jnp.exp(-jnp.abs(z)))
    logf_o[...] = ls[:, :FOX_HEADS]


def _proj(x2d, w_in, b_forget, tm):
    m, d = x2d.shape
    o = np.cumsum([0, NSA_Q_W, 6 * NSA_KV_W, KVH * GRP * 3, FOX_W, FOX_W, FOX_W, FOX_HEADS, d, d])
    wb = w_in.astype(MXU)
    wnq = wb[:, o[0]:o[1]]
    wnkv = wb[:, o[1]:o[1] + 4 * NSA_KV_W]
    wkw = wb[:, o[1] + 4 * NSA_KV_W:o[2]]
    wfq = wb[:, o[3]:o[4]]
    wfkv = wb[:, o[4]:o[6]]
    wga = wb[:, o[7]:o[8]]
    wgb = wb[:, o[8]:o[9]]
    nsm = FOX_HEADS + KVH * GRP * 3
    wsm = jnp.concatenate([wb[:, o[6]:o[7]], wb[:, o[2]:o[3]], jnp.zeros((d, LANE - nsm), MXU)], axis=1)
    bsm = jnp.concatenate([b_forget.astype(F32), jnp.zeros((LANE - FOX_HEADS,), F32)]).reshape(1, LANE)
    ws = [wnq, wnkv, wkw, wfq, wfkv, wga, wgb, wsm]
    widths = [w.shape[1] for w in ws] + [FOX_HEADS]
    row = lambda n: pl.BlockSpec((tm, n), lambda i: (i, 0))
    full = lambda a: pl.BlockSpec(a.shape, lambda i: (0, 0))
    out_shapes = [jax.ShapeDtypeStruct((m, n), F32) for n in widths]
    return pl.pallas_call(
        _proj_body,
        grid=(m // tm,),
        in_specs=[row(d)] + [full(w) for w in ws] + [full(bsm)],
        out_specs=[row(n) for n in widths],
        out_shape=out_shapes,
        compiler_params=_cparams(("parallel",)),
        name="proj",
    )(x2d, *ws, bsm)


def _bias_body(tb_ref, idx_ref, o_ref):
    idx = idx_ref[...]
    nh = tb_ref.shape[1]

    def step(k, acc):
        hd = k % nh
        val = (tb_ref[k // nh, hd] - tb_ref[FAR_BUCKET, hd]) * LOG2E
        return jnp.where(idx == k, val, acc)

    acc = lax.fori_loop(0, tb_ref.shape[0] * nh, step, jnp.zeros(idx.shape, F32))
    o_ref[...] = jnp.where(idx == -1, NEG_INF, acc)


def _bias_lookup(rel_bias, idx_np):
    idx = jnp.asarray(idx_np, dtype=jnp.int32)
    return pl.pallas_call(
        _bias_body,
        in_specs=[pl.BlockSpec(memory_space=pltpu.SMEM), pl.BlockSpec(idx.shape, lambda: (0, 0))],
        out_specs=pl.BlockSpec(idx.shape, lambda: (0, 0)),
        out_shape=jax.ShapeDtypeStruct(idx.shape, F32),
        name="bias_tables",
    )(rel_bias.astype(F32), idx)


def _bias_idx(dist, head, ok):
    nh = KVH * GRP
    return np.where(ok & (dist >= 0), _bucket_np(dist) * nh + head, -1).astype(np.int32)


def _cumsum_body(pt_ref, *refs, gp, ns, has_new):
    pages = refs[:gp]
    rest = refs[gp:]
    new_ref = rest[0] if has_new else None
    c_o, ct_o, pad_s, carry_s = rest[1:] if has_new else rest
    st = pl.program_id(1)

    @pl.when(st == 0)
    def _():
        pad_s[...] = jnp.zeros_like(pad_s)
        carry_s[...] = jnp.zeros_like(carry_s)

    ltri = (_iota((PAGE, PAGE), 1) <= _iota((PAGE, PAGE), 0)).astype(BF16)

    def one_page(x, w):
        pad_s[:, 0:FOX_HEADS] = x
        xp = pad_s[...]
        cs = sum(_dot(ltri, t.astype(BF16)) for t in _split3(xp)) + carry_s[...]
        c_o[0, w * PAGE:(w + 1) * PAGE, :] = cs[:, 0:FOX_HEADS]
        ct_o[0, :, w * PAGE:(w + 1) * PAGE] = cs.T[0:FOX_HEADS, :]
        carry_s[...] = cs[PAGE - 1:PAGE, :]

    @pl.when(st < ns)
    def _():
        for w in range(gp):
            one_page(pages[w][0], w)

    if has_new:
        @pl.when(st == ns)
        def _():
            one_page(new_ref[0], 0)


def _paged_cumsum(pool, page_table, gp, new_page=None):
    b, npg = page_table.shape
    ns = npg // gp
    has_new = new_page is not None
    nsteps = ns + (1 if has_new else 0)
    stc = lambda st: jnp.minimum(st, ns - 1)
    in_specs = [pl.BlockSpec((1, PAGE, FOX_HEADS), functools.partial(
        lambda i, st, pt, w: (pt[i, stc(st) * gp + w], 0, 0), w=w)) for w in range(gp)]
    args = [pool] * gp
    if has_new:
        in_specs.append(pl.BlockSpec((1, PAGE, FOX_HEADS), lambda i, st, pt: (i, 0, 0)))
        args.append(new_page)
    width = gp * PAGE
    grid_spec = pltpu.PrefetchScalarGridSpec(
        num_scalar_prefetch=1,
        grid=(b, nsteps),
        in_specs=in_specs,
        out_specs=[pl.BlockSpec((1, width, FOX_HEADS), lambda i, st, pt: (i, st, 0)),
                   pl.BlockSpec((1, FOX_HEADS, width), lambda i, st, pt: (i, 0, st))],
        scratch_shapes=[pltpu.VMEM((PAGE, LANE), F32), pltpu.VMEM((1, LANE), F32)],
    )
    return pl.pallas_call(
        functools.partial(_cumsum_body, gp=gp, ns=ns, has_new=has_new),
        grid_spec=grid_spec,
        out_shape=[jax.ShapeDtypeStruct((b, nsteps * width, FOX_HEADS), F32),
                   jax.ShapeDtypeStruct((b, FOX_HEADS, nsteps * width), F32)],
        compiler_params=_cparams(("arbitrary", "arbitrary")),
        name="logf_cumsum",
    )(page_table, *args)


def _cumsum_t_body(pt_ref, *refs, gp, ns):
    pages = refs[:gp]
    new_ref, ct_o, carry_s = refs[gp:]
    st = pl.program_id(1)

    @pl.when(st == 0)
    def _():
        carry_s[...] = jnp.zeros_like(carry_s)

    utri = (_iota((PAGE, PAGE), 0) <= _iota((PAGE, PAGE), 1)).astype(BF16)

    def scan(xs):
        pad = [jnp.zeros_like(xs[0])] * (len(xs) % 2)
        x = jnp.concatenate(list(xs) + pad, axis=0)
        local = sum(_dot(t.astype(BF16), utri) for t in _split3(x))
        run = carry_s[...]
        for w in range(len(xs)):
            lw = local[w * FOX_HEADS:(w + 1) * FOX_HEADS]
            ct_o[0, :, w * PAGE:(w + 1) * PAGE] = lw + run
            run = run + jnp.broadcast_to(lw[:, PAGE - 1:PAGE], run.shape)
        carry_s[...] = run

    @pl.when(st < ns)
    def _():
        scan([pg[0] for pg in pages])

    @pl.when(st == ns)
    def _():
        scan([new_ref[0]])
        if gp > 1:
            ct_o[0, :, PAGE:gp * PAGE] = jnp.zeros((FOX_HEADS, (gp - 1) * PAGE), F32)


def _paged_cumsum_t(pool_t, page_table, gp, new_page_t):
    b, npg = page_table.shape
    ns = npg // gp
    stc = lambda st: jnp.minimum(st, ns - 1)
    page_spec = lambda w: pl.BlockSpec((1, FOX_HEADS, PAGE), lambda i, st, pt: (pt[i, stc(st) * gp + w], 0, 0))
    width = gp * PAGE
    grid_spec = pltpu.PrefetchScalarGridSpec(
        num_scalar_prefetch=1,
        grid=(b, ns + 1),
        in_specs=[page_spec(w) for w in range(gp)] + [pl.BlockSpec((1, FOX_HEADS, PAGE), lambda i, st, pt: (i, 0, 0))],
        out_specs=pl.BlockSpec((1, FOX_HEADS, width), lambda i, st, pt: (i, 0, st)),
        scratch_shapes=[pltpu.VMEM((FOX_HEADS, LANE), F32)],
    )
    return pl.pallas_call(
        functools.partial(_cumsum_t_body, gp=gp, ns=ns),
        grid_spec=grid_spec,
        out_shape=jax.ShapeDtypeStruct((b, FOX_HEADS, (ns + 1) * width), F32),
        compiler_params=_cparams(("arbitrary", "arbitrary")),
        name="logf_cumsum_t",
    )(page_table, *([pool_t] * gp), new_page_t)


def _compress_body(pt_ref, *refs, gp, nrows, transposed):
    pages = refs[:gp]
    w1_ref, pe_ref, w2_ref, out_ref, buf_k, buf_v = refs[gp:]
    st = pl.program_id(1)
    ns = nrows // (gp * PAGE)
    nb = nrows // CMP_STRIDE
    bufs = (buf_k, buf_v)

    @pl.when(st == 0)
    def _():
        for buf in bufs:
            buf[nrows:nrows + CMP_BLOCK, :] = jnp.zeros((CMP_BLOCK, LANE), F32)

    eye = (_iota((PAGE, PAGE), 0) == _iota((PAGE, PAGE), 1)).astype(MXU)
    for w in range(gp):
        base = pl.multiple_of((st * gp + w) * PAGE, PAGE)
        for typ, buf in enumerate(bufs):
            if transposed:
                xt = pages[w][0, typ].reshape(LANE, PAGE).astype(MXU)
                buf[pl.ds(base, PAGE), :] = _dot_nt(eye, xt)
            else:
                buf[pl.ds(base, PAGE), :] = pages[w][0, :, typ * LANE:(typ + 1) * LANE].astype(F32)

    @pl.when(st == ns - 1)
    def _():
        for typ, buf in enumerate(bufs):
            acc = jnp.zeros((nb, 2 * LANE), F32)
            bias = jnp.zeros((8, 2 * LANE), F32)
            for lp in range(CMP_BLOCK // 2):
                xa = buf[pl.ds(2 * lp, nb, stride=CMP_STRIDE), :]
                xb = buf[pl.ds(2 * lp + 1, nb, stride=CMP_STRIDE), :]
                x2 = jnp.concatenate([xa, xb], axis=1).astype(MXU)
                acc = acc + _dot(x2, w1_ref[typ, lp])
                pe2 = jnp.concatenate([pe_ref[typ, 2 * lp:2 * lp + 1, :], pe_ref[typ, 2 * lp + 1:2 * lp + 2, :]], axis=1)
                bias = bias + _dot(jnp.broadcast_to(pe2, (8, 2 * LANE)).astype(MXU), w1_ref[typ, lp])
            hid = jax.nn.gelu(acc + bias[0:1, :])
            out_ref[0, :, typ * LANE:(typ + 1) * LANE] = _dot(hid.astype(MXU), w2_ref[typ]).astype(out_ref.dtype)


def _compress(pool, page_table, pe, w1, w2, gp, transposed=False):
    b, npg = page_table.shape
    nrows = npg * PAGE
    nb = nrows // CMP_STRIDE
    z = jnp.zeros_like(w1)
    w1bd = jnp.concatenate([jnp.concatenate([w1, z], axis=-1), jnp.concatenate([z, w1], axis=-1)], axis=-2)
    w1pair = w1bd.reshape(2, CMP_BLOCK // 2, 4 * HEAD_DIM, w1bd.shape[-1]).astype(MXU)
    z2 = jnp.zeros_like(w2)
    w2bd = jnp.concatenate([jnp.concatenate([w2, z2], axis=-1), jnp.concatenate([z2, w2], axis=-1)], axis=-2).astype(MXU)
    pe2 = jnp.concatenate([pe, pe], axis=-1).astype(F32)
    if transposed:
        page_spec = lambda w: pl.BlockSpec((1, 2, KVH, HEAD_DIM, PAGE), lambda i, st, pt: (pt[i, st * gp + w], 0, 0, 0, 0))
    else:
        page_spec = lambda w: pl.BlockSpec((1, PAGE, 2 * LANE), lambda i, st, pt: (pt[i, st * gp + w], 0, 0))
    grid_spec = pltpu.PrefetchScalarGridSpec(
        num_scalar_prefetch=1,
        grid=(b, npg // gp),
        in_specs=[page_spec(w) for w in range(gp)]
                 + [pl.BlockSpec(w1pair.shape, lambda i, st, pt: (0, 0, 0, 0)),
                    pl.BlockSpec(pe2.shape, lambda i, st, pt: (0, 0, 0)),
                    pl.BlockSpec(w2bd.shape, lambda i, st, pt: (0, 0, 0))],
        out_specs=pl.BlockSpec((1, nb, 2 * LANE), lambda i, st, pt: (i, 0, 0)),
        scratch_shapes=[pltpu.VMEM((nrows + CMP_BLOCK, LANE), F32), pltpu.VMEM((nrows + CMP_BLOCK, LANE), F32)],
    )
    return pl.pallas_call(
        functools.partial(_compress_body, gp=gp, nrows=nrows, transposed=transposed),
        grid_spec=grid_spec,
        out_shape=jax.ShapeDtypeStruct((b, nb, 2 * LANE), MXU),
        compiler_params=_cparams(("arbitrary", "arbitrary")),
        name="nsa_compress",
    )(page_table, *([pool] * gp), w1pair, pe2, w2bd)


def _fox_prep_body(q_ref, k_ref, v_ref, c_ref, qa_o, ka_o, va_o):
    tp = q_ref.shape[1]
    lane = _iota((tp, LANE), 1)
    q = q_ref[0]
    k = k_ref[0]
    v = v_ref[0]
    c = c_ref[0] * LOG2E
    one = jnp.ones((tp, LANE), F32)
    zero = jnp.zeros((tp, LANE), F32)
    for h in range(FOX_HEADS):
        qs = q[:, (h // 2) * LANE:(h // 2 + 1) * LANE]
        ks = k[:, (h // 2) * LANE:(h // 2 + 1) * LANE]
        vs = v[:, (h // 2) * LANE:(h // 2 + 1) * LANE]
        if h % 2 == 1:
            qs = pltpu.roll(qs, HEAD_DIM, 1)
            ks = pltpu.roll(ks, HEAD_DIM, 1)
            vs = pltpu.roll(vs, HEAD_DIM, 1)
        chi, cmid, clo = _split3(jnp.broadcast_to(c[:, h:h + 1], (tp, LANE)))
        qaug = jnp.where(lane == HEAD_DIM, chi, jnp.where(lane == HEAD_DIM + 1, cmid, jnp.where(
            lane == HEAD_DIM + 2, clo, jnp.where(lane < HEAD_DIM + 6, one, zero))))
        kaug = jnp.where(lane == HEAD_DIM + 3, -chi, jnp.where(lane == HEAD_DIM + 4, -cmid, jnp.where(
            lane == HEAD_DIM + 5, -clo, jnp.where(lane < HEAD_DIM + 3, one, zero))))
        qa_o[0, h] = jnp.where(lane < HEAD_DIM, qs * QSCALE, qaug).astype(qa_o.dtype)
        ka_o[0, h] = jnp.where(lane < HEAD_DIM, ks, kaug).astype(ka_o.dtype)
        va_o[0, h] = jnp.where(lane < HEAD_DIM, vs, jnp.where(lane == HEAD_DIM, one, zero)).astype(va_o.dtype)


def _fox_prep(fq, fkv, c, tp):
    b, t, _ = fq.shape
    hspec = pl.BlockSpec((1, FOX_HEADS, tp, LANE), lambda i, j: (i, 0, j, 0))
    hshape = jax.ShapeDtypeStruct((b, FOX_HEADS, t, LANE), MXU)
    return pl.pallas_call(
        _fox_prep_body,
        grid=(b, t // tp),
        in_specs=[pl.BlockSpec((1, tp, FOX_W), lambda i, j: (i, j, 0)),
                  pl.BlockSpec((1, tp, FOX_W), lambda i, j: (i, j, 0)),
                  pl.BlockSpec((1, tp, FOX_W), lambda i, j: (i, j, 1)),
                  pl.BlockSpec((1, tp, FOX_HEADS), lambda i, j: (i, j, 0))],
        out_specs=[hspec, hspec, hspec],
        out_shape=[hshape, hshape, hshape],
        compiler_params=_cparams(("parallel", "parallel")),
        name="fox_prep",
    )(fq, fkv, fkv, c)


RC = 256
KC = 512
AHEAD = 3


def _flash_chunk(s, m_ref, acc_ref, r0, v):
    rc = s.shape[0]
    rep = s.shape[1] // LANE
    m_old = m_ref[r0:r0 + rc, :]
    m_new = jnp.maximum(m_old, jnp.max(s, axis=1, keepdims=True))
    alpha = jnp.exp2(m_old - m_new)
    p = jnp.exp2(s - jnp.concatenate([m_new] * rep, axis=1))
    acc = acc_ref[r0:r0 + rc, :]
    acc_ref[r0:r0 + rc, :] = jnp.concatenate([alpha] * (acc.shape[1] // LANE), axis=1) * acc + _dot(p.astype(MXU), v)
    m_ref[r0:r0 + rc, :] = m_new


def _pfox_body(q_ref, k_ref, v_ref, o_ref, m_s, acc_s, *, tq, tk, nk):
    qi = pl.program_id(2)
    kj = pl.program_id(3)
    last = ((qi + 1) * tq - 1) // tk

    @pl.when(kj == 0)
    def _():
        m_s[...] = jnp.full_like(m_s, -MASK_BIG)
        acc_s[...] = jnp.zeros_like(acc_s)

    kc = min(KC, tk)

    def update(diag):
        items = []
        for ks in range(tk // kc):
            for hh in range(2):
                for c in range(tq // RC):
                    r0 = c * RC
                    if diag and ks * kc > r0 + RC - 1:
                        continue
                    items.append((hh, r0, ks, diag and ks * kc + kc - 1 > r0))

        def logits(i):
            hh, r0, ks, masked = items[i]
            s = _dot_nt(q_ref[0, hh, r0:r0 + RC, :], k_ref[0, hh, ks * kc:(ks + 1) * kc, :])
            if masked:
                causal = (ks * kc + _iota((RC, kc), 1)) <= (r0 + _iota((RC, kc), 0))
                s = jnp.where(causal, s, -MASK_BIG)
            return s

        def finish(i, s):
            hh, r0, ks, _ = items[i]
            _flash_chunk(s, m_s.at[hh], acc_s.at[hh], r0, v_ref[0, hh, ks * kc:(ks + 1) * kc, :])

        _pipelined(len(items), logits, finish, AHEAD)

    crosses = kj == qi

    @pl.when((kj <= last) & jnp.logical_not(crosses))
    def _():
        update(False)

    @pl.when((kj <= last) & crosses)
    def _():
        update(True)

    @pl.when(kj == nk - 1)
    def _():
        lane = _iota((tq, LANE), 1)
        outs = []
        for hh in range(2):
            acc = acc_s[hh]
            outs.append(acc * (1.0 / acc[:, HEAD_DIM:HEAD_DIM + 1]))
        o_ref[0] = jnp.where(lane < HEAD_DIM, outs[0], pltpu.roll(outs[1], HEAD_DIM, 1))


def _prompt_fox(qa, ka, va, tq, tk):
    assert tq == tk
    b, _, t, _ = qa.shape
    nq, nk = t // tq, t // tk
    hp = FOX_HEADS // 2
    lastf = lambda qi: ((qi + 1) * tq - 1) // tk
    kspec = pl.BlockSpec((1, 2, tk, LANE), lambda i, h, qi, kj: (i, h, jnp.minimum(kj, lastf(qi)), 0))
    return pl.pallas_call(
        functools.partial(_pfox_body, tq=tq, tk=tk, nk=nk),
        grid=(b, hp, nq, nk),
        in_specs=[pl.BlockSpec((1, 2, tq, LANE), lambda i, h, qi, kj: (i, h, qi, 0)), kspec, kspec],
        out_specs=pl.BlockSpec((1, tq, LANE), lambda i, h, qi, kj: (i, qi, h)),
        out_shape=jax.ShapeDtypeStruct((b, t, FOX_W), F32),
        scratch_shapes=[pltpu.VMEM((2, tq, LANE), F32), pltpu.VMEM((2, tq, LANE), F32)],
        compiler_params=_cparams(("parallel", "parallel", "parallel", "arbitrary")),
        name="prompt_fox",
    )(qa, ka, va)


QB = 128
NWIN = WINDOW // QB + 1


def _nsa_prep_body(k_ref, v_ref, e_ref, kcat_o, vcat_o):
    tp = k_ref.shape[1]
    lane = _iota((tp, LANE), 1)
    kcat_o[0] = jnp.concatenate([k_ref[0].astype(MXU), e_ref[...]], axis=1)
    ones_col = jnp.where(lane == 0, 1.0, 0.0).astype(MXU)
    vcat_o[0] = jnp.concatenate([v_ref[0].astype(MXU), ones_col], axis=1)


def _nsa_prep(nkv, e_mat, tp):
    b, t, _ = nkv.shape
    njp = e_mat.shape[1]
    return pl.pallas_call(
        _nsa_prep_body,
        grid=(b, t // tp),
        in_specs=[pl.BlockSpec((1, tp, LANE), lambda i, j: (i, j, 2)),
                  pl.BlockSpec((1, tp, LANE), lambda i, j: (i, j, 3)),
                  pl.BlockSpec((tp, njp), lambda i, j: (j, 0))],
        out_specs=[pl.BlockSpec((1, tp, LANE + njp), lambda i, j: (i, j, 0)),
                   pl.BlockSpec((1, tp, 2 * LANE), lambda i, j: (i, j, 0))],
        out_shape=[jax.ShapeDtypeStruct((b, t, LANE + njp), MXU), jax.ShapeDtypeStruct((b, t, 2 * LANE), MXU)],
        compiler_params=_cparams(("parallel", "parallel")),
        name="nsa_prep",
    )(nkv, nkv, e_mat)


def _pnsa_body(q_ref, g_ref, kc_ref, vc_ref, kcat_ref, vcat_ref, *refs, t_len, tk, nk):
    wk = refs[:NWIN]
    wv = refs[NWIN:2 * NWIN]
    ovt_ref, band_ref, bandc_ref, o_ref, qa_s, oc_s, ow_s, m_s, acc_s = refs[2 * NWIN:]
    qi = pl.program_id(1)
    kj = pl.program_id(2)
    t0 = qi * QB
    last = (t0 + QB - 1) // tk
    njp = ovt_ref.shape[0]
    nc = t_len // CMP_STRIDE
    n_c = (t_len - CMP_BLOCK) // CMP_STRIDE + 1
    nhg = KVH * GRP
    ch = 2 * QB

    @pl.when(kj == 0)
    def _():
        q = q_ref[0]
        lane = _iota((QB, LANE), 1)
        for hg in range(nhg):
            h = hg // GRP
            sl = q[:, (hg // 2) * LANE:(hg // 2 + 1) * LANE]
            if hg % 2 != h:
                sl = pltpu.roll(sl, HEAD_DIM, 1)
            qa_s[hg * QB:(hg + 1) * QB, 0:LANE] = (jnp.where((lane // HEAD_DIM) == h, sl, 0.0) * QSCALE).astype(MXU)

        kcb = kc_ref[0]
        vcb = vc_ref[0]
        shift_t = ((_iota((nc, LANE), 0) == 8 * qi - 9 + _iota((nc, LANE), 1))
                   & (_iota((nc, LANE), 1) < 16)).astype(MXU)
        rhs = jnp.concatenate([kcb, shift_t, shift_t, shift_t], axis=1)
        trow = t0 + _iota((ch, nc), 0) % QB
        ncol = _iota((ch, nc), 1)
        maskc = (CMP_STRIDE * ncol + (CMP_BLOCK - 1) <= trow) & (ncol < n_c)
        pcs = [None] * KVH

        def cmp_logits(c):
            r0 = c * ch
            lhs = jnp.concatenate([qa_s[r0:r0 + ch, 0:LANE]]
                                  + [t.astype(MXU) for t in _split3(bandc_ref[r0:r0 + ch, :])], axis=1)
            return _dot_nt(lhs, rhs)

        def cmp_finish(c, s):
            r0 = c * ch
            pc = _softmax2_rows(jnp.where(maskc, s, NEG_INF))
            oc_s[r0:r0 + ch, :] = _dot(pc.astype(MXU), vcb)
            h = r0 // (GRP * QB)
            part = pc[0:QB] + pc[QB:ch]
            pcs[h] = part if pcs[h] is None else pcs[h] + part

        _pipelined(nhg * QB // ch, cmp_logits, cmp_finish, 2)
        pcsum = jnp.concatenate(pcs, axis=0)
        imp = _dot_nt(ovt_ref[...], pcsum.astype(MXU))
        width = KVH * QB
        jr = _iota((njp, width), 0)
        qblk = (t0 + _iota((njp, width), 1) % QB) // SEL_BLOCK
        forced = (jr == 0) | (jr == qblk) | (jr == qblk - 1)
        imp = jnp.where(forced, jnp.inf, jnp.where(jr > qblk, NEG_INF, imp))
        picked = _top_k_mask(imp, N_SELECT)
        pen_t = jnp.where(picked, 0.0, -MASK_BIG).astype(MXU)
        eye = (_iota((width, width), 0) == _iota((width, width), 1)).astype(MXU)
        pen = _dot_nt(eye, pen_t).astype(MXU)
        for hg in range(nhg):
            h = hg // GRP
            qa_s[hg * QB:(hg + 1) * QB, LANE:LANE + njp] = pen[h * QB:(h + 1) * QB]

        kwin = jnp.concatenate([r[0].astype(MXU) for r in wk], axis=0)
        vwin = jnp.concatenate([r[0].astype(MXU) for r in wv], axis=0)
        ri = _iota((ch, QB), 0) % QB
        ci = _iota((ch, QB), 1)

        def win_logits(c):
            return _dot_nt(qa_s[c * ch:(c + 1) * ch, 0:LANE], kwin)

        def win_finish(c, s):
            r0 = c * ch
            blocks = []
            for w in range(NWIN):
                sw = s[:, w * QB:(w + 1) * QB]
                ok = qi - (NWIN - 1) + w >= 0
                if w == NWIN - 1:
                    sw = jnp.where((ci <= ri) & ok, sw + band_ref[r0:r0 + ch, QB:2 * QB], NEG_INF)
                elif w == NWIN - 2:
                    sw = jnp.where(ok, sw + band_ref[r0:r0 + ch, 0:QB], NEG_INF)
                elif w == 0:
                    sw = jnp.where((ci > ri) & ok, sw, NEG_INF)
                else:
                    sw = jnp.where(ok, sw, NEG_INF)
                blocks.append(sw)
            pw = _softmax2_rows(jnp.concatenate(blocks, axis=1))
            ow_s[r0:r0 + ch, :] = _dot(pw.astype(MXU), vwin)

        _pipelined(nhg * QB // ch, win_logits, win_finish, 2)

        m_s[...] = jnp.full_like(m_s, -MASK_BIG)
        acc_s[...] = jnp.zeros_like(acc_s)

    kc = min(KC, tk)
    nsub = tk // kc

    nrc = nhg * QB // RC

    def step(near):
        def logits(i):
            ks, r0 = i // nrc, (i % nrc) * RC
            s = _dot_nt(qa_s[r0:r0 + RC, :], kcat_ref[0, ks * kc:(ks + 1) * kc, :])
            if near:
                trow = t0 + (r0 + _iota((RC, kc), 0)) % QB
                scol = kj * tk + ks * kc + _iota((RC, kc), 1)
                chunks = []
                for c in range(kc // QB):
                    delta = qi - ((kj * tk + ks * kc) // QB + c)
                    chunks.append(jnp.where(delta == 0, band_ref[r0:r0 + RC, QB:2 * QB],
                                            jnp.where(delta == 1, band_ref[r0:r0 + RC, 0:QB], 0.0)))
                s = jnp.where(scol <= trow, s + jnp.concatenate(chunks, axis=1), -MASK_BIG)
            return s

        def finish(i, s):
            ks, r0 = i // nrc, (i % nrc) * RC
            _flash_chunk(s, m_s, acc_s, r0, vcat_ref[0, ks * kc:(ks + 1) * kc, :])

        _pipelined(nsub * nrc, logits, finish, AHEAD)

    near_from = (t0 - QB) // tk

    @pl.when(kj < near_from)
    def _():
        step(False)

    @pl.when((kj >= near_from) & (kj <= last))
    def _():
        step(True)

    @pl.when(kj == nk - 1)
    def _():
        gt = g_ref[0]
        lane = _iota((QB, LANE), 1)
        outs = []
        for hg in range(nhg):
            r0 = hg * QB
            acc = acc_s[r0:r0 + QB, :]
            osl = acc[:, 0:LANE] * (1.0 / acc[:, LANE:LANE + 1])
            off = FOX_HEADS + hg * 3
            outs.append(gt[:, off:off + 1] * oc_s[r0:r0 + QB, :] + gt[:, off + 1:off + 2] * osl
                        + gt[:, off + 2:off + 3] * ow_s[r0:r0 + QB, :])
        for pr in range(nhg // 2):
            h = (2 * pr) // GRP
            ev, od = outs[2 * pr], outs[2 * pr + 1]
            if h == 0:
                od = pltpu.roll(od, HEAD_DIM, 1)
            else:
                ev = pltpu.roll(ev, HEAD_DIM, 1)
            o_ref[0, :, pr * LANE:(pr + 1) * LANE] = jnp.where(lane < HEAD_DIM, ev, od)


def _prompt_nsa(nq, ng, cmp2, kcat, vcat, kw, ovt, band, bandc, tk):
    b, t, _ = nq.shape
    nqb, nk = t // QB, t // tk
    nc = cmp2.shape[1]
    njp = ovt.shape[0]
    rows = KVH * GRP * QB
    lastf = lambda qi: (qi * QB + QB - 1) // tk
    kjc = lambda qi, kj: jnp.minimum(kj, lastf(qi))
    wspec = lambda w, part: pl.BlockSpec((1, QB, LANE), lambda i, qi, kj: (i, jnp.maximum(qi - (NWIN - 1) + w, 0), part))
    const = lambda a: pl.BlockSpec(a.shape, lambda i, qi, kj: (0,) * a.ndim)
    return pl.pallas_call(
        functools.partial(_pnsa_body, t_len=t, tk=tk, nk=nk),
        grid=(b, nqb, nk),
        in_specs=[pl.BlockSpec((1, QB, NSA_Q_W), lambda i, qi, kj: (i, qi, 0)),
                  pl.BlockSpec((1, QB, LANE), lambda i, qi, kj: (i, qi, 0)),
                  pl.BlockSpec((1, nc, LANE), lambda i, qi, kj: (i, 0, 0)),
                  pl.BlockSpec((1, nc, LANE), lambda i, qi, kj: (i, 0, 1)),
                  pl.BlockSpec((1, tk, LANE + njp), lambda i, qi, kj: (i, kjc(qi, kj), 0)),
                  pl.BlockSpec((1, tk, 2 * LANE), lambda i, qi, kj: (i, kjc(qi, kj), 0))]
                 + [wspec(w, 0) for w in range(NWIN)] + [wspec(w, 1) for w in range(NWIN)]
                 + [const(ovt), const(band), const(bandc)],
        out_specs=pl.BlockSpec((1, QB, NSA_Q_W), lambda i, qi, kj: (i, qi, 0)),
        out_shape=jax.ShapeDtypeStruct((b, t, NSA_Q_W), F32),
        scratch_shapes=[pltpu.VMEM((rows, LANE + njp), MXU), pltpu.VMEM((rows, LANE), F32), pltpu.VMEM((rows, LANE), F32),
                        pltpu.VMEM((rows, LANE), F32), pltpu.VMEM((rows, 2 * LANE), F32)],
        compiler_params=_cparams(("parallel", "parallel", "arbitrary")),
        name="prompt_nsa",
    )(nq, ng, cmp2, cmp2, kcat, vcat, *([kw] * (2 * NWIN)), ovt, band, bandc)


def _dfox_body(pt_ref, *refs, gp, ns, s_new):
    kpages = refs[:gp]
    vpages = refs[gp:2 * gp]
    q_ref, cq_ref, ck_ref, ckn_ref, new_ref, o_ref, m_s, l_s, acc_s = refs[2 * gp:]
    st = pl.program_id(1)
    rows = FOX_HEADS * s_new
    qb = (q_ref[0] * QSCALE).astype(MXU)

    @pl.when(st == 0)
    def _():
        m_s[...] = jnp.full_like(m_s, NEG_INF)
        l_s[...] = jnp.zeros_like(l_s)
        acc_s[...] = jnp.zeros_like(acc_s)

    def flat(x):
        return x.reshape(FOX_W, PAGE).astype(MXU)

    @pl.when(st < ns)
    def _():
        s = jnp.concatenate([_dot(qb, flat(kp[0, 0])) for kp in kpages], axis=1)
        ck = jnp.concatenate([ck_ref[0]] * s_new, axis=0)
        lg = s + (cq_ref[0] - ck) * LOG2E
        alpha, p = _online_update(lg, m_s, l_s)
        pv = jnp.zeros((rows, FOX_W), F32)
        for w in range(gp):
            pv = pv + _dot_nt(p[:, w * PAGE:(w + 1) * PAGE].astype(MXU), flat(vpages[w][0, 0]))
        acc_s[...] = alpha * acc_s[...] + pv

    @pl.when(st == ns)
    def _():
        s = _dot(qb, flat(new_ref[0, 0]))
        ck = jnp.concatenate([ckn_ref[0]] * s_new, axis=0)
        qrow = _iota((rows, PAGE), 0) // FOX_HEADS
        kcol = _iota((rows, PAGE), 1)
        lg = jnp.where(kcol <= qrow, s + (cq_ref[0] - ck) * LOG2E, NEG_INF)
        alpha, p = _online_update(lg, m_s, l_s)
        acc = alpha * acc_s[...] + _dot_nt(p.astype(MXU), flat(new_ref[0, 1]))
        o = acc * (1.0 / l_s[...])
        own = (_iota((rows, FOX_W), 1) // HEAD_DIM) == (_iota((rows, FOX_W), 0) % FOX_HEADS)
        o = jnp.where(own, o, 0.0)
        o_ref[0] = jnp.sum(o.reshape(s_new, FOX_HEADS, FOX_W), axis=1)


def _decode_fox(cache_t, page_table, qrows, cq, ct_all, new_t, gp):
    b, npg = page_table.shape
    ns = npg // gp
    rows = qrows.shape[1]
    s_new = rows // FOX_HEADS
    stc = lambda st: jnp.minimum(st, ns - 1)
    page_spec = lambda w, part: pl.BlockSpec((1, 1, FOX_HEADS, HEAD_DIM, PAGE),
                                             lambda i, st, pt: (pt[i, stc(st) * gp + w], part, 0, 0, 0))
    grid_spec = pltpu.PrefetchScalarGridSpec(
        num_scalar_prefetch=1,
        grid=(b, ns + 1),
        in_specs=[page_spec(w, 0) for w in range(gp)] + [page_spec(w, 1) for w in range(gp)]
                 + [pl.BlockSpec((1, rows, FOX_W), lambda i, st, pt: (i, 0, 0)),
                    pl.BlockSpec((1, rows, 1), lambda i, st, pt: (i, 0, 0)),
                    pl.BlockSpec((1, FOX_HEADS, gp * PAGE), lambda i, st, pt: (i, 0, stc(st))),
                    pl.BlockSpec((1, FOX_HEADS, PAGE), lambda i, st, pt: (i, 0, npg)),
                    pl.BlockSpec((1, 2, FOX_HEADS, HEAD_DIM, PAGE), lambda i, st, pt: (i, 0, 0, 0, 0))],
        out_specs=pl.BlockSpec((1, s_new, FOX_W), lambda i, st, pt: (i, 0, 0)),
        scratch_shapes=[pltpu.VMEM((rows, 1), F32), pltpu.VMEM((rows, 1), F32), pltpu.VMEM((rows, FOX_W), F32)],
    )
    return pl.pallas_call(
        functools.partial(_dfox_body, gp=gp, ns=ns, s_new=s_new),
        grid_spec=grid_spec,
        out_shape=jax.ShapeDtypeStruct((b, s_new, FOX_W), F32),
        compiler_params=_cparams(("parallel", "arbitrary")),
        name="decode_fox",
    )(page_table, *([cache_t] * (2 * gp)), qrows, cq, ct_all, ct_all, new_t)


def _dnsa_body(pt_ref, *refs, gp, ns, nj, njp):
    kpages = refs[:gp]
    vpages = refs[gp:2 * gp]
    (q_ref, g_ref, kc_ref, vc_ref, ovt_ref, bcmp_ref, bwin_ref, blast_ref, bnew_ref,
     win_ref, wnew_ref, snew_ref, o_ref, sel_s, oc_s, ow_s, m_s, l_s, acc_s) = refs[2 * gp:]
    st = pl.program_id(1)
    rows = q_ref.shape[1]
    rq = rows // GRP
    qb = (q_ref[0] * QSCALE).astype(MXU)

    @pl.when(st == 0)
    def _():
        kc = kc_ref[0].astype(MXU)
        vc = vc_ref[0].astype(MXU)
        pc = _softmax2_rows(_dot_nt(qb, kc) + bcmp_ref[...])
        oc_s[...] = _dot(pc.astype(MXU), vc)
        pcs = pc[0:rq]
        for g in range(1, GRP):
            pcs = pcs + pc[g * rq:(g + 1) * rq]
        pcs = jnp.concatenate([pcs, jnp.zeros((LANE - rq, pcs.shape[1]), F32)], axis=0)
        imp = _dot_nt(ovt_ref[...], pcs.astype(MXU))
        jr = _iota((njp, LANE), 0)
        qblk = ((nj - 1) * SEL_BLOCK + _iota((njp, LANE), 1) // KVH) // SEL_BLOCK
        forced = (jr == 0) | (jr == qblk) | (jr == qblk - 1)
        imp = jnp.where(forced, jnp.inf, jnp.where(jr > qblk, NEG_INF, imp))
        sel_t = _top_k_mask(imp, N_SELECT).astype(BF16)
        eye = (_iota((LANE, LANE), 0) == _iota((LANE, LANE), 1)).astype(BF16)
        sel = _dot_nt(eye, sel_t).astype(BF16)
        sel_s[...] = jnp.concatenate([sel[0:rq]] * GRP, axis=0)

        flat = lambda x: x.reshape(LANE, x.shape[-1]).astype(MXU)
        bw = bwin_ref[...]
        sw = jnp.concatenate([_dot(qb, flat(win_ref[0, 0])) + bw[:, 0:WINDOW],
                              _dot(qb, flat(wnew_ref[0, 0])) + bw[:, WINDOW:WINDOW + PAGE]], axis=1)
        pw = _softmax2_rows(sw)
        ow_s[...] = (_dot_nt(pw[:, 0:WINDOW].astype(MXU), flat(win_ref[0, 1]))
                     + _dot_nt(pw[:, WINDOW:WINDOW + PAGE].astype(MXU), flat(wnew_ref[0, 1])))

        m_s[...] = jnp.full_like(m_s, NEG_INF)
        l_s[...] = jnp.zeros_like(l_s)
        acc_s[...] = jnp.zeros_like(acc_s)

    @pl.when(st < ns)
    def _():
        flat = lambda x: x.reshape(LANE, PAGE).astype(MXU)
        s = jnp.concatenate([_dot(qb, flat(kp[0, 0])) for kp in kpages], axis=1)
        width = gp * PAGE
        expand = (_iota((njp, width), 0) == (st * width + _iota((njp, width), 1)) // SEL_BLOCK).astype(BF16)
        mexp = _dot(sel_s[...], expand)
        tail = jnp.where(st == ns - 1, blast_ref[...], 0.0)
        if gp > 1:
            tail = jnp.concatenate([jnp.zeros((rows, width - PAGE), F32), tail], axis=1)
        lg = jnp.where(mexp > 0.5, s + tail, NEG_INF)
        alpha, p = _online_update(lg, m_s, l_s)
        pv = jnp.zeros((rows, LANE), F32)
        for w in range(gp):
            pv = pv + _dot_nt(p[:, w * PAGE:(w + 1) * PAGE].astype(MXU), flat(vpages[w][0, 0]))
        acc_s[...] = alpha * acc_s[...] + pv

    @pl.when(st == ns)
    def _():
        flat = lambda x: x.reshape(LANE, PAGE).astype(MXU)
        s = _dot(qb, flat(snew_ref[0, 0])) + bnew_ref[...]
        picked = sel_s[:, nj - 1:nj].astype(F32) > 0.5
        lg = jnp.where(picked, s, NEG_INF)
        alpha, p = _online_update(lg, m_s, l_s)
        acc = alpha * acc_s[...] + _dot_nt(p.astype(MXU), flat(snew_ref[0, 1]))
        osl = acc * (1.0 / jnp.maximum(l_s[...], 1e-30))
        gt = g_ref[0]
        of = gt[:, 0:1] * oc_s[...] + gt[:, 1:2] * osl + gt[:, 2:3] * ow_s[...]
        head = _iota((rows, LANE), 0) % KVH
        of = jnp.where(head == 0, of, pltpu.roll(of, HEAD_DIM, 1))
        o_ref[0] = of[:, 0:HEAD_DIM]


def _decode_nsa(cache_t, page_table, qrows, gates, cmp2, ovt, bcmp, bwin, blast, bnew, win_t, wnew_t, snew_t, gp, nj):
    b, npg = page_table.shape
    ns = npg // gp
    rows = qrows.shape[1]
    ncp = cmp2.shape[1]
    njp = ovt.shape[0]
    stc = lambda st: jnp.minimum(st, ns - 1)
    page_spec = lambda w, typ: pl.BlockSpec((1, 1, KVH, HEAD_DIM, PAGE),
                                            lambda i, st, pt: (pt[i, stc(st) * gp + w], typ, 0, 0, 0))
    const2 = lambda a: pl.BlockSpec(a.shape, lambda i, st, pt: (0, 0))
    own5 = lambda a: pl.BlockSpec((1,) + a.shape[1:], lambda i, st, pt: (i, 0, 0, 0, 0))
    grid_spec = pltpu.PrefetchScalarGridSpec(
        num_scalar_prefetch=1,
        grid=(b, ns + 1),
        in_specs=[page_spec(w, 2) for w in range(gp)] + [page_spec(w, 3) for w in range(gp)]
                 + [pl.BlockSpec((1, rows, LANE), lambda i, st, pt: (i, 0, 0)),
                    pl.BlockSpec((1, rows, 3), lambda i, st, pt: (i, 0, 0)),
                    pl.BlockSpec((1, ncp, LANE), lambda i, st, pt: (i, 0, 0)),
                    pl.BlockSpec((1, ncp, LANE), lambda i, st, pt: (i, 0, 1)),
                    const2(ovt), const2(bcmp), const2(bwin), const2(blast), const2(bnew),
                    own5(win_t), own5(wnew_t), own5(snew_t)],
        out_specs=pl.BlockSpec((1, rows, HEAD_DIM), lambda i, st, pt: (i, 0, 0)),
        scratch_shapes=[pltpu.VMEM((rows, njp), BF16), pltpu.VMEM((rows, LANE), F32), pltpu.VMEM((rows, LANE), F32),
                        pltpu.VMEM((rows, 1), F32), pltpu.VMEM((rows, 1), F32), pltpu.VMEM((rows, LANE), F32)],
    )
    return pl.pallas_call(
        functools.partial(_dnsa_body, gp=gp, ns=ns, nj=nj, njp=njp),
        grid_spec=grid_spec,
        out_shape=jax.ShapeDtypeStruct((b, rows, HEAD_DIM), F32),
        compiler_params=_cparams(("parallel", "arbitrary")),
        name="decode_nsa",
    )(page_table, *([cache_t] * (2 * gp)), qrows, gates, cmp2, cmp2, ovt, bcmp, bwin, blast, bnew, win_t, wnew_t, snew_t)


def _merge_body(on_ref, of_ref, ga_ref, gb_ref, x_ref, wa, wb, wo, g1, b1, o_ref, *, alpha):
    a = _dot(on_ref[...].astype(MXU), wa[...])
    bb = _dot(of_ref[...].astype(MXU), wb[...])
    mix = _dot((ga_ref[...] * a + gb_ref[...] * bb).astype(MXU), wo[...])
    o_ref[...] = _layer_norm(alpha * x_ref[...] + mix, g1[...], b1[...])


def _merge(o_nsa, o_fox, sga, sgb, x2d, wa, wb, wo, g1, b1, tm, alpha):
    m, d = x2d.shape
    row = lambda n: pl.BlockSpec((tm, n), lambda i: (i, 0))
    full = lambda a: pl.BlockSpec(a.shape, lambda i: (0, 0))
    wa, wb, wo = wa.astype(MXU), wb.astype(MXU), wo.astype(MXU)
    g1, b1 = g1.reshape(1, d), b1.reshape(1, d)
    return pl.pallas_call(
        functools.partial(_merge_body, alpha=alpha),
        grid=(m // tm,),
        in_specs=[row(NSA_Q_W), row(FOX_W), row(d), row(d), row(d), full(wa), full(wb), full(wo), full(g1), full(b1)],
        out_specs=row(d),
        out_shape=jax.ShapeDtypeStruct((m, d), F32),
        compiler_params=_cparams(("parallel",)),
        name="merge_ln1",
    )(o_nsa, o_fox, sga, sgb, x2d, wa, wb, wo, g1, b1)


def _ffn_up_long_body(x_ref, prev_ref, wu, wg, cw, cb, h_ref, st_ref, carry_s, *, tiles_per_seq):
    i = pl.program_id(0)
    tm = x_ref.shape[0]
    xb = x_ref[...].astype(MXU)
    u = _dot(xb, wu[...])

    @pl.when(i % tiles_per_seq == 0)
    def _():
        carry_s[...] = prev_ref[0]

    r = _iota(u.shape, 0)
    um1 = jnp.where(r >= 1, pltpu.roll(u, 1, 0), carry_s[1:2, :])
    um2 = jnp.where(r >= 2, pltpu.roll(u, 2, 0), jnp.where(r == 0, carry_s[0:1, :], carry_s[1:2, :]))
    conv = cb[...] + cw[0:1, :] * um2 + cw[1:2, :] * um1 + cw[2:3, :] * u
    h_ref[...] = (jax.nn.gelu(conv) * _dot(xb, wg[...])).astype(h_ref.dtype)
    last2 = u[tm - 2:tm, :]
    carry_s[...] = last2
    st_ref[0] = last2


def _ffn_up_short_body(x_ref, p1_ref, p2_ref, wu, wg, cw, cb, h_ref, u_ref, *, seq):
    xb = x_ref[...].astype(MXU)
    u = _dot(xb, wu[...])
    t = _iota(u.shape, 0) % seq
    um1 = jnp.where(t >= 1, pltpu.roll(u, 1, 0), p1_ref[...])
    um2 = jnp.where(t >= 2, pltpu.roll(u, 2, 0), p2_ref[...])
    conv = cb[...] + cw[0:1, :] * um2 + cw[1:2, :] * um1 + cw[2:3, :] * u
    h_ref[...] = (jax.nn.gelu(conv) * _dot(xb, wg[...])).astype(h_ref.dtype)
    u_ref[...] = u


def _ffn_up_long(x1, conv_prev, wu, wg, cw, cb, seq, tm):
    m, d = x1.shape
    f = wu.shape[1]
    nseq = m // seq
    tps = seq // tm
    wu, wg = wu.astype(MXU), wg.astype(MXU)
    cb = cb.reshape(1, f)
    full = lambda a: pl.BlockSpec(a.shape, lambda i: (0, 0))
    return pl.pallas_call(
        functools.partial(_ffn_up_long_body, tiles_per_seq=tps),
        grid=(m // tm,),
        in_specs=[pl.BlockSpec((tm, d), lambda i: (i, 0)),
                  pl.BlockSpec((1, 2, f), lambda i: (i // tps, 0, 0)),
                  full(wu), full(wg), full(cw), full(cb)],
        out_specs=[pl.BlockSpec((tm, f), lambda i: (i, 0)),
                   pl.BlockSpec((1, 2, f), lambda i: (i // tps, 0, 0))],
        out_shape=[jax.ShapeDtypeStruct((m, f), MXU), jax.ShapeDtypeStruct((nseq, 2, f), F32)],
        scratch_shapes=[pltpu.VMEM((2, f), F32)],
        compiler_params=_cparams(("arbitrary",)),
        name="ffn_up_long",
    )(x1, conv_prev, wu, wg, cw, cb)


def _ffn_up_short(x1, conv_prev, wu, wg, cw, cb, seq):
    m, d = x1.shape
    f = wu.shape[1]
    nseq = m // seq
    wu, wg = wu.astype(MXU), wg.astype(MXU)
    cb = cb.reshape(1, f)
    zeros = jnp.zeros((nseq, seq, f), F32)
    p1 = zeros.at[:, 0].set(conv_prev[:, 1]).reshape(m, f)
    p2 = zeros.at[:, 0].set(conv_prev[:, 0]).at[:, 1].set(conv_prev[:, 1]).reshape(m, f)
    full = lambda a: pl.BlockSpec(a.shape, lambda i: (0, 0))
    h, u = pl.pallas_call(
        functools.partial(_ffn_up_short_body, seq=seq),
        grid=(1,),
        in_specs=[full(x1), full(p1), full(p2), full(wu), full(wg), full(cw), full(cb)],
        out_specs=[pl.BlockSpec((m, f), lambda i: (0, 0)), pl.BlockSpec((m, f), lambda i: (0, 0))],
        out_shape=[jax.ShapeDtypeStruct((m, f), MXU), jax.ShapeDtypeStruct((m, f), F32)],
        compiler_params=_cparams(("arbitrary",)),
        name="ffn_up_short",
    )(x1, p1, p2, wu, wg, cw, cb)
    return h, u.reshape(nseq, seq, f)[:, seq - 2:]


def _ffn_down_body(h_ref, x1_ref, p_ref, wd, wpg, wp, g2, b2, o_ref, *, alpha):
    f = _dot(h_ref[...], wd[...])
    x2 = _layer_norm(alpha * x1_ref[...] + f, g2[...], b2[...])
    gate = jax.nn.sigmoid(_dot(x2.astype(MXU), wpg[...]))
    o_ref[...] = x2 + gate * _dot(p_ref[...].astype(MXU), wp[...])


def _ffn_down(h, x1, p2d, wd, wpg, wp, g2, b2, tm, alpha):
    m, d = x1.shape
    row = lambda n: pl.BlockSpec((tm, n), lambda i: (i, 0))
    full = lambda a: pl.BlockSpec(a.shape, lambda i: (0, 0))
    wd, wpg, wp = wd.astype(MXU), wpg.astype(MXU), wp.astype(MXU)
    g2, b2 = g2.reshape(1, d), b2.reshape(1, d)
    return pl.pallas_call(
        functools.partial(_ffn_down_body, alpha=alpha),
        grid=(m // tm,),
        in_specs=[row(h.shape[1]), row(d), row(p2d.shape[1]), full(wd), full(wpg), full(wp), full(g2), full(b2)],
        out_specs=row(d),
        out_shape=jax.ShapeDtypeStruct((m, d), F32),
        compiler_params=_cparams(("parallel",)),
        name="ffn_down_ln2_ple",
    )(h, x1, p2d, wd, wpg, wp, g2, b2)


def _overlap_t(n_sel, n_sel_pad, n_c, n_c_pad):
    start = np.arange(n_c_pad)[None, :] * CMP_STRIDE
    j = np.arange(n_sel_pad)[:, None]
    ov = (start < (j + 1) * SEL_BLOCK) & (start + CMP_BLOCK > j * SEL_BLOCK)
    ov &= (np.arange(n_c_pad)[None, :] < n_c) & (j < n_sel)
    return jnp.asarray(ov, dtype=MXU)


def _block_onehot(t_len, njp):
    e = (np.arange(t_len)[:, None] // SEL_BLOCK) == np.arange(njp)[None, :]
    return jnp.asarray(e, dtype=MXU)


def _prompt_bias_idx():
    nhg = KVH * GRP
    hg = np.repeat(np.arange(nhg), QB)[:, None]
    i = np.tile(np.arange(QB), nhg)[:, None]
    c = np.arange(2 * QB)[None, :]
    band = _bias_idx(i + QB - c, hg, np.ones((nhg * QB, 2 * QB), bool))
    band = np.where(band < 0, -2, band)
    mcol = np.arange(LANE)[None, :]
    dc = i + 113 - CMP_STRIDE * mcol
    bandc = _bias_idx(dc, hg, mcol < 16)
    bandc = np.where(bandc < 0, -2, bandc)
    return band, bandc


def _decode_bias_idx(p_len, s_new, n_c, ncp):
    rows = GRP * s_new * KVH
    r = np.arange(rows)
    g, i, h = r // (s_new * KVH), (r // KVH) % s_new, r % KVH
    hg = (h * GRP + g)[:, None]
    qpos = (p_len + i)[:, None]
    ones = np.ones((rows, 1), bool)
    n = np.arange(ncp)
    bcmp = _bias_idx(qpos - (CMP_STRIDE * n + CMP_BLOCK - 1)[None, :], hg, (n < n_c)[None, :] & ones)
    wb = min(WINDOW, p_len)
    wpos = np.concatenate([p_len - wb + np.arange(wb), p_len + np.arange(PAGE)])
    wok = np.concatenate([np.ones(wb, bool), np.arange(PAGE) < s_new])[None, :]
    dwin = qpos - wpos[None, :]
    bwin = _bias_idx(dwin, hg, wok & (dwin < WINDOW))
    blast = _bias_idx(qpos - (p_len - PAGE + np.arange(PAGE))[None, :], hg, np.ones((rows, PAGE), bool))
    bnew = _bias_idx(qpos - (p_len + np.arange(PAGE))[None, :], hg, (np.arange(PAGE) < s_new)[None, :] & ones)
    return bcmp, bwin, blast, bnew


def _pick(n, prefs):
    for p in prefs:
        if n % p == 0:
            return p
    return n


def _finish(x2d, o_nsa, o_fox, sga, sgb, p2d, conv_prev, seq, W, alpha):
    m = x2d.shape[0]
    tm = _pick(m, (256, 128))
    x1 = _merge(o_nsa, o_fox, sga, sgb, x2d, W['w_branch_a'], W['w_branch_b'], W['w_out'], W['ln1_g'], W['ln1_b'],
                tm, alpha)
    if seq >= 128:
        h, conv_state = _ffn_up_long(x1, conv_prev, W['w_ffn_up'], W['w_ffn_gate'], W['ffn_conv_w'], W['ffn_conv_b'],
                                     seq, _pick(seq, (256, 128)))
    else:
        h, conv_state = _ffn_up_short(x1, conv_prev, W['w_ffn_up'], W['w_ffn_gate'], W['ffn_conv_w'],
                                      W['ffn_conv_b'], seq)
    y = _ffn_down(h, x1, p2d, W['w_ffn_down'], W['w_ple_gate'], W['w_ple'], W['ln2_g'], W['ln2_b'], tm, alpha)
    return y, conv_state


def _prompt_layer(x, p_emb, W, rel_bias, alpha):
    b, t, d = x.shape
    m = b * t
    x2d = x.reshape(m, d)
    nq, nkv, kw, fq, fkv, sga, sgb, ng, logf = _proj(x2d, W['w_in'], W['b_forget'], _pick(m, (256, 128)))
    npg = t // PAGE
    gp = _pick(npg, (8, 4, 2, 1))
    ident = jnp.arange(b * npg, dtype=jnp.int32).reshape(b, npg)
    tile = _pick(t, (1024, 512, 256, 128))

    c, _ = _paged_cumsum(logf.reshape(b * npg, PAGE, FOX_HEADS), ident, gp)
    fq3, fkv3 = fq.reshape(b, t, FOX_W), fkv.reshape(b, t, 2 * FOX_W)
    qa, ka, va = _fox_prep(fq3, fkv3, c, tile)
    o_fox = _prompt_fox(qa, ka, va, tile, tile)

    nkv3 = nkv.reshape(b, t, 4 * NSA_KV_W)
    cmp2 = _compress(nkv3.reshape(b * npg, PAGE, 4 * NSA_KV_W), ident, W['nsa_cmp_pe'], W['nsa_cmp_w1'],
                     W['nsa_cmp_w2'], gp)
    nc = cmp2.shape[1]
    n_c = (t - CMP_BLOCK) // CMP_STRIDE + 1
    nj = t // SEL_BLOCK
    njp = -(-nj // LANE) * LANE
    band_idx, bandc_idx = _prompt_bias_idx()
    kcat, vcat = _nsa_prep(nkv3, _block_onehot(t, njp), tile)
    o_nsa = _prompt_nsa(nq.reshape(b, t, NSA_Q_W), ng.reshape(b, t, LANE), cmp2, kcat, vcat,
                        kw.reshape(b, t, 2 * NSA_KV_W), _overlap_t(nj, njp, n_c, nc),
                        _bias_lookup(rel_bias, band_idx), _bias_lookup(rel_bias, bandc_idx), tile)

    conv_prev = jnp.zeros((b, 2, W['w_ffn_up'].shape[1]), F32)
    y, conv_state = _finish(x2d, o_nsa.reshape(m, NSA_Q_W), o_fox.reshape(m, FOX_W), sga, sgb,
                            p_emb.reshape(m, -1), conv_prev, t, W, alpha)
    wb = min(WINDOW, t)
    state = (fkv.reshape(b, t, 2, FOX_HEADS, HEAD_DIM), logf.reshape(b, t, FOX_HEADS),
             nkv.reshape(b, t, 4, KVH, HEAD_DIM), kw.reshape(b, t, 2, KVH, HEAD_DIM)[:, t - wb:], conv_state)
    return y.reshape(b, t, d), state


def _sample_layer(x, p_emb, c_fox_kv, c_fox_logf, c_nsa, win, conv_prev, page_table, W, rel_bias, alpha):
    b, s, d = x.shape
    m = b * s
    npg = page_table.shape[1]
    p_len = npg * PAGE
    x2d = x.reshape(m, d)
    nq, nkv, kw, fq, fkv, sga, sgb, ng, logf = _proj(x2d, W['w_in'], W['b_forget'], _pick(m, (256, 128, 8)))
    gp = _pick(npg, (8, 4, 2, 1))

    def rows_last(a, lead):
        a = jnp.moveaxis(a.reshape((b, s) + lead), 1, -1)
        return jnp.pad(a, ((0, 0),) * (a.ndim - 1) + ((0, PAGE - s),))

    fox_t = jnp.transpose(c_fox_kv, (0, 2, 3, 4, 1))
    nsa_t = jnp.transpose(c_nsa, (0, 2, 3, 4, 1))
    logf_t = jnp.transpose(c_fox_logf.astype(F32), (0, 2, 1))
    win_t = jnp.transpose(win, (0, 2, 3, 4, 1))

    ct_all = _paged_cumsum_t(logf_t, page_table, gp, rows_last(logf, (FOX_HEADS,)))
    cq = jnp.transpose(ct_all[:, :, p_len:p_len + s], (0, 2, 1)).reshape(b, s * FOX_HEADS, 1)
    own = (np.arange(FOX_W)[None, :] // HEAD_DIM) == (np.arange(s * FOX_HEADS)[:, None] % FOX_HEADS)
    qrows_f = jnp.where(jnp.asarray(own)[None], jnp.repeat(fq.reshape(b, s, FOX_W), FOX_HEADS, axis=1), 0.0)
    o_fox = _decode_fox(fox_t, page_table, qrows_f, cq, ct_all, rows_last(fkv, (2, FOX_HEADS, HEAD_DIM)), gp)

    cmp2 = _compress(nsa_t, page_table, W['nsa_cmp_pe'], W['nsa_cmp_w1'], W['nsa_cmp_w2'], gp, transposed=True)
    ncp = cmp2.shape[1]
    n_c = (p_len + s - CMP_BLOCK) // CMP_STRIDE + 1
    nj = -(-(p_len + s) // SEL_BLOCK)
    njp = -(-nj // LANE) * LANE
    tables = [_bias_lookup(rel_bias, ix) for ix in _decode_bias_idx(p_len, s, n_c, ncp)]
    rows = GRP * s * KVH
    h_of_row = np.arange(rows) % KVH
    qsel = jnp.transpose(nq.reshape(b, s, KVH, GRP, HEAD_DIM), (0, 3, 1, 2, 4)).reshape(b, rows, HEAD_DIM)
    half = jnp.asarray((np.arange(LANE)[None, :] // HEAD_DIM) == h_of_row[:, None])
    qrows_n = jnp.where(half[None], jnp.concatenate([qsel, qsel], axis=-1), 0.0)
    gates = jnp.transpose(ng[:, FOX_HEADS:FOX_HEADS + KVH * GRP * 3].reshape(b, s, KVH, GRP, 3),
                          (0, 3, 1, 2, 4)).reshape(b, rows, 3)
    nkv_t = rows_last(nkv, (4, KVH, HEAD_DIM))
    o_rows = _decode_nsa(nsa_t, page_table, qrows_n, gates, cmp2, _overlap_t(nj, njp, n_c, ncp), *tables,
                         win_t, rows_last(kw, (2, KVH, HEAD_DIM)), nkv_t[:, 2:4], gp, nj)
    o_nsa = jnp.transpose(o_rows.reshape(b, GRP, s, KVH, HEAD_DIM), (0, 2, 3, 1, 4)).reshape(m, NSA_Q_W)

    y, conv_state = _finish(x2d, o_nsa, o_fox.reshape(m, FOX_W), sga, sgb, p_emb.reshape(m, -1), conv_prev, s, W,
                            alpha)
    win_all = jnp.concatenate([win, kw.reshape(b, s, 2, KVH, HEAD_DIM)], axis=1)
    state = (fkv.reshape(b, s, 2, FOX_HEADS, HEAD_DIM), logf.reshape(b, s, FOX_HEADS),
             nkv.reshape(b, s, 4, KVH, HEAD_DIM), win_all[:, s:], conv_state)
    return y.reshape(b, s, d), state


def kernel(x_prompt, x_sample, p_prompt, p_sample, cache_fox_kv, cache_fox_logf, cache_nsa_kv, state_nsa_win,
           state_ffn_conv, page_table, w_in, b_forget, nsa_cmp_pe, nsa_cmp_w1, nsa_cmp_w2, rel_bias, w_branch_a,
           w_branch_b, w_out, ln1_g, ln1_b, ln2_g, ln2_b, w_ffn_up, w_ffn_gate, ffn_conv_w, ffn_conv_b, w_ffn_down,
           w_ple, w_ple_gate):
    depth = w_in.shape[0]
    alpha = (2.0 * depth) ** 0.25
    xp, xs = x_prompt, x_sample
    st_p, st_s = [], []
    for i in range(depth):
        W = {
            'w_in': w_in[i], 'b_forget': b_forget[i], 'nsa_cmp_pe': nsa_cmp_pe[i], 'nsa_cmp_w1': nsa_cmp_w1[i],
            'nsa_cmp_w2': nsa_cmp_w2[i], 'w_branch_a': w_branch_a[i], 'w_branch_b': w_branch_b[i], 'w_out': w_out[i],
            'ln1_g': ln1_g[i], 'ln1_b': ln1_b[i], 'ln2_g': ln2_g[i], 'ln2_b': ln2_b[i], 'w_ffn_up': w_ffn_up[i],
            'w_ffn_gate': w_ffn_gate[i], 'ffn_conv_w': ffn_conv_w[i], 'ffn_conv_b': ffn_conv_b[i],
            'w_ffn_down': w_ffn_down[i], 'w_ple': w_ple[i], 'w_ple_gate': w_ple_gate[i],
        }
        xp, sp = _prompt_layer(xp, p_prompt[i], W, rel_bias, alpha)
        xs, ss = _sample_layer(xs, p_sample[i], cache_fox_kv[i], cache_fox_logf[i], cache_nsa_kv[i],
                               state_nsa_win[i], state_ffn_conv[i], page_table, W, rel_bias, alpha)
        st_p.append(sp)
        st_s.append(ss)

    def stk(lst, j):
        return jnp.stack([s[j] for s in lst])

    return (xp, xs, stk(st_p, 0), stk(st_s, 0), stk(st_p, 1), stk(st_s, 1), stk(st_p, 2), stk(st_s, 2),
            stk(st_p, 3), stk(st_s, 3), stk(st_p, 4), stk(st_s, 4))
```

```python
import functools
import math

import numpy as np
import jax
import jax.numpy as jnp
from jax import lax
from jax.experimental import pallas as pl
from jax.experimental.pallas import tpu as pltpu

F32 = jnp.float32
BF16 = jnp.bfloat16
MXU = jnp.bfloat16

HEAD_DIM = 64
KVH = 2
GRP = 4
FOX_HEADS = 8
CMP_BLOCK = 32
CMP_STRIDE = 16
SEL_BLOCK = 64
N_SELECT = 16
WINDOW = 512
N_BUCKETS = 32
MAX_DISTANCE = 128
PAGE = 128
LN_EPS = 1e-5
SCALE = HEAD_DIM ** -0.5
LOG2E = math.log2(math.e)
QSCALE = SCALE * LOG2E
NSA_Q_W = KVH * GRP * HEAD_DIM
NSA_KV_W = KVH * HEAD_DIM
FOX_W = FOX_HEADS * HEAD_DIM
LANE = 128
VMEM_LIMIT = 56 * 1024 * 1024
NEG_INF = float("-inf")
MASK_BIG = 1e9


def _cparams(sem):
    return pltpu.CompilerParams(dimension_semantics=sem, vmem_limit_bytes=VMEM_LIMIT)


def _dot(a, b):
    return jnp.dot(a, b, preferred_element_type=F32)


def _dot_nt(a, b):
    return lax.dot_general(a, b, (((1,), (1,)), ((), ())), preferred_element_type=F32)


def _split3(x):
    hi = x.astype(BF16).astype(F32)
    r = x - hi
    mid = r.astype(BF16).astype(F32)
    lo = (r - mid).astype(BF16).astype(F32)
    return hi, mid, lo


def _iota(shape, dim):
    return lax.broadcasted_iota(jnp.int32, shape, dim)


def _softmax2_rows(lg):
    m = jnp.max(lg, axis=-1, keepdims=True)
    m = jnp.where(m == NEG_INF, 0.0, m)
    e = jnp.exp2(lg - m)
    d = jnp.maximum(jnp.sum(e, axis=-1, keepdims=True), 1e-30)
    return e * (1.0 / d)


def _online_update(lg, m_ref, l_ref):
    m_old = m_ref[...]
    m_new = jnp.maximum(m_old, jnp.max(lg, axis=-1, keepdims=True))
    m_safe = jnp.where(m_new == NEG_INF, 0.0, m_new)
    alpha = jnp.exp2(m_old - m_safe)
    p = jnp.exp2(lg - m_safe)
    l_ref[...] = alpha * l_ref[...] + jnp.sum(p, axis=-1, keepdims=True)
    m_ref[...] = m_new
    return alpha, p


def _pipelined(n_items, produce, consume, ahead):
    pending = {}
    for i in range(n_items + ahead):
        if i < n_items:
            pending[i] = produce(i)
        if i >= ahead:
            consume(i - ahead, pending.pop(i - ahead))


def _top_k_mask(x, k):
    n, w = x.shape
    row = _iota((n, w), 0).astype(F32)
    picked = jnp.zeros((n, w), jnp.bool_)
    work = x
    for _ in range(k):
        top = jnp.max(work, axis=0, keepdims=True)
        first = jnp.min(jnp.where(work == top, row, float(n)), axis=0, keepdims=True)
        hit = row == first
        picked = picked | (hit & (top > NEG_INF))
        work = jnp.where(hit, NEG_INF, work)
    return picked


def _layer_norm(x, g, b):
    mu = jnp.mean(x, axis=-1, keepdims=True)
    xc = x - mu
    var = jnp.mean(xc * xc, axis=-1, keepdims=True)
    return xc * lax.rsqrt(var + LN_EPS) * g + b


def _bucket_np(dist):
    n = np.maximum(np.asarray(dist), 0)
    max_exact = N_BUCKETS // 2
    nf = np.maximum(n, 1).astype(np.float32)
    large = max_exact + (np.log(nf / np.float32(max_exact)) / np.float32(math.log(MAX_DISTANCE / max_exact))
                         * np.float32(N_BUCKETS - max_exact)).astype(np.int32)
    large = np.minimum(large, N_BUCKETS - 1)
    return np.where(n < max_exact, n, large).astype(np.int32)


FAR_BUCKET = int(_bucket_np(np.array(MAX_DISTANCE)))


def _proj_body(x_ref, wnq, wnkv, wkw, wfq, wfkv, wga, wgb, wsm, bsm,
               nq_o, nkv_o, kw_o, fq_o, fkv_o, ga_o, gb_o, ng_o, logf_o):
    xb = x_ref[...].astype(MXU)
    nq_o[...] = _dot(xb, wnq[...])
    nkv_o[...] = _dot(xb, wnkv[...])
    kw_o[...] = _dot(xb, wkw[...])
    fq_o[...] = _dot(xb, wfq[...])
    fkv_o[...] = _dot(xb, wfkv[...])
    ga_o[...] = jax.nn.sigmoid(_dot(xb, wga[...]))
    gb_o[...] = jax.nn.sigmoid(_dot(xb, wgb[...]))
    sm = _dot(xb, wsm[...])
    ng_o[...] = jax.nn.sigmoid(sm)
    z = sm + bsm[...]
    ls = jnp.minimum(z, 0.0) - jnp.log1p(jnp.exp(-jnp.abs(z)))
    logf_o[...] = ls[:, :FOX_HEADS]


def _proj(x2d, w_in, b_forget, tm):
    m, d = x2d.shape
    o = np.cumsum([0, NSA_Q_W, 6 * NSA_KV_W, KVH * GRP * 3, FOX_W, FOX_W, FOX_W, FOX_HEADS, d, d])
    wb = w_in.astype(MXU)
    wnq = wb[:, o[0]:o[1]]
    wnkv = wb[:, o[1]:o[1] + 4 * NSA_KV_W]
    wkw = wb[:, o[1] + 4 * NSA_KV_W:o[2]]
    wfq = wb[:, o[3]:o[4]]
    wfkv = wb[:, o[4]:o[6]]
    wga = wb[:, o[7]:o[8]]
    wgb = wb[:, o[8]:o[9]]
    nsm = FOX_HEADS + KVH * GRP * 3
    wsm = jnp.concatenate([wb[:, o[6]:o[7]], wb[:, o[2]:o[3]], jnp.zeros((d, LANE - nsm), MXU)], axis=1)
    bsm = jnp.concatenate([b_forget.astype(F32), jnp.zeros((LANE - FOX_HEADS,), F32)]).reshape(1, LANE)
    ws = [wnq, wnkv, wkw, wfq, wfkv, wga, wgb, wsm]
    widths = [w.shape[1] for w in ws] + [FOX_HEADS]
    row = lambda n: pl.BlockSpec((tm, n), lambda i: (i, 0))
    full = lambda a: pl.BlockSpec(a.shape, lambda i: (0, 0))
    out_shapes = [jax.ShapeDtypeStruct((m, n), F32) for n in widths]
    return pl.pallas_call(
        _proj_body,
        grid=(m // tm,),
        in_specs=[row(d)] + [full(w) for w in ws] + [full(bsm)],
        out_specs=[row(n) for n in widths],
        out_shape=out_shapes,
        compiler_params=_cparams(("parallel",)),
        name="proj",
    )(x2d, *ws, bsm)


def _bias_body(tb_ref, idx_ref, o_ref):
    idx = idx_ref[...]
    nh = tb_ref.shape[1]

    def step(k, acc):
        hd = k % nh
        val = (tb_ref[k // nh, hd] - tb_ref[FAR_BUCKET, hd]) * LOG2E
        return jnp.where(idx == k, val, acc)

    acc = lax.fori_loop(0, tb_ref.shape[0] * nh, step, jnp.zeros(idx.shape, F32))
    o_ref[...] = jnp.where(idx == -1, NEG_INF, acc)


def _bias_lookup(rel_bias, idx_np):
    idx = jnp.asarray(idx_np, dtype=jnp.int32)
    return pl.pallas_call(
        _bias_body,
        in_specs=[pl.BlockSpec(memory_space=pltpu.SMEM), pl.BlockSpec(idx.shape, lambda: (0, 0))],
        out_specs=pl.BlockSpec(idx.shape, lambda: (0, 0)),
        out_shape=jax.ShapeDtypeStruct(idx.shape, F32),
        name="bias_tables",
    )(rel_bias.astype(F32), idx)


def _bias_heads_body(tb_ref, idx_ref, o_ref):
    hd = pl.program_id(0)
    idx = idx_ref[...]

    def step(k, acc):
        return jnp.where(idx == k, (tb_ref[k, hd] - tb_ref[FAR_BUCKET, hd]) * LOG2E, acc)

    o_ref[...] = lax.fori_loop(0, tb_ref.shape[0], step, jnp.zeros(idx.shape, F32))


def _bias_lookup_heads(rel_bias, bucket_np):
    idx = jnp.asarray(bucket_np, dtype=jnp.int32)
    nh = rel_bias.shape[1]
    return pl.pallas_call(
        _bias_heads_body,
        grid=(nh,),
        in_specs=[pl.BlockSpec(memory_space=pltpu.SMEM), pl.BlockSpec(idx.shape, lambda h: (0, 0))],
        out_specs=pl.BlockSpec(idx.shape, lambda h: (h, 0)),
        out_shape=jax.ShapeDtypeStruct((nh * idx.shape[0], idx.shape[1]), F32),
        name="bias_tables_heads",
    )(rel_bias.astype(F32), idx)


def _bias_idx(dist, head, ok):
    nh = KVH * GRP
    return np.where(ok & (dist >= 0), _bucket_np(dist) * nh + head, -1).astype(np.int32)


def _cumsum_body(pt_ref, *refs, gp, ns, has_new):
    pages = refs[:gp]
    rest = refs[gp:]
    new_ref = rest[0] if has_new else None
    c_o, ct_o, pad_s, carry_s = rest[1:] if has_new else rest
    st = pl.program_id(1)

    @pl.when(st == 0)
    def _():
        pad_s[...] = jnp.zeros_like(pad_s)
        carry_s[...] = jnp.zeros_like(carry_s)

    ltri = (_iota((PAGE, PAGE), 1) <= _iota((PAGE, PAGE), 0)).astype(BF16)

    def one_page(x, w):
        pad_s[:, 0:FOX_HEADS] = x
        xp = pad_s[...]
        cs = sum(_dot(ltri, t.astype(BF16)) for t in _split3(xp)) + carry_s[...]
        c_o[0, w * PAGE:(w + 1) * PAGE, :] = cs[:, 0:FOX_HEADS]
        ct_o[0, :, w * PAGE:(w + 1) * PAGE] = cs.T[0:FOX_HEADS, :]
        carry_s[...] = cs[PAGE - 1:PAGE, :]

    @pl.when(st < ns)
    def _():
        for w in range(gp):
            one_page(pages[w][0], w)

    if has_new:
        @pl.when(st == ns)
        def _():
            one_page(new_ref[0], 0)


def _paged_cumsum(pool, page_table, gp, new_page=None):
    b, npg = page_table.shape
    ns = npg // gp
    has_new = new_page is not None
    nsteps = ns + (1 if has_new else 0)
    stc = lambda st: jnp.minimum(st, ns - 1)
    in_specs = [pl.BlockSpec((1, PAGE, FOX_HEADS), functools.partial(
        lambda i, st, pt, w: (pt[i, stc(st) * gp + w], 0, 0), w=w)) for w in range(gp)]
    args = [pool] * gp
    if has_new:
        in_specs.append(pl.BlockSpec((1, PAGE, FOX_HEADS), lambda i, st, pt: (i, 0, 0)))
        args.append(new_page)
    width = gp * PAGE
    grid_spec = pltpu.PrefetchScalarGridSpec(
        num_scalar_prefetch=1,
        grid=(b, nsteps),
        in_specs=in_specs,
        out_specs=[pl.BlockSpec((1, width, FOX_HEADS), lambda i, st, pt: (i, st, 0)),
                   pl.BlockSpec((1, FOX_HEADS, width), lambda i, st, pt: (i, 0, st))],
        scratch_shapes=[pltpu.VMEM((PAGE, LANE), F32), pltpu.VMEM((1, LANE), F32)],
    )
    return pl.pallas_call(
        functools.partial(_cumsum_body, gp=gp, ns=ns, has_new=has_new),
        grid_spec=grid_spec,
        out_shape=[jax.ShapeDtypeStruct((b, nsteps * width, FOX_HEADS), F32),
                   jax.ShapeDtypeStruct((b, FOX_HEADS, nsteps * width), F32)],
        compiler_params=_cparams(("arbitrary", "arbitrary")),
        name="logf_cumsum",
    )(page_table, *args)


def _cumsum_t_body(pt_ref, *refs, gp, ns):
    pages = refs[:gp]
    new_ref, ct_o, carry_s = refs[gp:]
    st = pl.program_id(1)

    @pl.when(st == 0)
    def _():
        carry_s[...] = jnp.zeros_like(carry_s)

    utri = (_iota((PAGE, PAGE), 0) <= _iota((PAGE, PAGE), 1)).astype(BF16)

    def scan(xs):
        pad = [jnp.zeros_like(xs[0])] * (len(xs) % 2)
        x = jnp.concatenate(list(xs) + pad, axis=0)
        local = sum(_dot(t.astype(BF16), utri) for t in _split3(x))
        run = carry_s[...]
        for w in range(len(xs)):
            lw = local[w * FOX_HEADS:(w + 1) * FOX_HEADS]
            ct_o[0, :, w * PAGE:(w + 1) * PAGE] = lw + run
            run = run + jnp.broadcast_to(lw[:, PAGE - 1:PAGE], run.shape)
        carry_s[...] = run

    @pl.when(st < ns)
    def _():
        scan([pg[0] for pg in pages])

    @pl.when(st == ns)
    def _():
        scan([new_ref[0]])
        if gp > 1:
            ct_o[0, :, PAGE:gp * PAGE] = jnp.zeros((FOX_HEADS, (gp - 1) * PAGE), F32)


def _paged_cumsum_t(pool_t, page_table, gp, new_page_t):
    b, npg = page_table.shape
    ns = npg // gp
    stc = lambda st: jnp.minimum(st, ns - 1)
    page_spec = lambda w: pl.BlockSpec((1, FOX_HEADS, PAGE), lambda i, st, pt: (pt[i, stc(st) * gp + w], 0, 0))
    width = gp * PAGE
    grid_spec = pltpu.PrefetchScalarGridSpec(
        num_scalar_prefetch=1,
        grid=(b, ns + 1),
        in_specs=[page_spec(w) for w in range(gp)] + [pl.BlockSpec((1, FOX_HEADS, PAGE), lambda i, st, pt: (i, 0, 0))],
        out_specs=pl.BlockSpec((1, FOX_HEADS, width), lambda i, st, pt: (i, 0, st)),
        scratch_shapes=[pltpu.VMEM((FOX_HEADS, LANE), F32)],
    )
    return pl.pallas_call(
        functools.partial(_cumsum_t_body, gp=gp, ns=ns),
        grid_spec=grid_spec,
        out_shape=jax.ShapeDtypeStruct((b, FOX_HEADS, (ns + 1) * width), F32),
        compiler_params=_cparams(("arbitrary", "arbitrary")),
        name="logf_cumsum_t",
    )(page_table, *([pool_t] * gp), new_page_t)


def _compress_body(pt_ref, *refs, gp, nrows, transposed):
    pages = refs[:gp]
    w1_ref, pe_ref, w2_ref, out_ref, buf_k, buf_v = refs[gp:]
    st = pl.program_id(1)
    ns = nrows // (gp * PAGE)
    nb = nrows // CMP_STRIDE
    bufs = (buf_k, buf_v)

    @pl.when(st == 0)
    def _():
        for buf in bufs:
            buf[nrows:nrows + CMP_BLOCK, :] = jnp.zeros((CMP_BLOCK, LANE), F32)

    eye = (_iota((PAGE, PAGE), 0) == _iota((PAGE, PAGE), 1)).astype(MXU)
    for w in range(gp):
        base = pl.multiple_of((st * gp + w) * PAGE, PAGE)
        for typ, buf in enumerate(bufs):
            if transposed:
                xt = pages[w][0, typ].reshape(LANE, PAGE).astype(MXU)
                buf[pl.ds(base, PAGE), :] = _dot_nt(eye, xt)
            else:
                buf[pl.ds(base, PAGE), :] = pages[w][0, :, typ * LANE:(typ + 1) * LANE].astype(F32)

    @pl.when(st == ns - 1)
    def _():
        for typ, buf in enumerate(bufs):
            acc = jnp.zeros((nb, 2 * LANE), F32)
            bias = jnp.zeros((8, 2 * LANE), F32)
            for lp in range(CMP_BLOCK // 2):
                xa = buf[pl.ds(2 * lp, nb, stride=CMP_STRIDE), :]
                xb = buf[pl.ds(2 * lp + 1, nb, stride=CMP_STRIDE), :]
                x2 = jnp.concatenate([xa, xb], axis=1).astype(MXU)
                acc = acc + _dot(x2, w1_ref[typ, lp])
                pe2 = jnp.concatenate([pe_ref[typ, 2 * lp:2 * lp + 1, :], pe_ref[typ, 2 * lp + 1:2 * lp + 2, :]], axis=1)
                bias = bias + _dot(jnp.broadcast_to(pe2, (8, 2 * LANE)).astype(MXU), w1_ref[typ, lp])
            hid = jax.nn.gelu(acc + bias[0:1, :])
            out_ref[0, :, typ * LANE:(typ + 1) * LANE] = _dot(hid.astype(MXU), w2_ref[typ]).astype(out_ref.dtype)


def _compress(pool, page_table, pe, w1, w2, gp, transposed=False):
    b, npg = page_table.shape
    nrows = npg * PAGE
    nb = nrows // CMP_STRIDE
    z = jnp.zeros_like(w1)
    w1bd = jnp.concatenate([jnp.concatenate([w1, z], axis=-1), jnp.concatenate([z, w1], axis=-1)], axis=-2)
    w1pair = w1bd.reshape(2, CMP_BLOCK // 2, 4 * HEAD_DIM, w1bd.shape[-1]).astype(MXU)
    z2 = jnp.zeros_like(w2)
    w2bd = jnp.concatenate([jnp.concatenate([w2, z2], axis=-1), jnp.concatenate([z2, w2], axis=-1)], axis=-2).astype(MXU)
    pe2 = jnp.concatenate([pe, pe], axis=-1).astype(F32)
    if transposed:
        page_spec = lambda w: pl.BlockSpec((1, 2, KVH, HEAD_DIM, PAGE), lambda i, st, pt: (pt[i, st * gp + w], 0, 0, 0, 0))
    else:
        page_spec = lambda w: pl.BlockSpec((1, PAGE, 2 * LANE), lambda i, st, pt: (pt[i, st * gp + w], 0, 0))
    grid_spec = pltpu.PrefetchScalarGridSpec(
        num_scalar_prefetch=1,
        grid=(b, npg // gp),
        in_specs=[page_spec(w) for w in range(gp)]
                 + [pl.BlockSpec(w1pair.shape, lambda i, st, pt: (0, 0, 0, 0)),
                    pl.BlockSpec(pe2.shape, lambda i, st, pt: (0, 0, 0)),
                    pl.BlockSpec(w2bd.shape, lambda i, st, pt: (0, 0, 0))],
        out_specs=pl.BlockSpec((1, nb, 2 * LANE), lambda i, st, pt: (i, 0, 0)),
        scratch_shapes=[pltpu.VMEM((nrows + CMP_BLOCK, LANE), F32), pltpu.VMEM((nrows + CMP_BLOCK, LANE), F32)],
    )
    return pl.pallas_call(
        functools.partial(_compress_body, gp=gp, nrows=nrows, transposed=transposed),
        grid_spec=grid_spec,
        out_shape=jax.ShapeDtypeStruct((b, nb, 2 * LANE), MXU),
        compiler_params=_cparams(("arbitrary", "arbitrary")),
        name="nsa_compress",
    )(page_table, *([pool] * gp), w1pair, pe2, w2bd)


def _fox_prep_body(q_ref, k_ref, v_ref, c_ref, qa_o, ka_o, va_o):
    tp = q_ref.shape[1]
    lane = _iota((tp, LANE), 1)
    q = q_ref[0]
    k = k_ref[0]
    v = v_ref[0]
    c = c_ref[0] * LOG2E
    one = jnp.ones((tp, LANE), F32)
    zero = jnp.zeros((tp, LANE), F32)
    for h in range(FOX_HEADS):
        qs = q[:, (h // 2) * LANE:(h // 2 + 1) * LANE]
        ks = k[:, (h // 2) * LANE:(h // 2 + 1) * LANE]
        vs = v[:, (h // 2) * LANE:(h // 2 + 1) * LANE]
        if h % 2 == 1:
            qs = pltpu.roll(qs, HEAD_DIM, 1)
            ks = pltpu.roll(ks, HEAD_DIM, 1)
            vs = pltpu.roll(vs, HEAD_DIM, 1)
        chi, cmid, clo = _split3(jnp.broadcast_to(c[:, h:h + 1], (tp, LANE)))
        qaug = jnp.where(lane == HEAD_DIM, chi, jnp.where(lane == HEAD_DIM + 1, cmid, jnp.where(
            lane == HEAD_DIM + 2, clo, jnp.where(lane < HEAD_DIM + 6, one, zero))))
        kaug = jnp.where(lane == HEAD_DIM + 3, -chi, jnp.where(lane == HEAD_DIM + 4, -cmid, jnp.where(
            lane == HEAD_DIM + 5, -clo, jnp.where(lane < HEAD_DIM + 3, one, zero))))
        qa_o[0, h] = jnp.where(lane < HEAD_DIM, qs * QSCALE, qaug).astype(qa_o.dtype)
        ka_o[0, h] = jnp.where(lane < HEAD_DIM, ks, kaug).astype(ka_o.dtype)
        va_o[0, h] = jnp.where(lane < HEAD_DIM, vs, jnp.where(lane == HEAD_DIM, one, zero)).astype(va_o.dtype)


def _fox_prep(fq, fkv, c, tp):
    b, t, _ = fq.shape
    hspec = pl.BlockSpec((1, FOX_HEADS, tp, LANE), lambda i, j: (i, 0, j, 0))
    hshape = jax.ShapeDtypeStruct((b, FOX_HEADS, t, LANE), MXU)
    return pl.pallas_call(
        _fox_prep_body,
        grid=(b, t // tp),
        in_specs=[pl.BlockSpec((1, tp, FOX_W), lambda i, j: (i, j, 0)),
                  pl.BlockSpec((1, tp, FOX_W), lambda i, j: (i, j, 0)),
                  pl.BlockSpec((1, tp, FOX_W), lambda i, j: (i, j, 1)),
                  pl.BlockSpec((1, tp, FOX_HEADS), lambda i, j: (i, j, 0))],
        out_specs=[hspec, hspec, hspec],
        out_shape=[hshape, hshape, hshape],
        compiler_params=_cparams(("parallel", "parallel")),
        name="fox_prep",
    )(fq, fkv, fkv, c)


RC = 256
KC = 512
AHEAD = 3


def _flash_chunk(s, m_ref, acc_ref, r0, v):
    rc = s.shape[0]
    rep = s.shape[1] // LANE
    m_old = m_ref[r0:r0 + rc, :]
    m_new = jnp.maximum(m_old, jnp.max(s, axis=1, keepdims=True))
    alpha = jnp.exp2(m_old - m_new)
    p = jnp.exp2(s - jnp.concatenate([m_new] * rep, axis=1))
    acc = acc_ref[r0:r0 + rc, :]
    acc_ref[r0:r0 + rc, :] = jnp.concatenate([alpha] * (acc.shape[1] // LANE), axis=1) * acc + _dot(p.astype(MXU), v)
    m_ref[r0:r0 + rc, :] = m_new


def _pfox_body(q_ref, k_ref, v_ref, o_ref, m_s, acc_s, *, tq, tk, nk):
    qi = pl.program_id(2)
    kj = pl.program_id(3)
    last = ((qi + 1) * tq - 1) // tk

    @pl.when(kj == 0)
    def _():
        m_s[...] = jnp.full_like(m_s, -MASK_BIG)
        acc_s[...] = jnp.zeros_like(acc_s)

    kc = min(KC, tk)

    def update(diag):
        items = []
        for ks in range(tk // kc):
            for hh in range(2):
                for c in range(tq // RC):
                    r0 = c * RC
                    if diag and ks * kc > r0 + RC - 1:
                        continue
                    items.append((hh, r0, ks, diag and ks * kc + kc - 1 > r0))

        def logits(i):
            hh, r0, ks, masked = items[i]
            s = _dot_nt(q_ref[0, hh, r0:r0 + RC, :], k_ref[0, hh, ks * kc:(ks + 1) * kc, :])
            if masked:
                causal = (ks * kc + _iota((RC, kc), 1)) <= (r0 + _iota((RC, kc), 0))
                s = jnp.where(causal, s, -MASK_BIG)
            return s

        def finish(i, s):
            hh, r0, ks, _ = items[i]
            _flash_chunk(s, m_s.at[hh], acc_s.at[hh], r0, v_ref[0, hh, ks * kc:(ks + 1) * kc, :])

        _pipelined(len(items), logits, finish, AHEAD)

    crosses = kj == qi

    @pl.when((kj <= last) & jnp.logical_not(crosses))
    def _():
        update(False)

    @pl.when((kj <= last) & crosses)
    def _():
        update(True)

    @pl.when(kj == nk - 1)
    def _():
        lane = _iota((tq, LANE), 1)
        outs = []
        for hh in range(2):
            acc = acc_s[hh]
            outs.append(acc * (1.0 / acc[:, HEAD_DIM:HEAD_DIM + 1]))
        o_ref[0] = jnp.where(lane < HEAD_DIM, outs[0], pltpu.roll(outs[1], HEAD_DIM, 1))


def _prompt_fox(qa, ka, va, tq, tk):
    assert tq == tk
    b, _, t, _ = qa.shape
    nq, nk = t // tq, t // tk
    hp = FOX_HEADS // 2
    lastf = lambda qi: ((qi + 1) * tq - 1) // tk
    kspec = pl.BlockSpec((1, 2, tk, LANE), lambda i, h, qi, kj: (i, h, jnp.minimum(kj, lastf(qi)), 0))
    return pl.pallas_call(
        functools.partial(_pfox_body, tq=tq, tk=tk, nk=nk),
        grid=(b, hp, nq, nk),
        in_specs=[pl.BlockSpec((1, 2, tq, LANE), lambda i, h, qi, kj: (i, h, qi, 0)), kspec, kspec],
        out_specs=pl.BlockSpec((1, tq, LANE), lambda i, h, qi, kj: (i, qi, h)),
        out_shape=jax.ShapeDtypeStruct((b, t, FOX_W), F32),
        scratch_shapes=[pltpu.VMEM((2, tq, LANE), F32), pltpu.VMEM((2, tq, LANE), F32)],
        compiler_params=_cparams(("parallel", "parallel", "parallel", "arbitrary")),
        name="prompt_fox",
    )(qa, ka, va)


QB = 128
NWIN = WINDOW // QB + 1


def _nsa_prep_body(k_ref, v_ref, e_ref, kcat_o, vcat_o):
    tp = k_ref.shape[1]
    lane = _iota((tp, LANE), 1)
    kcat_o[0] = jnp.concatenate([k_ref[0].astype(MXU), e_ref[...]], axis=1)
    ones_col = jnp.where(lane == 0, 1.0, 0.0).astype(MXU)
    vcat_o[0] = jnp.concatenate([v_ref[0].astype(MXU), ones_col], axis=1)


def _nsa_prep(nkv, e_mat, tp):
    b, t, _ = nkv.shape
    njp = e_mat.shape[1]
    return pl.pallas_call(
        _nsa_prep_body,
        grid=(b, t // tp),
        in_specs=[pl.BlockSpec((1, tp, LANE), lambda i, j: (i, j, 2)),
                  pl.BlockSpec((1, tp, LANE), lambda i, j: (i, j, 3)),
                  pl.BlockSpec((tp, njp), lambda i, j: (j, 0))],
        out_specs=[pl.BlockSpec((1, tp, LANE + njp), lambda i, j: (i, j, 0)),
                   pl.BlockSpec((1, tp, 2 * LANE), lambda i, j: (i, j, 0))],
        out_shape=[jax.ShapeDtypeStruct((b, t, LANE + njp), MXU), jax.ShapeDtypeStruct((b, t, 2 * LANE), MXU)],
        compiler_params=_cparams(("parallel", "parallel")),
        name="nsa_prep",
    )(nkv, nkv, e_mat)


def _pnsa_body(q_ref, g_ref, kc_ref, vc_ref, kcat_ref, vcat_ref, *refs, t_len, tk, nk):
    wk = refs[:NWIN]
    wv = refs[NWIN:2 * NWIN]
    ovt_ref, band_ref, bandc_ref, o_ref, qa_s, oc_s, ow_s, m_s, acc_s = refs[2 * NWIN:]
    qi = pl.program_id(1)
    kj = pl.program_id(2)
    t0 = qi * QB
    last = (t0 + QB - 1) // tk
    njp = ovt_ref.shape[0]
    nc = t_len // CMP_STRIDE
    n_c = (t_len - CMP_BLOCK) // CMP_STRIDE + 1
    nhg = KVH * GRP
    ch = 2 * QB

    @pl.when(kj == 0)
    def _():
        q = q_ref[0]
        lane = _iota((QB, LANE), 1)
        for hg in range(nhg):
            h = hg // GRP
            sl = q[:, (hg // 2) * LANE:(hg // 2 + 1) * LANE]
            if hg % 2 != h:
                sl = pltpu.roll(sl, HEAD_DIM, 1)
            qa_s[hg * QB:(hg + 1) * QB, 0:LANE] = (jnp.where((lane // HEAD_DIM) == h, sl, 0.0) * QSCALE).astype(MXU)

        kcb = kc_ref[0]
        vcb = vc_ref[0]
        shift_t = ((_iota((nc, LANE), 0) == 8 * qi - 9 + _iota((nc, LANE), 1))
                   & (_iota((nc, LANE), 1) < 16)).astype(MXU)
        rhs = jnp.concatenate([kcb, shift_t, shift_t, shift_t], axis=1)
        trow = t0 + _iota((ch, nc), 0) % QB
        ncol = _iota((ch, nc), 1)
        maskc = (CMP_STRIDE * ncol + (CMP_BLOCK - 1) <= trow) & (ncol < n_c)
        pcs = [None] * KVH

        def cmp_logits(c):
            r0 = c * ch
            lhs = jnp.concatenate([qa_s[r0:r0 + ch, 0:LANE]]
                                  + [t.astype(MXU) for t in _split3(bandc_ref[r0:r0 + ch, :])], axis=1)
            return _dot_nt(lhs, rhs)

        def cmp_finish(c, s):
            r0 = c * ch
            pc = _softmax2_rows(jnp.where(maskc, s, NEG_INF))
            oc_s[r0:r0 + ch, :] = _dot(pc.astype(MXU), vcb)
            h = r0 // (GRP * QB)
            part = pc[0:QB] + pc[QB:ch]
            pcs[h] = part if pcs[h] is None else pcs[h] + part

        _pipelined(nhg * QB // ch, cmp_logits, cmp_finish, 2)
        pcsum = jnp.concatenate(pcs, axis=0)
        imp = _dot_nt(ovt_ref[...], pcsum.astype(MXU))
        width = KVH * QB
        jr = _iota((njp, width), 0)
        qblk = (t0 + _iota((njp, width), 1) % QB) // SEL_BLOCK
        forced = (jr == 0) | (jr == qblk) | (jr == qblk - 1)
        imp = jnp.where(forced, jnp.inf, jnp.where(jr > qblk, NEG_INF, imp))
        picked = _top_k_mask(imp, N_SELECT)
        pen_t = jnp.where(picked, 0.0, -MASK_BIG).astype(MXU)
        eye = (_iota((width, width), 0) == _iota((width, width), 1)).astype(MXU)
        pen = _dot_nt(eye, pen_t).astype(MXU)
        for hg in range(nhg):
            h = hg // GRP
            qa_s[hg * QB:(hg + 1) * QB, LANE:LANE + njp] = pen[h * QB:(h + 1) * QB]

        kwin = jnp.concatenate([r[0].astype(MXU) for r in wk], axis=0)
        vwin = jnp.concatenate([r[0].astype(MXU) for r in wv], axis=0)
        ri = _iota((ch, QB), 0) % QB
        ci = _iota((ch, QB), 1)

        def win_logits(c):
            return _dot_nt(qa_s[c * ch:(c + 1) * ch, 0:LANE], kwin)

        def win_finish(c, s):
            r0 = c * ch
            blocks = []
            for w in range(NWIN):
                sw = s[:, w * QB:(w + 1) * QB]
                ok = qi - (NWIN - 1) + w >= 0
                if w == NWIN - 1:
                    sw = jnp.where((ci <= ri) & ok, sw + band_ref[r0:r0 + ch, QB:2 * QB], NEG_INF)
                elif w == NWIN - 2:
                    sw = jnp.where(ok, sw + band_ref[r0:r0 + ch, 0:QB], NEG_INF)
                elif w == 0:
                    sw = jnp.where((ci > ri) & ok, sw, NEG_INF)
                else:
                    sw = jnp.where(ok, sw, NEG_INF)
                blocks.append(sw)
            pw = _softmax2_rows(jnp.concatenate(blocks, axis=1))
            ow_s[r0:r0 + ch, :] = _dot(pw.astype(MXU), vwin)

        _pipelined(nhg * QB // ch, win_logits, win_finish, 2)

        m_s[...] = jnp.full_like(m_s, -MASK_BIG)
        acc_s[...] = jnp.zeros_like(acc_s)

    kc = min(KC, tk)
    nsub = tk // kc

    nrc = nhg * QB // RC

    def step(near):
        def logits(i):
            ks, r0 = i // nrc, (i % nrc) * RC
            s = _dot_nt(qa_s[r0:r0 + RC, :], kcat_ref[0, ks * kc:(ks + 1) * kc, :])
            if near:
                trow = t0 + (r0 + _iota((RC, kc), 0)) % QB
                scol = kj * tk + ks * kc + _iota((RC, kc), 1)
                chunks = []
                for c in range(kc // QB):
                    delta = qi - ((kj * tk + ks * kc) // QB + c)
                    chunks.append(jnp.where(delta == 0, band_ref[r0:r0 + RC, QB:2 * QB],
                                            jnp.where(delta == 1, band_ref[r0:r0 + RC, 0:QB], 0.0)))
                s = jnp.where(scol <= trow, s + jnp.concatenate(chunks, axis=1), -MASK_BIG)
            return s

        def finish(i, s):
            ks, r0 = i // nrc, (i % nrc) * RC
            _flash_chunk(s, m_s, acc_s, r0, vcat_ref[0, ks * kc:(ks + 1) * kc, :])

        _pipelined(nsub * nrc, logits, finish, AHEAD)

    near_from = (t0 - QB) // tk

    @pl.when(kj < near_from)
    def _():
        step(False)

    @pl.when((kj >= near_from) & (kj <= last))
    def _():
        step(True)

    @pl.when(kj == nk - 1)
    def _():
        gt = g_ref[0]
        lane = _iota((QB, LANE), 1)
        outs = []
        for hg in range(nhg):
            r0 = hg * QB
            acc = acc_s[r0:r0 + QB, :]
            osl = acc[:, 0:LANE] * (1.0 / acc[:, LANE:LANE + 1])
            off = FOX_HEADS + hg * 3
            outs.append(gt[:, off:off + 1] * oc_s[r0:r0 + QB, :] + gt[:, off + 1:off + 2] * osl
                        + gt[:, off + 2:off + 3] * ow_s[r0:r0 + QB, :])
        for pr in range(nhg // 2):
            h = (2 * pr) // GRP
            ev, od = outs[2 * pr], outs[2 * pr + 1]
            if h == 0:
                od = pltpu.roll(od, HEAD_DIM, 1)
            else:
                ev = pltpu.roll(ev, HEAD_DIM, 1)
            o_ref[0, :, pr * LANE:(pr + 1) * LANE] = jnp.where(lane < HEAD_DIM, ev, od)


def _prompt_nsa(nq, ng, cmp2, kcat, vcat, kw, ovt, band, bandc, tk):
    b, t, _ = nq.shape
    nqb, nk = t // QB, t // tk
    nc = cmp2.shape[1]
    njp = ovt.shape[0]
    rows = KVH * GRP * QB
    lastf = lambda qi: (qi * QB + QB - 1) // tk
    kjc = lambda qi, kj: jnp.minimum(kj, lastf(qi))
    wspec = lambda w, part: pl.BlockSpec((1, QB, LANE), lambda i, qi, kj: (i, jnp.maximum(qi - (NWIN - 1) + w, 0), part))
    const = lambda a: pl.BlockSpec(a.shape, lambda i, qi, kj: (0,) * a.ndim)
    return pl.pallas_call(
        functools.partial(_pnsa_body, t_len=t, tk=tk, nk=nk),
        grid=(b, nqb, nk),
        in_specs=[pl.BlockSpec((1, QB, NSA_Q_W), lambda i, qi, kj: (i, qi, 0)),
                  pl.BlockSpec((1, QB, LANE), lambda i, qi, kj: (i, qi, 0)),
                  pl.BlockSpec((1, nc, LANE), lambda i, qi, kj: (i, 0, 0)),
                  pl.BlockSpec((1, nc, LANE), lambda i, qi, kj: (i, 0, 1)),
                  pl.BlockSpec((1, tk, LANE + njp), lambda i, qi, kj: (i, kjc(qi, kj), 0)),
                  pl.BlockSpec((1, tk, 2 * LANE), lambda i, qi, kj: (i, kjc(qi, kj), 0))]
                 + [wspec(w, 0) for w in range(NWIN)] + [wspec(w, 1) for w in range(NWIN)]
                 + [const(ovt), const(band), const(bandc)],
        out_specs=pl.BlockSpec((1, QB, NSA_Q_W), lambda i, qi, kj: (i, qi, 0)),
        out_shape=jax.ShapeDtypeStruct((b, t, NSA_Q_W), F32),
        scratch_shapes=[pltpu.VMEM((rows, LANE + njp), MXU), pltpu.VMEM((rows, LANE), F32), pltpu.VMEM((rows, LANE), F32),
                        pltpu.VMEM((rows, LANE), F32), pltpu.VMEM((rows, 2 * LANE), F32)],
        compiler_params=_cparams(("parallel", "parallel", "arbitrary")),
        name="prompt_nsa",
    )(nq, ng, cmp2, cmp2, kcat, vcat, *([kw] * (2 * NWIN)), ovt, band, bandc)


def _dfox_body(pt_ref, *refs, gp, ns, s_new):
    kpages = refs[:gp]
    vpages = refs[gp:2 * gp]
    q_ref, cq_ref, ck_ref, ckn_ref, new_ref, o_ref, m_s, l_s, acc_s = refs[2 * gp:]
    st = pl.program_id(1)
    rows = FOX_HEADS * s_new
    qb = (q_ref[0] * QSCALE).astype(MXU)

    @pl.when(st == 0)
    def _():
        m_s[...] = jnp.full_like(m_s, NEG_INF)
        l_s[...] = jnp.zeros_like(l_s)
        acc_s[...] = jnp.zeros_like(acc_s)

    def flat(x):
        return x.reshape(FOX_W, PAGE).astype(MXU)

    @pl.when(st < ns)
    def _():
        s = jnp.concatenate([_dot(qb, flat(kp[0, 0])) for kp in kpages], axis=1)
        ck = jnp.concatenate([ck_ref[0]] * s_new, axis=0)
        lg = s + (cq_ref[0] - ck) * LOG2E
        alpha, p = _online_update(lg, m_s, l_s)
        pv = jnp.zeros((rows, FOX_W), F32)
        for w in range(gp):
            pv = pv + _dot_nt(p[:, w * PAGE:(w + 1) * PAGE].astype(MXU), flat(vpages[w][0, 0]))
        acc_s[...] = alpha * acc_s[...] + pv

    @pl.when(st == ns)
    def _():
        s = _dot(qb, flat(new_ref[0, 0]))
        ck = jnp.concatenate([ckn_ref[0]] * s_new, axis=0)
        qrow = _iota((rows, PAGE), 0) // FOX_HEADS
        kcol = _iota((rows, PAGE), 1)
        lg = jnp.where(kcol <= qrow, s + (cq_ref[0] - ck) * LOG2E, NEG_INF)
        alpha, p = _online_update(lg, m_s, l_s)
        acc = alpha * acc_s[...] + _dot_nt(p.astype(MXU), flat(new_ref[0, 1]))
        o = acc * (1.0 / l_s[...])
        own = (_iota((rows, FOX_W), 1) // HEAD_DIM) == (_iota((rows, FOX_W), 0) % FOX_HEADS)
        o = jnp.where(own, o, 0.0)
        o_ref[0] = jnp.sum(o.reshape(s_new, FOX_HEADS, FOX_W), axis=1)


def _decode_fox(cache_t, page_table, qrows, cq, ct_all, new_t, gp):
    b, npg = page_table.shape
    ns = npg // gp
    rows = qrows.shape[1]
    s_new = rows // FOX_HEADS
    stc = lambda st: jnp.minimum(st, ns - 1)
    page_spec = lambda w, part: pl.BlockSpec((1, 1, FOX_HEADS, HEAD_DIM, PAGE),
                                             lambda i, st, pt: (pt[i, stc(st) * gp + w], part, 0, 0, 0))
    grid_spec = pltpu.PrefetchScalarGridSpec(
        num_scalar_prefetch=1,
        grid=(b, ns + 1),
        in_specs=[page_spec(w, 0) for w in range(gp)] + [page_spec(w, 1) for w in range(gp)]
                 + [pl.BlockSpec((1, rows, FOX_W), lambda i, st, pt: (i, 0, 0)),
                    pl.BlockSpec((1, rows, 1), lambda i, st, pt: (i, 0, 0)),
                    pl.BlockSpec((1, FOX_HEADS, gp * PAGE), lambda i, st, pt: (i, 0, stc(st))),
                    pl.BlockSpec((1, FOX_HEADS, PAGE), lambda i, st, pt: (i, 0, npg)),
                    pl.BlockSpec((1, 2, FOX_HEADS, HEAD_DIM, PAGE), lambda i, st, pt: (i, 0, 0, 0, 0))],
        out_specs=pl.BlockSpec((1, s_new, FOX_W), lambda i, st, pt: (i, 0, 0)),
        scratch_shapes=[pltpu.VMEM((rows, 1), F32), pltpu.VMEM((rows, 1), F32), pltpu.VMEM((rows, FOX_W), F32)],
    )
    return pl.pallas_call(
        functools.partial(_dfox_body, gp=gp, ns=ns, s_new=s_new),
        grid_spec=grid_spec,
        out_shape=jax.ShapeDtypeStruct((b, s_new, FOX_W), F32),
        compiler_params=_cparams(("parallel", "arbitrary")),
        name="decode_fox",
    )(page_table, *([cache_t] * (2 * gp)), qrows, cq, ct_all, ct_all, new_t)


def _dnsa_body(pt_ref, *refs, gp, ns, nj, njp):
    kpages = refs[:gp]
    vpages = refs[gp:2 * gp]
    (q_ref, g_ref, kc_ref, vc_ref, ovt_ref, bcmp_ref, bwin_ref, blast_ref, bnew_ref,
     win_ref, wnew_ref, snew_ref, o_ref, sel_s, oc_s, ow_s, m_s, l_s, acc_s) = refs[2 * gp:]
    st = pl.program_id(1)
    rows = q_ref.shape[1]
    rq = rows // GRP
    qb = (q_ref[0] * QSCALE).astype(MXU)

    @pl.when(st == 0)
    def _():
        kc = kc_ref[0].astype(MXU)
        vc = vc_ref[0].astype(MXU)
        pc = _softmax2_rows(_dot_nt(qb, kc) + bcmp_ref[...])
        oc_s[...] = _dot(pc.astype(MXU), vc)
        pcs = pc[0:rq]
        for g in range(1, GRP):
            pcs = pcs + pc[g * rq:(g + 1) * rq]
        pcs = jnp.concatenate([pcs, jnp.zeros((LANE - rq, pcs.shape[1]), F32)], axis=0)
        imp = _dot_nt(ovt_ref[...], pcs.astype(MXU))
        jr = _iota((njp, LANE), 0)
        qblk = ((nj - 1) * SEL_BLOCK + _iota((njp, LANE), 1) // KVH) // SEL_BLOCK
        forced = (jr == 0) | (jr == qblk) | (jr == qblk - 1)
        imp = jnp.where(forced, jnp.inf, jnp.where(jr > qblk, NEG_INF, imp))
        sel_t = _top_k_mask(imp, N_SELECT).astype(BF16)
        eye = (_iota((LANE, LANE), 0) == _iota((LANE, LANE), 1)).astype(BF16)
        sel = _dot_nt(eye, sel_t).astype(BF16)
        sel_s[...] = jnp.concatenate([sel[0:rq]] * GRP, axis=0)

        flat = lambda x: x.reshape(LANE, x.shape[-1]).astype(MXU)
        bw = bwin_ref[...]
        sw = jnp.concatenate([_dot(qb, flat(win_ref[0, 0])) + bw[:, 0:WINDOW],
                              _dot(qb, flat(wnew_ref[0, 0])) + bw[:, WINDOW:WINDOW + PAGE]], axis=1)
        pw = _softmax2_rows(sw)
        ow_s[...] = (_dot_nt(pw[:, 0:WINDOW].astype(MXU), flat(win_ref[0, 1]))
                     + _dot_nt(pw[:, WINDOW:WINDOW + PAGE].astype(MXU), flat(wnew_ref[0, 1])))

        m_s[...] = jnp.full_like(m_s, NEG_INF)
        l_s[...] = jnp.zeros_like(l_s)
        acc_s[...] = jnp.zeros_like(acc_s)

    @pl.when(st < ns)
    def _():
        flat = lambda x: x.reshape(LANE, PAGE).astype(MXU)
        s = jnp.concatenate([_dot(qb, flat(kp[0, 0])) for kp in kpages], axis=1)
        width = gp * PAGE
        expand = (_iota((njp, width), 0) == (st * width + _iota((njp, width), 1)) // SEL_BLOCK).astype(BF16)
        mexp = _dot(sel_s[...], expand)
        tail = jnp.where(st == ns - 1, blast_ref[...], 0.0)
        if gp > 1:
            tail = jnp.concatenate([jnp.zeros((rows, width - PAGE), F32), tail], axis=1)
        lg = jnp.where(mexp > 0.5, s + tail, NEG_INF)
        alpha, p = _online_update(lg, m_s, l_s)
        pv = jnp.zeros((rows, LANE), F32)
        for w in range(gp):
            pv = pv + _dot_nt(p[:, w * PAGE:(w + 1) * PAGE].astype(MXU), flat(vpages[w][0, 0]))
        acc_s[...] = alpha * acc_s[...] + pv

    @pl.when(st == ns)
    def _():
        flat = lambda x: x.reshape(LANE, PAGE).astype(MXU)
        s = _dot(qb, flat(snew_ref[0, 0])) + bnew_ref[...]
        picked = sel_s[:, nj - 1:nj].astype(F32) > 0.5
        lg = jnp.where(picked, s, NEG_INF)
        alpha, p = _online_update(lg, m_s, l_s)
        acc = alpha * acc_s[...] + _dot_nt(p.astype(MXU), flat(snew_ref[0, 1]))
        osl = acc * (1.0 / jnp.maximum(l_s[...], 1e-30))
        gt = g_ref[0]
        of = gt[:, 0:1] * oc_s[...] + gt[:, 1:2] * osl + gt[:, 2:3] * ow_s[...]
        head = _iota((rows, LANE), 0) % KVH
        of = jnp.where(head == 0, of, pltpu.roll(of, HEAD_DIM, 1))
        o_ref[0] = of[:, 0:HEAD_DIM]


def _decode_nsa(cache_t, page_table, qrows, gates, cmp2, ovt, bcmp, bwin, blast, bnew, win_t, wnew_t, snew_t, gp, nj):
    b, npg = page_table.shape
    ns = npg // gp
    rows = qrows.shape[1]
    ncp = cmp2.shape[1]
    njp = ovt.shape[0]
    stc = lambda st: jnp.minimum(st, ns - 1)
    page_spec = lambda w, typ: pl.BlockSpec((1, 1, KVH, HEAD_DIM, PAGE),
                                            lambda i, st, pt: (pt[i, stc(st) * gp + w], typ, 0, 0, 0))
    const2 = lambda a: pl.BlockSpec(a.shape, lambda i, st, pt: (0, 0))
    own5 = lambda a: pl.BlockSpec((1,) + a.shape[1:], lambda i, st, pt: (i, 0, 0, 0, 0))
    grid_spec = pltpu.PrefetchScalarGridSpec(
        num_scalar_prefetch=1,
        grid=(b, ns + 1),
        in_specs=[page_spec(w, 2) for w in range(gp)] + [page_spec(w, 3) for w in range(gp)]
                 + [pl.BlockSpec((1, rows, LANE), lambda i, st, pt: (i, 0, 0)),
                    pl.BlockSpec((1, rows, 3), lambda i, st, pt: (i, 0, 0)),
                    pl.BlockSpec((1, ncp, LANE), lambda i, st, pt: (i, 0, 0)),
                    pl.BlockSpec((1, ncp, LANE), lambda i, st, pt: (i, 0, 1)),
                    const2(ovt), const2(bcmp), const2(bwin), const2(blast), const2(bnew),
                    own5(win_t), own5(wnew_t), own5(snew_t)],
        out_specs=pl.BlockSpec((1, rows, HEAD_DIM), lambda i, st, pt: (i, 0, 0)),
        scratch_shapes=[pltpu.VMEM((rows, njp), BF16), pltpu.VMEM((rows, LANE), F32), pltpu.VMEM((rows, LANE), F32),
                        pltpu.VMEM((rows, 1), F32), pltpu.VMEM((rows, 1), F32), pltpu.VMEM((rows, LANE), F32)],
    )
    return pl.pallas_call(
        functools.partial(_dnsa_body, gp=gp, ns=ns, nj=nj, njp=njp),
        grid_spec=grid_spec,
        out_shape=jax.ShapeDtypeStruct((b, rows, HEAD_DIM), F32),
        compiler_params=_cparams(("parallel", "arbitrary")),
        name="decode_nsa",
    )(page_table, *([cache_t] * (2 * gp)), qrows, gates, cmp2, cmp2, ovt, bcmp, bwin, blast, bnew, win_t, wnew_t, snew_t)


def _merge_body(on_ref, of_ref, ga_ref, gb_ref, x_ref, wa, wb, wo, g1, b1, o_ref, *, alpha):
    a = _dot(on_ref[...].astype(MXU), wa[...])
    bb = _dot(of_ref[...].astype(MXU), wb[...])
    mix = _dot((ga_ref[...] * a + gb_ref[...] * bb).astype(MXU), wo[...])
    o_ref[...] = _layer_norm(alpha * x_ref[...] + mix, g1[...], b1[...])


def _merge(o_nsa, o_fox, sga, sgb, x2d, wa, wb, wo, g1, b1, tm, alpha):
    m, d = x2d.shape
    row = lambda n: pl.BlockSpec((tm, n), lambda i: (i, 0))
    full = lambda a: pl.BlockSpec(a.shape, lambda i: (0, 0))
    wa, wb, wo = wa.astype(MXU), wb.astype(MXU), wo.astype(MXU)
    g1, b1 = g1.reshape(1, d), b1.reshape(1, d)
    return pl.pallas_call(
        functools.partial(_merge_body, alpha=alpha),
        grid=(m // tm,),
        in_specs=[row(NSA_Q_W), row(FOX_W), row(d), row(d), row(d), full(wa), full(wb), full(wo), full(g1), full(b1)],
        out_specs=row(d),
        out_shape=jax.ShapeDtypeStruct((m, d), F32),
        compiler_params=_cparams(("parallel",)),
        name="merge_ln1",
    )(o_nsa, o_fox, sga, sgb, x2d, wa, wb, wo, g1, b1)


def _ffn_up_long_body(x_ref, prev_ref, wu, wg, cw, cb, h_ref, st_ref, carry_s, *, tiles_per_seq):
    i = pl.program_id(0)
    tm = x_ref.shape[0]
    xb = x_ref[...].astype(MXU)
    u = _dot(xb, wu[...])

    @pl.when(i % tiles_per_seq == 0)
    def _():
        carry_s[...] = prev_ref[0]

    r = _iota(u.shape, 0)
    um1 = jnp.where(r >= 1, pltpu.roll(u, 1, 0), carry_s[1:2, :])
    um2 = jnp.where(r >= 2, pltpu.roll(u, 2, 0), jnp.where(r == 0, carry_s[0:1, :], carry_s[1:2, :]))
    conv = cb[...] + cw[0:1, :] * um2 + cw[1:2, :] * um1 + cw[2:3, :] * u
    h_ref[...] = (jax.nn.gelu(conv) * _dot(xb, wg[...])).astype(h_ref.dtype)
    last2 = u[tm - 2:tm, :]
    carry_s[...] = last2
    st_ref[0] = last2


def _ffn_up_short_body(x_ref, p1_ref, p2_ref, wu, wg, cw, cb, h_ref, u_ref, *, seq):
    xb = x_ref[...].astype(MXU)
    u = _dot(xb, wu[...])
    t = _iota(u.shape, 0) % seq
    um1 = jnp.where(t >= 1, pltpu.roll(u, 1, 0), p1_ref[...])
    um2 = jnp.where(t >= 2, pltpu.roll(u, 2, 0), p2_ref[...])
    conv = cb[...] + cw[0:1, :] * um2 + cw[1:2, :] * um1 + cw[2:3, :] * u
    h_ref[...] = (jax.nn.gelu(conv) * _dot(xb, wg[...])).astype(h_ref.dtype)
    u_ref[...] = u


def _ffn_up_long(x1, conv_prev, wu, wg, cw, cb, seq, tm):
    m, d = x1.shape
    f = wu.shape[1]
    nseq = m // seq
    tps = seq // tm
    wu, wg = wu.astype(MXU), wg.astype(MXU)
    cb = cb.reshape(1, f)
    full = lambda a: pl.BlockSpec(a.shape, lambda i: (0, 0))
    return pl.pallas_call(
        functools.partial(_ffn_up_long_body, tiles_per_seq=tps),
        grid=(m // tm,),
        in_specs=[pl.BlockSpec((tm, d), lambda i: (i, 0)),
                  pl.BlockSpec((1, 2, f), lambda i: (i // tps, 0, 0)),
                  full(wu), full(wg), full(cw), full(cb)],
        out_specs=[pl.BlockSpec((tm, f), lambda i: (i, 0)),
                   pl.BlockSpec((1, 2, f), lambda i: (i // tps, 0, 0))],
        out_shape=[jax.ShapeDtypeStruct((m, f), MXU), jax.ShapeDtypeStruct((nseq, 2, f), F32)],
        scratch_shapes=[pltpu.VMEM((2, f), F32)],
        compiler_params=_cparams(("arbitrary",)),
        name="ffn_up_long",
    )(x1, conv_prev, wu, wg, cw, cb)


def _ffn_up_short(x1, conv_prev, wu, wg, cw, cb, seq):
    m, d = x1.shape
    f = wu.shape[1]
    nseq = m // seq
    wu, wg = wu.astype(MXU), wg.astype(MXU)
    cb = cb.reshape(1, f)
    zeros = jnp.zeros((nseq, seq, f), F32)
    p1 = zeros.at[:, 0].set(conv_prev[:, 1]).reshape(m, f)
    p2 = zeros.at[:, 0].set(conv_prev[:, 0]).at[:, 1].set(conv_prev[:, 1]).reshape(m, f)
    full = lambda a: pl.BlockSpec(a.shape, lambda i: (0, 0))
    h, u = pl.pallas_call(
        functools.partial(_ffn_up_short_body, seq=seq),
        grid=(1,),
        in_specs=[full(x1), full(p1), full(p2), full(wu), full(wg), full(cw), full(cb)],
        out_specs=[pl.BlockSpec((m, f), lambda i: (0, 0)), pl.BlockSpec((m, f), lambda i: (0, 0))],
        out_shape=[jax.ShapeDtypeStruct((m, f), MXU), jax.ShapeDtypeStruct((m, f), F32)],
        compiler_params=_cparams(("arbitrary",)),
        name="ffn_up_short",
    )(x1, p1, p2, wu, wg, cw, cb)
    return h, u.reshape(nseq, seq, f)[:, seq - 2:]


def _ffn_down_body(h_ref, x1_ref, p_ref, wd, wpg, wp, g2, b2, o_ref, *, alpha):
    f = _dot(h_ref[...], wd[...])
    x2 = _layer_norm(alpha * x1_ref[...] + f, g2[...], b2[...])
    gate = jax.nn.sigmoid(_dot(x2.astype(MXU), wpg[...]))
    o_ref[...] = x2 + gate * _dot(p_ref[...].astype(MXU), wp[...])


def _ffn_down(h, x1, p2d, wd, wpg, wp, g2, b2, tm, alpha):
    m, d = x1.shape
    row = lambda n: pl.BlockSpec((tm, n), lambda i: (i, 0))
    full = lambda a: pl.BlockSpec(a.shape, lambda i: (0, 0))
    wd, wpg, wp = wd.astype(MXU), wpg.astype(MXU), wp.astype(MXU)
    g2, b2 = g2.reshape(1, d), b2.reshape(1, d)
    return pl.pallas_call(
        functools.partial(_ffn_down_body, alpha=alpha),
        grid=(m // tm,),
        in_specs=[row(h.shape[1]), row(d), row(p2d.shape[1]), full(wd), full(wpg), full(wp), full(g2), full(b2)],
        out_specs=row(d),
        out_shape=jax.ShapeDtypeStruct((m, d), F32),
        compiler_params=_cparams(("parallel",)),
        name="ffn_down_ln2_ple",
    )(h, x1, p2d, wd, wpg, wp, g2, b2)


def _overlap_t(n_sel, n_sel_pad, n_c, n_c_pad):
    start = np.arange(n_c_pad)[None, :] * CMP_STRIDE
    j = np.arange(n_sel_pad)[:, None]
    ov = (start < (j + 1) * SEL_BLOCK) & (start + CMP_BLOCK > j * SEL_BLOCK)
    ov &= (np.arange(n_c_pad)[None, :] < n_c) & (j < n_sel)
    return jnp.asarray(ov, dtype=MXU)


def _block_onehot(t_len, njp):
    e = (np.arange(t_len)[:, None] // SEL_BLOCK) == np.arange(njp)[None, :]
    return jnp.asarray(e, dtype=MXU)


def _prompt_bias_idx():
    i = np.arange(QB)[:, None]
    d = i + QB - np.arange(2 * QB)[None, :]
    band = np.where(d >= 0, _bucket_np(d), -2).astype(np.int32)
    mcol = np.arange(LANE)[None, :]
    dc = i + 113 - CMP_STRIDE * mcol
    bandc = np.where((dc >= 0) & (mcol < 16), _bucket_np(dc), -2).astype(np.int32)
    return band, bandc


def _decode_bias_idx(p_len, s_new, n_c, ncp):
    rows = GRP * s_new * KVH
    r = np.arange(rows)
    g, i, h = r // (s_new * KVH), (r // KVH) % s_new, r % KVH
    hg = (h * GRP + g)[:, None]
    qpos = (p_len + i)[:, None]
    ones = np.ones((rows, 1), bool)
    n = np.arange(ncp)
    bcmp = _bias_idx(qpos - (CMP_STRIDE * n + CMP_BLOCK - 1)[None, :], hg, (n < n_c)[None, :] & ones)
    wb = min(WINDOW, p_len)
    wpos = np.concatenate([p_len - wb + np.arange(wb), p_len + np.arange(PAGE)])
    wok = np.concatenate([np.ones(wb, bool), np.arange(PAGE) < s_new])[None, :]
    dwin = qpos - wpos[None, :]
    bwin = _bias_idx(dwin, hg, wok & (dwin < WINDOW))
    blast = _bias_idx(qpos - (p_len - PAGE + np.arange(PAGE))[None, :], hg, np.ones((rows, PAGE), bool))
    bnew = _bias_idx(qpos - (p_len + np.arange(PAGE))[None, :], hg, (np.arange(PAGE) < s_new)[None, :] & ones)
    return bcmp, bwin, blast, bnew


def _pick(n, prefs):
    for p in prefs:
        if n % p == 0:
            return p
    return n


def _finish(x2d, o_nsa, o_fox, sga, sgb, p2d, conv_prev, seq, W, alpha):
    m = x2d.shape[0]
    tm = _pick(m, (256, 128))
    x1 = _merge(o_nsa, o_fox, sga, sgb, x2d, W['w_branch_a'], W['w_branch_b'], W['w_out'], W['ln1_g'], W['ln1_b'],
                tm, alpha)
    if seq >= 128:
        h, conv_state = _ffn_up_long(x1, conv_prev, W['w_ffn_up'], W['w_ffn_gate'], W['ffn_conv_w'], W['ffn_conv_b'],
                                     seq, _pick(seq, (256, 128)))
    else:
        h, conv_state = _ffn_up_short(x1, conv_prev, W['w_ffn_up'], W['w_ffn_gate'], W['ffn_conv_w'],
                                      W['ffn_conv_b'], seq)
    y = _ffn_down(h, x1, p2d, W['w_ffn_down'], W['w_ple_gate'], W['w_ple'], W['ln2_g'], W['ln2_b'], tm, alpha)
    return y, conv_state


def _prompt_layer(x, p_emb, W, rel_bias, alpha):
    b, t, d = x.shape
    m = b * t
    x2d = x.reshape(m, d)
    nq, nkv, kw, fq, fkv, sga, sgb, ng, logf = _proj(x2d, W['w_in'], W['b_forget'], _pick(m, (256, 128)))
    npg = t // PAGE
    gp = _pick(npg, (8, 4, 2, 1))
    ident = jnp.arange(b * npg, dtype=jnp.int32).reshape(b, npg)
    tile = _pick(t, (1024, 512, 256, 128))

    c, _ = _paged_cumsum(logf.reshape(b * npg, PAGE, FOX_HEADS), ident, gp)
    fq3, fkv3 = fq.reshape(b, t, FOX_W), fkv.reshape(b, t, 2 * FOX_W)
    qa, ka, va = _fox_prep(fq3, fkv3, c, tile)
    o_fox = _prompt_fox(qa, ka, va, tile, tile)

    nkv3 = nkv.reshape(b, t, 4 * NSA_KV_W)
    cmp2 = _compress(nkv3.reshape(b * npg, PAGE, 4 * NSA_KV_W), ident, W['nsa_cmp_pe'], W['nsa_cmp_w1'],
                     W['nsa_cmp_w2'], gp)
    nc = cmp2.shape[1]
    n_c = (t - CMP_BLOCK) // CMP_STRIDE + 1
    nj = t // SEL_BLOCK
    njp = -(-nj // LANE) * LANE
    band_idx, bandc_idx = _prompt_bias_idx()
    kcat, vcat = _nsa_prep(nkv3, _block_onehot(t, njp), tile)
    o_nsa = _prompt_nsa(nq.reshape(b, t, NSA_Q_W), ng.reshape(b, t, LANE), cmp2, kcat, vcat,
                        kw.reshape(b, t, 2 * NSA_KV_W), _overlap_t(nj, njp, n_c, nc),
                        _bias_lookup_heads(rel_bias, band_idx), _bias_lookup_heads(rel_bias, bandc_idx), tile)

    conv_prev = jnp.zeros((b, 2, W['w_ffn_up'].shape[1]), F32)
    y, conv_state = _finish(x2d, o_nsa.reshape(m, NSA_Q_W), o_fox.reshape(m, FOX_W), sga, sgb,
                            p_emb.reshape(m, -1), conv_prev, t, W, alpha)
    wb = min(WINDOW, t)
    state = (fkv.reshape(b, t, 2, FOX_HEADS, HEAD_DIM), logf.reshape(b, t, FOX_HEADS),
             nkv.reshape(b, t, 4, KVH, HEAD_DIM), kw.reshape(b, t, 2, KVH, HEAD_DIM)[:, t - wb:], conv_state)
    return y.reshape(b, t, d), state


def _sample_layer(x, p_emb, c_fox_kv, c_fox_logf, c_nsa, win, conv_prev, page_table, W, rel_bias, alpha):
    b, s, d = x.shape
    m = b * s
    npg = page_table.shape[1]
    p_len = npg * PAGE
    x2d = x.reshape(m, d)
    nq, nkv, kw, fq, fkv, sga, sgb, ng, logf = _proj(x2d, W['w_in'], W['b_forget'], _pick(m, (256, 128, 8)))
    gp = _pick(npg, (8, 4, 2, 1))

    def rows_last(a, lead):
        a = jnp.moveaxis(a.reshape((b, s) + lead), 1, -1)
        return jnp.pad(a, ((0, 0),) * (a.ndim - 1) + ((0, PAGE - s),))

    fox_t = jnp.transpose(c_fox_kv, (0, 2, 3, 4, 1))
    nsa_t = jnp.transpose(c_nsa, (0, 2, 3, 4, 1))
    logf_t = jnp.transpose(c_fox_logf.astype(F32), (0, 2, 1))
    win_t = jnp.transpose(win, (0, 2, 3, 4, 1))

    gp_wide = _pick(npg, (16, 8, 4, 2, 1))
    ct_all = _paged_cumsum_t(logf_t, page_table, gp_wide, rows_last(logf, (FOX_HEADS,)))
    cq = jnp.transpose(ct_all[:, :, p_len:p_len + s], (0, 2, 1)).reshape(b, s * FOX_HEADS, 1)
    own = (np.arange(FOX_W)[None, :] // HEAD_DIM) == (np.arange(s * FOX_HEADS)[:, None] % FOX_HEADS)
    qrows_f = jnp.where(jnp.asarray(own)[None], jnp.repeat(fq.reshape(b, s, FOX_W), FOX_HEADS, axis=1), 0.0)
    o_fox = _decode_fox(fox_t, page_table, qrows_f, cq, ct_all, rows_last(fkv, (2, FOX_HEADS, HEAD_DIM)), gp_wide)

    cmp2 = _compress(nsa_t, page_table, W['nsa_cmp_pe'], W['nsa_cmp_w1'], W['nsa_cmp_w2'], gp, transposed=True)
    ncp = cmp2.shape[1]
    n_c = (p_len + s - CMP_BLOCK) // CMP_STRIDE + 1
    nj = -(-(p_len + s) // SEL_BLOCK)
    njp = -(-nj // LANE) * LANE
    tables = [_bias_lookup(rel_bias, ix) for ix in _decode_bias_idx(p_len, s, n_c, ncp)]
    rows = GRP * s * KVH
    h_of_row = np.arange(rows) % KVH
    qsel = jnp.transpose(nq.reshape(b, s, KVH, GRP, HEAD_DIM), (0, 3, 1, 2, 4)).reshape(b, rows, HEAD_DIM)
    half = jnp.asarray((np.arange(LANE)[None, :] // HEAD_DIM) == h_of_row[:, None])
    qrows_n = jnp.where(half[None], jnp.concatenate([qsel, qsel], axis=-1), 0.0)
    gates = jnp.transpose(ng[:, FOX_HEADS:FOX_HEADS + KVH * GRP * 3].reshape(b, s, KVH, GRP, 3),
                          (0, 3, 1, 2, 4)).reshape(b, rows, 3)
    nkv_t = rows_last(nkv, (4, KVH, HEAD_DIM))
    o_rows = _decode_nsa(nsa_t, page_table, qrows_n, gates, cmp2, _overlap_t(nj, njp, n_c, ncp), *tables,
                         win_t, rows_last(kw, (2, KVH, HEAD_DIM)), nkv_t[:, 2:4], gp, nj)
    o_nsa = jnp.transpose(o_rows.reshape(b, GRP, s, KVH, HEAD_DIM), (0, 2, 3, 1, 4)).reshape(m, NSA_Q_W)

    y, conv_state = _finish(x2d, o_nsa, o_fox.reshape(m, FOX_W), sga, sgb, p_emb.reshape(m, -1), conv_prev, s, W,
                            alpha)
    win_all = jnp.concatenate([win, kw.reshape(b, s, 2, KVH, HEAD_DIM)], axis=1)
    state = (fkv.reshape(b, s, 2, FOX_HEADS, HEAD_DIM), logf.reshape(b, s, FOX_HEADS),
             nkv.reshape(b, s, 4, KVH, HEAD_DIM), win_all[:, s:], conv_state)
    return y.reshape(b, s, d), state


def kernel(x_prompt, x_sample, p_prompt, p_sample, cache_fox_kv, cache_fox_logf, cache_nsa_kv, state_nsa_win,
           state_ffn_conv, page_table, w_in, b_forget, nsa_cmp_pe, nsa_cmp_w1, nsa_cmp_w2, rel_bias, w_branch_a,
           w_branch_b, w_out, ln1_g, ln1_b, ln2_g, ln2_b, w_ffn_up, w_ffn_gate, ffn_conv_w, ffn_conv_b, w_ffn_down,
           w_ple, w_ple_gate):
    depth = w_in.shape[0]
    alpha = (2.0 * depth) ** 0.25
    xp, xs = x_prompt, x_sample
    st_p, st_s = [], []
    for i in range(depth):
        W = {
            'w_in': w_in[i], 'b_forget': b_forget[i], 'nsa_cmp_pe': nsa_cmp_pe[i], 'nsa_cmp_w1': nsa_cmp_w1[i],
            'nsa_cmp_w2': nsa_cmp_w2[i], 'w_branch_a': w_branch_a[i], 'w_branch_b': w_branch_b[i], 'w_out': w_out[i],
            'ln1_g': ln1_g[i], 'ln1_b': ln1_b[i], 'ln2_g': ln2_g[i], 'ln2_b': ln2_b[i], 'w_ffn_up': w_ffn_up[i],
            'w_ffn_gate': w_ffn_gate[i], 'ffn_conv_w': ffn_conv_w[i], 'ffn_conv_b': ffn_conv_b[i],
            'w_ffn_down': w_ffn_down[i], 'w_ple': w_ple[i], 'w_ple_gate': w_ple_gate[i],
        }
        xp, sp = _prompt_layer(xp, p_prompt[i], W, rel_bias, alpha)
        xs, ss = _sample_layer(xs, p_sample[i], cache_fox_kv[i], cache_fox_logf[i], cache_nsa_kv[i],
                               state_nsa_win[i], state_ffn_conv[i], page_table, W, rel_bias, alpha)
        st_p.append(sp)
        st_s.append(ss)

    def stk(lst, j):
        return jnp.stack([s[j] for s in lst])

    return (xp, xs, stk(st_p, 0), stk(st_s, 0), stk(st_p, 1), stk(st_s, 1), stk(st_p, 2), stk(st_s, 2),
            stk(st_p, 3), stk(st_s, 3), stk(st_p, 4), stk(st_s, 4))
```
